```python
import math, functools
import jax, jax.numpy as jnp
from jax import lax
import numpy as np

D_MODEL = 1024
BATCH = 4
SEQ = 4096
DEPTH = 2
DEC_BATCH = 128
DEC_SEQ = 8
PAST_LEN = 2048
PAGE_SIZE = 128

D_ATTN = D_MODEL // 2
ATTN_HEAD_DIM = 64
N_ATTN_HEADS = D_ATTN // ATTN_HEAD_DIM
ATTN_SCALE = ATTN_HEAD_DIM ** -0.5
D_REC = D_MODEL - D_ATTN
REC_EXPAND = 128
N_REC_HEADS = D_REC // REC_EXPAND
REC_DV = D_REC // N_REC_HEADS
REC_CHUNK = 64
D_FF = 4 * D_MODEL
DILATED_PATTERNS = ((128, 1), (512, 4), (2048, 16))
MAX_WINDOW = max(w for w, _ in DILATED_PATTERNS)
BAND_BLOCK = 128
NUM_BUCKETS = 32
MAX_DISTANCE = MAX_WINDOW
RMS_EPS = 1e-6
NEG_INF = -1e30
MIN_FORGET = 1e-30
IN_COLS = 4 * D_REC + 3 * D_ATTN

kernel_name = "hymba_hgrn2_dilated_swa_decoder_step"


def rms_norm(x, g):
    xf = x.astype(jnp.float32)
    y = xf * lax.rsqrt(jnp.mean(xf * xf, axis=-1, keepdims=True) + RMS_EPS)
    return (y * g.astype(jnp.float32)).astype(x.dtype)


def rel_bucket(dist):
    exact = NUM_BUCKETS // 2
    d = jnp.maximum(dist, exact).astype(jnp.float32)
    large = exact + (jnp.log(d / exact) / math.log(MAX_DISTANCE / exact) * (NUM_BUCKETS - exact)).astype(jnp.int32)
    large = jnp.minimum(large, NUM_BUCKETS - 1)
    return jnp.where(dist < exact, dist, large)


def gated_linear_recurrence(q, k, v, logf, s0, chunk):
    B, T, H, DK = q.shape
    DV = v.shape[-1]
    n = T // chunk

    def blocks(a):
        return a.reshape(B, n, chunk, H, a.shape[-1]).transpose(1, 0, 3, 2, 4)

    causal = jnp.tril(jnp.ones((chunk, chunk), dtype=bool))

    def step(state, inp):
        qc, kc, vc, gc = inp
        b = jnp.cumsum(gc, axis=2)
        diff = b[:, :, :, None, :] - b[:, :, None, :, :]
        decay = jnp.exp(jnp.where(causal[:, :, None], diff, NEG_INF))
        scores = jnp.einsum('bhtsk,bhsk->bhts', qc[:, :, :, None, :] * decay, kc)
        out = (jnp.einsum('bhts,bhsv->bhtv', scores, vc)
               + jnp.einsum('bhtk,bhkv->bhtv', qc * jnp.exp(b), state))
        b_last = b[:, :, -1:, :]
        state = (jnp.exp(b_last[:, :, 0, :, None]) * state
                 + jnp.einsum('bhsk,bhsv->bhkv', kc * jnp.exp(b_last - b), vc))
        return state, out

    s_final, out = lax.scan(step, s0, (blocks(q), blocks(k), blocks(v), blocks(logf)))
    out = out.transpose(1, 0, 3, 2, 4).reshape(B, T, H, DV)
    return out, s_final


def hgrn2_mixer(q_in, f_in, i_in, g_in, lb, g_norm, s0):
    B, T, _ = q_in.shape
    heads = lambda a: a.reshape(B, T, N_REC_HEADS, -1).astype(jnp.float32)
    z = heads(f_in)
    lbh = lb.reshape(N_REC_HEADS, REC_EXPAND)
    f = lbh + (1.0 - lbh) * jax.nn.sigmoid(z)
    logf = jnp.log(jnp.maximum(f, MIN_FORGET))
    k = (1.0 - lbh) * jax.nn.sigmoid(-z)
    q = jax.nn.silu(heads(q_in))
    v = heads(i_in)
    chunk = REC_CHUNK if T % REC_CHUNK == 0 else T
    o, s_final = gated_linear_recurrence(q, k, v, logf, s0, chunk)
    o = rms_norm(o, g_norm.reshape(N_REC_HEADS, REC_DV)) * jax.nn.silu(heads(g_in))
    return o.reshape(B, T, D_REC).astype(q_in.dtype), s_final


def dilated_band_attention(q, k, v, rel_bias, window, dil):
    B, S, H, Dh = q.shape
    taps = window // dil
    L = S // dil
    nb = -(-L // BAND_BLOCK)
    Lp = nb * BAND_BLOCK
    N = B * dil

    def to_sub(a, front):
        a = a.reshape(B, L, dil, H, Dh).transpose(0, 2, 1, 3, 4).reshape(N, L, H, Dh)
        return jnp.pad(a, ((0, 0), (front, Lp - L), (0, 0), (0, 0)))

    def band(a):
        ap = to_sub(a, BAND_BLOCK)
        prev = ap[:, :Lp].reshape(N, nb, BAND_BLOCK, H, Dh)
        cur = ap[:, BAND_BLOCK:].reshape(N, nb, BAND_BLOCK, H, Dh)
        return jnp.concatenate([prev, cur], axis=2)

    qb = to_sub(q, 0).reshape(N, nb, BAND_BLOCK, H, Dh)
    kb, vb = band(k), band(v)
    scores = jnp.einsum('nbqhd,nbkhd->nbhqk', qb, kb, preferred_element_type=jnp.float32) * ATTN_SCALE
    qi = jnp.arange(BAND_BLOCK)[:, None]
    ki = jnp.arange(2 * BAND_BLOCK)[None, :]
    tap = qi + BAND_BLOCK - ki
    bias = rel_bias[rel_bucket(jnp.maximum(tap, 0) * dil)]
    key_idx = jnp.arange(nb)[:, None, None] * BAND_BLOCK + ki[None] - BAND_BLOCK
    valid = ((tap >= 0) & (tap <= taps))[None] & (key_idx >= 0)
    scores = jnp.where(valid[None, :, None],
                       scores + jnp.transpose(bias, (2, 0, 1)).astype(jnp.float32)[None, None], NEG_INF)
    m = jnp.max(scores, axis=-1, keepdims=True)
    p = jnp.exp(scores - m)
    den = jnp.sum(p, axis=-1)
    o = jnp.einsum('nbhqk,nbkhd->nbqhd', p, vb.astype(jnp.float32)) / jnp.transpose(den, (0, 1, 3, 2))[..., None]
    lse = jnp.transpose(m[..., 0] + jnp.log(den), (0, 1, 3, 2))

    def from_sub(a):
        a = a.reshape(N, Lp, *a.shape[3:])[:, :L]
        a = a.reshape(B, dil, L, *a.shape[2:])
        return jnp.swapaxes(a, 1, 2).reshape(B, S, *a.shape[3:])

    return from_sub(o), from_sub(lse)


def dilated_gather_attention(q, k_all, v_all, rel_bias, window, dil):
    T = q.shape[1]
    n_past = k_all.shape[1] - T
    taps = window // dil
    j = jnp.arange(taps + 1)
    idx = n_past + jnp.arange(T)[:, None] - j[None, :] * dil
    valid = idx >= 0
    idx = jnp.maximum(idx, 0)
    kg = k_all[:, idx]
    vg = v_all[:, idx]
    bias = rel_bias[rel_bucket(j * dil)]
    scores = (jnp.einsum('bthd,btjhd->bthj', q, kg, preferred_element_type=jnp.float32) * ATTN_SCALE
              + bias.T.astype(jnp.float32))
    scores = jnp.where(valid[:, None, :], scores, NEG_INF)
    m = jnp.max(scores, axis=-1, keepdims=True)
    p = jnp.exp(scores - m)
    den = jnp.sum(p, axis=-1)
    o = jnp.einsum('bthj,btjhd->bthd', p, vg.astype(jnp.float32)) / den[..., None]
    return o, m[..., 0] + jnp.log(den)


def mix_patterns(outs, lses):
    w = jax.nn.softmax(jnp.stack(lses, axis=0), axis=0)
    return jnp.einsum('pbth,pbthd->bthd', w, jnp.stack(outs, axis=0))


def prompt_attention(q, k, v, rel_bias):
    res = [dilated_band_attention(q, k, v, rel_bias, w, d) for w, d in DILATED_PATTERNS]
    return mix_patterns([r[0] for r in res], [r[1] for r in res])


def sample_attention(q, k, v, ck, cv, rel_bias):
    k_all = jnp.concatenate([ck.astype(k.dtype), k], axis=1)
    v_all = jnp.concatenate([cv.astype(v.dtype), v], axis=1)
    res = [dilated_gather_attention(q, k_all, v_all, rel_bias, w, d) for w, d in DILATED_PATTERNS]
    return mix_patterns([r[0] for r in res], [r[1] for r in res])


def trunk_layer(x, attn_fn, s0, lb, w_in, w_out, w_up, w_down,
                g_mix_pre, g_mix_post, g_mlp_pre, g_mlp_post, g_rec_out, g_attn_out):
    B, T, _ = x.shape
    h = rms_norm(x, g_mix_pre)
    proj = jnp.einsum('btd,dc->btc', h, w_in)
    splits = list(np.cumsum([D_REC] * 4 + [D_ATTN] * 2))
    q_r, f_r, i_r, g_r, q_a, k_a, v_a = jnp.split(proj, splits, axis=-1)
    rec_out, s_new = hgrn2_mixer(q_r, f_r, i_r, g_r, lb, g_rec_out, s0)
    ah = lambda a: a.reshape(B, T, N_ATTN_HEADS, ATTN_HEAD_DIM)
    q_a, k_a, v_a = ah(q_a), ah(k_a), ah(v_a)
    attn = attn_fn(q_a, k_a, v_a)
    attn = rms_norm(attn, g_attn_out.reshape(N_ATTN_HEADS, ATTN_HEAD_DIM)).reshape(B, T, D_ATTN).astype(x.dtype)
    mixed = jnp.einsum('btc,cd->btd', jnp.concatenate([rec_out, attn], axis=-1), w_out)
    x = x + rms_norm(mixed, g_mix_post)
    h = rms_norm(x, g_mlp_pre)
    u = jnp.square(jax.nn.relu(jnp.einsum('btd,df->btf', h, w_up)))
    x = x + rms_norm(jnp.einsum('btf,fd->btd', u, w_down), g_mlp_post)
    return x, s_new, k_a, v_a


def setup_inputs(seed: int = 0) -> dict:
    key = jax.random.key(seed)
    ks = jax.random.split(key, 18)
    f32 = jnp.float32
    nrm = lambda k, shape, scale: scale * jax.random.normal(k, shape, f32)
    win = min(MAX_WINDOW, PAST_LEN)
    return {
        "x_prompt": nrm(ks[0], (BATCH, SEQ, D_MODEL), 1.0),
        "x_sample": nrm(ks[1], (DEC_BATCH, DEC_SEQ, D_MODEL), 1.0),
        "state_hgrn": nrm(ks[2], (DEPTH, DEC_BATCH, N_REC_HEADS, REC_EXPAND, REC_DV), 0.5),
        "cache_k": nrm(ks[3], (DEPTH, DEC_BATCH, win, N_ATTN_HEADS, ATTN_HEAD_DIM), 1.0),
        "cache_v": nrm(ks[4], (DEPTH, DEC_BATCH, win, N_ATTN_HEADS, ATTN_HEAD_DIM), 1.0),
        "rel_bias": nrm(ks[5], (NUM_BUCKETS, N_ATTN_HEADS), 0.5),
        "lb_raw": nrm(ks[6], (DEPTH, D_REC), 0.5),
        "w_in": nrm(ks[7], (DEPTH, D_MODEL, IN_COLS), D_MODEL ** -0.5),
        "w_out": nrm(ks[8], (DEPTH, D_MODEL, D_MODEL), D_MODEL ** -0.5),
        "w_up": nrm(ks[9], (DEPTH, D_MODEL, D_FF), D_MODEL ** -0.5),
        "w_down": nrm(ks[10], (DEPTH, D_FF, D_MODEL), D_FF ** -0.5),
        "g_mix_pre": 1.0 + nrm(ks[11], (DEPTH, D_MODEL), 0.02),
        "g_mix_post": 1.0 + nrm(ks[12], (DEPTH, D_MODEL), 0.02),
        "g_mlp_pre": 1.0 + nrm(ks[13], (DEPTH, D_MODEL), 0.02),
        "g_mlp_post": 1.0 + nrm(ks[14], (DEPTH, D_MODEL), 0.02),
        "g_rec_out": 1.0 + nrm(ks[15], (DEPTH, D_REC), 0.02),
        "g_attn_out": 1.0 + nrm(ks[16], (DEPTH, D_ATTN), 0.02),
    }


def reference(x_prompt, x_sample, state_hgrn, cache_k, cache_v, rel_bias, lb_raw, w_in, w_out, w_up, w_down,
              g_mix_pre, g_mix_post, g_mlp_pre, g_mlp_post, g_rec_out, g_attn_out):
    lb_soft = jax.nn.softmax(lb_raw.astype(jnp.float32), axis=0)
    lower_bounds = jnp.cumsum(lb_soft, axis=0) - lb_soft[0]
    keep = min(MAX_WINDOW, x_prompt.shape[1])
    p_attn = functools.partial(prompt_attention, rel_bias=rel_bias)
    yp, ys = x_prompt, x_sample
    rec_p, k_p, v_p, rec_s, k_s, v_s = [], [], [], [], [], []
    for l in range(DEPTH):
        lw = (w_in[l], w_out[l], w_up[l], w_down[l], g_mix_pre[l], g_mix_post[l],
              g_mlp_pre[l], g_mlp_post[l], g_rec_out[l], g_attn_out[l])
        s0 = jnp.zeros((yp.shape[0], N_REC_HEADS, REC_EXPAND, REC_DV), jnp.float32)
        yp, sp, kp, vp = trunk_layer(yp, p_attn, s0, lower_bounds[l], *lw)
        rec_p.append(sp)
        k_p.append(kp[:, -keep:])
        v_p.append(vp[:, -keep:])
        s_attn = functools.partial(sample_attention, ck=cache_k[l], cv=cache_v[l], rel_bias=rel_bias)
        ys, ss, ksm, vsm = trunk_layer(ys, s_attn, state_hgrn[l].astype(jnp.float32), lower_bounds[l], *lw)
        rec_s.append(ss)
        k_s.append(ksm)
        v_s.append(vsm)
    return (yp, ys, jnp.stack(rec_p), jnp.stack(k_p), jnp.stack(v_p),
            jnp.stack(rec_s), jnp.stack(k_s), jnp.stack(v_s))
```

```python
import functools
import math

import numpy as np
import jax
import jax.numpy as jnp
from jax import lax
from jax.experimental import pallas as pl
from jax.experimental.pallas import tpu as pltpu

F32 = jnp.float32
BF16 = jnp.bfloat16

ATTN_HEAD_DIM = 64
REC_EXPAND = 128
REC_DV = 128
DILATED_PATTERNS = ((128, 1), (512, 4), (2048, 16))
MAX_WINDOW = max(w for w, _ in DILATED_PATTERNS)
BAND_BLOCK = 128
NUM_BUCKETS = 32
MAX_DISTANCE = MAX_WINDOW
RMS_EPS = 1e-6
NEG_INF = -1e30
MIN_FORGET = 1e-30

LANES = 128
CHUNK = 128
VMEM_LIMIT = 56 * 1024 * 1024


def _dot(a, b):
    return jnp.dot(a, b, preferred_element_type=F32)


def _dot_nt(a, b):
    return lax.dot_general(a, b, (((1,), (1,)), ((), ())), preferred_element_type=F32)


def _split3(x):
    hi = x.astype(BF16)
    r = x - hi.astype(F32)
    mid = r.astype(BF16)
    lo = (r - mid.astype(F32)).astype(BF16)
    return jnp.concatenate([hi, mid, lo], axis=1)


def _sum3(y):
    n = y.shape[1] // 3
    return y[:, :n] + y[:, n:2 * n] + y[:, 2 * n:]


def _rms(x, g):
    ms = jnp.mean(x * x, axis=-1, keepdims=True)
    return x * lax.rsqrt(ms + RMS_EPS) * g


def _rel_bucket_np(dist):
    exact = NUM_BUCKETS // 2
    d = np.maximum(dist, exact).astype(np.float64)
    large = exact + (np.log(d / exact) / math.log(MAX_DISTANCE / exact) * (NUM_BUCKETS - exact)).astype(np.int64)
    large = np.minimum(large, NUM_BUCKETS - 1)
    return np.where(dist < exact, dist, large).astype(np.int32)


def _band_tables():
    qi = np.arange(BAND_BLOCK)[:, None]
    ki = np.arange(2 * BAND_BLOCK)[None, :]
    idx, add = [], []
    for w, d in DILATED_PATTERNS:
        taps = w // d
        tap = qi + BAND_BLOCK - ki
        valid = (tap >= 0) & (tap <= taps)
        bucket = _rel_bucket_np(np.maximum(tap, 0) * d)
        for first in (False, True):
            v = valid & (ki >= BAND_BLOCK) if first else valid
            idx.append(bucket)
            add.append(np.where(v, 0.0, NEG_INF))
    return np.stack(idx).astype(np.int32), np.stack(add).astype(np.float32)


def _sample_tables(n_past, t_new):
    t = np.arange(t_new)[:, None]
    col = np.arange(n_past + LANES)[None, :]
    dist = np.where(col < n_past, n_past + t - col, t - (col - n_past))
    ok = (dist >= 0) & ((col < n_past) | (col - n_past < t_new))
    cnt = np.zeros(dist.shape, np.int64)
    for w, d in DILATED_PATTERNS:
        cnt += (ok & (dist % d == 0) & (dist <= w)).astype(np.int64)
    idx = _rel_bucket_np(np.maximum(dist, 0))
    add = np.where(cnt > 0, np.log(np.maximum(cnt, 1)), NEG_INF)
    return idx[None].astype(np.int32), add[None].astype(np.float32)


def _levels(seg):
    out, h = [], seg // 2
    while h >= 1:
        out.append(h)
        h //= 2
    return out


def _hgrn_tables(seg):
    r = np.arange(CHUNK)
    same_seg = (r[:, None] // seg) == (r[None, :] // seg)
    tri = (same_seg & (r[None, :] <= r[:, None])).astype(np.float32)
    eye = np.eye(CHUNK, dtype=np.float32)
    blocks = []
    lv = np.full((CHUNK, CHUNK), -1, np.int32)
    levels = _levels(seg)
    for i, h in enumerate(levels):
        ref = r - (r % (2 * h)) + h - 1
        g = np.zeros((CHUNK, CHUNK), np.float32)
        g[r, ref] = 1.0
        second = (r % (2 * h)) >= h
        sign = np.where(second, 1.0, -1.0)[:, None].astype(np.float32)
        blocks.append(sign * (eye - g))
        pair = ((r[:, None] // (2 * h)) == (r[None, :] // (2 * h))) & second[:, None] & (~second)[None, :]
        lv[pair] = i
    lv[r, r] = len(levels)
    seg_end = r - (r % seg) + seg - 1
    gl = np.zeros((CHUNK, CHUNK), np.float32)
    gl[r, seg_end] = 1.0
    blocks.append(gl - eye)
    blocks.append(gl)
    return tri, np.concatenate(blocks, axis=0), lv


def _bias_kernel(rb_ref, idx_ref, add_ref, out_ref):
    h = pl.program_id(1)
    idx = idx_ref[0]
    acc = add_ref[0]
    for b in range(NUM_BUCKETS):
        acc = acc + jnp.where(idx == b, rb_ref[b, h], 0.0)
    out_ref[0, 0] = acc


def _expand_bias(rel_bias, idx, add):
    n, r, c = idx.shape
    n_heads = rel_bias.shape[1]
    return pl.pallas_call(
        _bias_kernel,
        grid=(n, n_heads),
        in_specs=[
            pl.BlockSpec(memory_space=pltpu.SMEM),
            pl.BlockSpec((1, r, c), lambda i, h: (i, 0, 0)),
            pl.BlockSpec((1, r, c), lambda i, h: (i, 0, 0)),
        ],
        out_specs=pl.BlockSpec((1, 1, r, c), lambda i, h: (i, h, 0, 0)),
        out_shape=jax.ShapeDtypeStruct((n, n_heads, r, c), F32),
        name="bias_expand",
    )(rel_bias.astype(F32), jnp.asarray(idx), jnp.asarray(add))


def _inproj_kernel(x_ref, g_ref, w_ref, lbraw_ref,
                   qr_ref, kr_ref, vr_ref, lf_ref, gate_ref,
                   qpm_ref, kpm_ref, vpm_ref, knat_ref, vnat_ref, *, layer, d_rec, d_attn):
    h = _rms(x_ref[...], g_ref[...]).astype(BF16)

    raw = lbraw_ref[...]
    e = jnp.exp(raw - jnp.max(raw, axis=0, keepdims=True))
    soft = e / jnp.sum(e, axis=0, keepdims=True)
    cum = soft[0:1]
    for i in range(1, layer + 1):
        cum = cum + soft[i:i + 1]
    lb = cum - soft[0:1]

    def proj(c0, width):
        return _dot(h, w_ref[:, c0:c0 + width])

    p = proj(0, d_rec)
    qr_ref[...] = p * jax.nn.sigmoid(p)
    z = proj(d_rec, d_rec)
    f = lb + (1.0 - lb) * jax.nn.sigmoid(z)
    lf_ref[...] = jnp.log(jnp.maximum(f, MIN_FORGET))
    kr_ref[...] = (1.0 - lb) * jax.nn.sigmoid(-z)
    vr_ref[...] = proj(2 * d_rec, d_rec)
    p = proj(3 * d_rec, d_rec)
    gate_ref[...] = p * jax.nn.sigmoid(p)

    c0 = 4 * d_rec
    qa = proj(c0, d_attn) * (ATTN_HEAD_DIM ** -0.5)
    ka = proj(c0 + d_attn, d_attn)
    va = proj(c0 + 2 * d_attn, d_attn)
    knat_ref[...] = ka
    vnat_ref[...] = va
    for pr in range(d_attn // LANES):
        sl = slice(pr * LANES, (pr + 1) * LANES)
        qpm_ref[pr] = qa[:, sl]
        kpm_ref[pr] = ka[:, sl]
        vpm_ref[pr] = va[:, sl]


def _inproj(x2d, g, w_bf, lb_raw, layer, tm):
    t, dm = x2d.shape
    d_rec = lb_raw.shape[1]
    d_attn = (w_bf.shape[1] - 4 * d_rec) // 3
    n_pair = d_attn // LANES
    nat = lambda wd: pl.BlockSpec((tm, wd), lambda i: (i, 0))
    pm = pl.BlockSpec((n_pair, tm, LANES), lambda i: (0, i, 0))
    full = lambda a: pl.BlockSpec(a.shape, lambda i: (0,) * a.ndim)
    f32 = lambda *s: jax.ShapeDtypeStruct(s, F32)
    return pl.pallas_call(
        functools.partial(_inproj_kernel, layer=layer, d_rec=d_rec, d_attn=d_attn),
        grid=(t // tm,),
        in_specs=[nat(dm), full(g), full(w_bf), full(lb_raw)],
        out_specs=[nat(d_rec)] * 5 + [pm] * 3 + [nat(d_attn)] * 2,
        out_shape=[f32(t, d_rec)] * 5 + [f32(n_pair, t, LANES)] * 3 + [f32(t, d_attn)] * 2,
        compiler_params=pltpu.CompilerParams(dimension_semantics=("parallel",), vmem_limit_bytes=VMEM_LIMIT),
        name="inproj",
    )(x2d, g, w_bf, lb_raw)


def _hgrn_kernel(q_ref, k_ref, v_ref, lf_ref, gate_ref, s0_ref, gn_ref, tri_ref, gst_ref, lv_ref,
                 o_ref, sout_ref, st_ref, *, seg, n_chunks):
    nseg = CHUNK // seg
    seg_shift = seg.bit_length() - 1
    levels = _levels(seg)
    nl = len(levels)
    ti = pl.program_id(2)

    @pl.when(ti == 0)
    def _():
        st_ref[...] = s0_ref[:, 0]

    rowi = lax.broadcasted_iota(jnp.int32, (CHUNK, CHUNK), 0)
    coli = lax.broadcasted_iota(jnp.int32, (CHUNK, CHUNK), 1)
    tri = tri_ref[...]
    gst = gst_ref[...]
    lv = lv_ref[...]
    gn = gn_ref[...]

    def chunk(c, carry):
        rows = pl.ds(pl.multiple_of(c * CHUNK, CHUNK), CHUNK)
        q = q_ref[rows, :]
        k = k_ref[rows, :]
        v = v_ref[rows, :]
        b = _sum3(_dot(tri, _split3(lf_ref[rows, :])))
        e = _sum3(_dot(gst, _split3(b)))
        vb = v.astype(BF16)

        a = jnp.zeros((CHUNK, CHUNK), F32)
        for i, h in enumerate(levels):
            second = (rowi & (2 * h - 1)) >= h
            x = (jnp.where(second, q, k) * jnp.exp(e[i * CHUNK:(i + 1) * CHUNK])).astype(BF16)
            a = jnp.where(lv == i, _dot_nt(x, x), a)
        a = jnp.where(lv == nl, _dot_nt(q.astype(BF16), k.astype(BF16)), a)
        o = _dot(a.astype(BF16), vb)

        qe = (q * jnp.exp(b)).astype(BF16)
        kd_t = (k * jnp.exp(e[nl * CHUNK:(nl + 1) * CHUNK])).T
        dec_t = jnp.exp(e[(nl + 1) * CHUNK:(nl + 2) * CHUNK]).T
        for j in range(nseg):
            s_old = st_ref[j]
            oj = _dot(qe, s_old.astype(BF16))
            kd_j = kd_t
            if nseg > 1:
                oj = jnp.where((rowi >> seg_shift) == j, oj, 0.0)
                kd_j = jnp.where((coli >> seg_shift) == j, kd_t, 0.0)
            o = o + oj
            col = j * seg + seg - 1
            st_ref[j] = s_old * dec_t[:, col:col + 1] + _dot(kd_j.astype(BF16), vb)

        o_ref[rows, :] = (_rms(o, gn) * gate_ref[rows, :]).astype(o_ref.dtype)
        return carry

    lax.fori_loop(0, n_chunks, chunk, 0)

    @pl.when(ti == pl.num_programs(2) - 1)
    def _():
        sout_ref[:, 0] = st_ref[...]


def _hgrn(qr, kr, vr, lf, gate, s0, g_norm, seg, rows_per_step, steps_per_seq):
    t, d_rec = qr.shape
    n_heads = d_rec // LANES
    nseg = CHUNK // seg
    n_blocks = s0.shape[0] // nseg
    tri, gst, lv = _hgrn_tables(seg)
    tok = pl.BlockSpec((rows_per_step, LANES), lambda bi, h, ti: (bi * steps_per_seq + ti, h))
    st = pl.BlockSpec((nseg, 1, REC_EXPAND, REC_DV), lambda bi, h, ti: (bi, h, 0, 0))
    full = lambda a: pl.BlockSpec(a.shape, lambda bi, h, ti: (0,) * a.ndim)
    tri_j, gst_j, lv_j = jnp.asarray(tri, BF16), jnp.asarray(gst, BF16), jnp.asarray(lv)
    return pl.pallas_call(
        functools.partial(_hgrn_kernel, seg=seg, n_chunks=rows_per_step // CHUNK),
        grid=(n_blocks, n_heads, steps_per_seq),
        in_specs=[tok] * 5 + [st, pl.BlockSpec((1, LANES), lambda bi, h, ti: (0, h)),
                              full(tri_j), full(gst_j), full(lv_j)],
        out_specs=[tok, st],
        out_shape=[jax.ShapeDtypeStruct((t, d_rec), BF16), jax.ShapeDtypeStruct(s0.shape, F32)],
        scratch_shapes=[pltpu.VMEM((nseg, REC_EXPAND, REC_DV), F32)],
        compiler_params=pltpu.CompilerParams(
            dimension_semantics=("parallel", "parallel", "arbitrary"), vmem_limit_bytes=VMEM_LIMIT),
        name="hgrn",
    )(qr, kr, vr, lf, gate, s0, g_norm, tri_j, gst_j, lv_j)


def _head_softmax_pv(q2, k2, v2, bias_pair, lo_half):
    outs, lses = [], []
    for hh in range(2):
        keep = lo_half if hh == 0 else jnp.logical_not(lo_half)
        qm = jnp.where(keep, q2, 0.0).astype(BF16)
        s = _dot_nt(qm, k2) + bias_pair(hh)
        m = jnp.max(s, axis=-1, keepdims=True)
        p = jnp.exp(s - m)
        l = jnp.sum(p, axis=-1, keepdims=True)
        outs.append(_dot(p.astype(BF16), v2) / l)
        lses.append(m + jnp.log(l))
    o2 = jnp.where(lo_half, outs[0], outs[1])
    l2 = jnp.where(lo_half, lses[0], lses[1])
    return o2, l2


def _prompt_attn_kernel(q_ref, k_ref, v_ref, bias_ref, g_ref, bd_ref, out_ref, oacc_ref, lacc_ref, *, seq):
    lane = lax.broadcasted_iota(jnp.int32, (1, LANES), 1)
    lo_half = lane < ATTN_HEAD_DIM
    bb = BAND_BLOCK

    for pi, (_, d) in enumerate(DILATED_PATTERNS):
        nb = seq // (d * bb)

        def first_block(r, carry, pi=pi, d=d):
            qsl = pl.ds(r, bb, stride=d)
            k2 = k_ref[0, qsl, :].astype(BF16)
            v2 = v_ref[0, qsl, :].astype(BF16)
            o2, l2 = _head_softmax_pv(q_ref[0, qsl, :], k2, v2,
                                      lambda hh: bias_ref[pi, 1, hh, :, bb:2 * bb], lo_half)
            oacc_ref[pi, qsl, :] = o2
            lacc_ref[pi, qsl, :] = l2
            return carry

        lax.fori_loop(0, d, first_block, 0)

        def later_block(i, r, pi=pi, d=d):
            base = r + i * (d * bb)
            qsl = pl.ds(base, bb, stride=d)
            ksl = pl.ds(base - d * bb, 2 * bb, stride=d)
            k2 = k_ref[0, ksl, :].astype(BF16)
            v2 = v_ref[0, ksl, :].astype(BF16)
            o2, l2 = _head_softmax_pv(q_ref[0, qsl, :], k2, v2,
                                      lambda hh: bias_ref[pi, 0, hh], lo_half)
            oacc_ref[pi, qsl, :] = o2
            lacc_ref[pi, qsl, :] = l2
            return r

        def residue(r, carry, nb=nb, later_block=later_block):
            lax.fori_loop(1, nb, later_block, r)
            return carry

        if nb > 1:
            lax.fori_loop(0, d, residue, 0)

    g = g_ref[...]
    bd = bd_ref[...]

    def merge(i, carry):
        rows = pl.ds(pl.multiple_of(i * bb, bb), bb)
        ls = [lacc_ref[pi, rows, :] for pi in range(len(DILATED_PATTERNS))]
        mx = functools.reduce(jnp.maximum, ls)
        ws = [jnp.exp(l - mx) for l in ls]
        den = functools.reduce(lambda a, b: a + b, ws)
        num = functools.reduce(lambda a, b: a + b,
                               [w * oacc_ref[pi, rows, :] for pi, w in enumerate(ws)])
        o = num / den
        ms = _dot((o * o).astype(BF16), bd)
        out_ref[0, rows, :] = (o * lax.rsqrt(ms + RMS_EPS) * g).astype(out_ref.dtype)
        return carry

    lax.fori_loop(0, seq // bb, merge, 0)


def _head_mean_matrix(width):
    r = np.arange(width)
    return jnp.asarray(((r[:, None] // ATTN_HEAD_DIM) == (r[None, :] // ATTN_HEAD_DIM)) / ATTN_HEAD_DIM, BF16)


def _prompt_attention(qpm, kpm, vpm, band_bias, g_attn, n_seq, seq):
    n_pair = qpm.shape[0]
    n_pat = len(DILATED_PATTERNS)
    for _, d in DILATED_PATTERNS:
        assert seq % (d * BAND_BLOCK) == 0
    tok = pl.BlockSpec((1, seq, LANES), lambda p, b: (p, b, 0))
    bd = _head_mean_matrix(LANES)
    return pl.pallas_call(
        functools.partial(_prompt_attn_kernel, seq=seq),
        grid=(n_pair, n_seq),
        in_specs=[tok, tok, tok,
                  pl.BlockSpec((n_pat, 2, 2, BAND_BLOCK, 2 * BAND_BLOCK), lambda p, b: (0, 0, p, 0, 0)),
                  pl.BlockSpec((1, LANES), lambda p, b: (0, p)),
                  pl.BlockSpec(bd.shape, lambda p, b: (0, 0))],
        out_specs=tok,
        out_shape=jax.ShapeDtypeStruct(qpm.shape, BF16),
        scratch_shapes=[pltpu.VMEM((n_pat, seq, LANES), F32), pltpu.VMEM((n_pat, seq, LANES), F32)],
        compiler_params=pltpu.CompilerParams(
            dimension_semantics=("parallel", "parallel"), vmem_limit_bytes=VMEM_LIMIT),
        name="prompt_attn",
    )(qpm, kpm, vpm, band_bias, g_attn, bd)


def _sample_attn_kernel(q_ref, kn_ref, vn_ref, ck_ref, cv_ref, bias_ref, g_ref, bd_ref, o_ref, *, n_heads):
    t_new, width = q_ref.shape[1], q_ref.shape[2]
    n_past = ck_ref.shape[1]
    rows = n_heads * t_new
    own = (lax.broadcasted_iota(jnp.int32, (rows, width), 0) // t_new
           == lax.broadcasted_iota(jnp.int32, (rows, width), 1) // ATTN_HEAD_DIM)
    qbd = jnp.where(own, jnp.concatenate([q_ref[0]] * n_heads, axis=0), 0.0).astype(BF16)

    pad = jnp.zeros((LANES - t_new, width), F32)
    kn = jnp.concatenate([kn_ref[0], pad], axis=0).astype(BF16)
    vn = jnp.concatenate([vn_ref[0], pad], axis=0).astype(BF16)
    s_c = _dot_nt(qbd, ck_ref[0].astype(BF16)) + bias_ref[:, :n_past]
    s_n = _dot_nt(qbd, kn) + bias_ref[:, n_past:]
    m = jnp.maximum(jnp.max(s_c, axis=-1, keepdims=True), jnp.max(s_n, axis=-1, keepdims=True))
    p_c = jnp.exp(s_c - m)
    p_n = jnp.exp(s_n - m)
    l = jnp.sum(p_c, axis=-1, keepdims=True) + jnp.sum(p_n, axis=-1, keepdims=True)
    acc = _dot(p_c.astype(BF16), cv_ref[0].astype(BF16)) + _dot(p_n.astype(BF16), vn)
    o = jnp.where(own, acc / l, 0.0)
    ms = _dot((o * o).astype(BF16), bd_ref[...])
    o = o * lax.rsqrt(ms + RMS_EPS) * g_ref[...]
    out = o[0:t_new]
    for h in range(1, n_heads):
        out = out + o[h * t_new:(h + 1) * t_new]
    o_ref[0] = out


def _sample_attention(q3, kn3, vn3, cache_k, cache_v, bias, g_attn):
    n_seq, t_new, width = q3.shape
    n_past = cache_k.shape[1]
    n_heads = width // ATTN_HEAD_DIM
    bd = _head_mean_matrix(width)
    tok = pl.BlockSpec((1, t_new, width), lambda b: (b, 0, 0))
    cache = pl.BlockSpec((1, n_past, width), lambda b: (b, 0, 0))
    full = lambda a: pl.BlockSpec(a.shape, lambda b: (0,) * a.ndim)
    return pl.pallas_call(
        functools.partial(_sample_attn_kernel, n_heads=n_heads),
        grid=(n_seq,),
        in_specs=[tok, tok, tok, cache, cache, full(bias), full(g_attn), full(bd)],
        out_specs=tok,
        out_shape=jax.ShapeDtypeStruct(q3.shape, F32),
        compiler_params=pltpu.CompilerParams(dimension_semantics=("parallel",), vmem_limit_bytes=VMEM_LIMIT),
        name="sample_attn",
    )(q3, kn3, vn3, cache_k, cache_v, bias, g_attn, bd)


def _mlp_kernel(x_ref, rec_ref, attn_ref, wout_ref, wup_ref, wdn_ref, gpost_ref, gpre_ref, gmpost_ref,
                y_ref, *, ff_chunk):
    d_rec = rec_ref.shape[1]
    mixed = _dot(rec_ref[...].astype(BF16), wout_ref[0:d_rec, :])
    for p in range(attn_ref.shape[0]):
        r0 = d_rec + p * LANES
        mixed = mixed + _dot(attn_ref[p].astype(BF16), wout_ref[r0:r0 + LANES, :])
    x1 = x_ref[...] + _rms(mixed, gpost_ref[...])
    h = _rms(x1, gpre_ref[...]).astype(BF16)
    acc = jnp.zeros(x1.shape, F32)
    for c0 in range(0, wup_ref.shape[1], ff_chunk):
        u = jnp.square(jnp.maximum(_dot(h, wup_ref[:, c0:c0 + ff_chunk]), 0.0))
        acc = acc + _dot(u.astype(BF16), wdn_ref[c0:c0 + ff_chunk, :])
    y_ref[...] = x1 + _rms(acc, gmpost_ref[...])


def _mlp(x2d, rec, attn_pm, wout, wup, wdn, g_post, g_pre, g_mpost, tm):
    t, dm = x2d.shape
    n_pair = attn_pm.shape[0]
    row = lambda wd: pl.BlockSpec((tm, wd), lambda i: (i, 0))
    once = lambda a: pl.BlockSpec(a.shape, lambda i: (0,) * a.ndim, pipeline_mode=pl.Buffered(1))
    return pl.pallas_call(
        functools.partial(_mlp_kernel, ff_chunk=1024),
        grid=(t // tm,),
        in_specs=[row(dm), row(rec.shape[1]), pl.BlockSpec((n_pair, tm, LANES), lambda i: (0, i, 0)),
                  once(wout), once(wup), once(wdn), once(g_post), once(g_pre), once(g_mpost)],
        out_specs=row(dm),
        out_shape=jax.ShapeDtypeStruct((t, dm), F32),
        compiler_params=pltpu.CompilerParams(dimension_semantics=("parallel",), vmem_limit_bytes=VMEM_LIMIT),
        name="mlp",
    )(x2d, rec, attn_pm, wout, wup, wdn, g_post, g_pre, g_mpost)


def kernel(x_prompt, x_sample, state_hgrn, cache_k, cache_v, rel_bias, lb_raw, w_in, w_out, w_up, w_down,
           g_mix_pre, g_mix_post, g_mlp_pre, g_mlp_post, g_rec_out, g_attn_out):
    n_p, seq, dm = x_prompt.shape
    n_s, t_new, _ = x_sample.shape
    depth = w_in.shape[0]
    n_past = cache_k.shape[2]
    n_heads, dh = cache_k.shape[3], cache_k.shape[4]
    d_attn = n_heads * dh
    d_rec = lb_raw.shape[1]
    n_rec_heads = d_rec // REC_EXPAND
    keep = min(MAX_WINDOW, seq)
    tm = 512
    tm_s = min(tm, n_s * t_new)
    assert dh == ATTN_HEAD_DIM and CHUNK % t_new == 0 and seq % tm == 0
    assert (n_s * t_new) % CHUNK == 0 and (n_s * t_new) % tm_s == 0

    band_idx, band_add = _band_tables()
    band_bias = _expand_bias(rel_bias, band_idx, band_add)
    band_bias = band_bias.reshape(len(DILATED_PATTERNS), 2, n_heads, BAND_BLOCK, 2 * BAND_BLOCK)
    s_idx, s_add = _sample_tables(n_past, t_new)
    sample_bias = _expand_bias(rel_bias, s_idx, s_add).reshape(n_heads * t_new, n_past + LANES)

    w_in_b, w_out_b = w_in.astype(BF16), w_out.astype(BF16)
    w_up_b, w_dn_b = w_up.astype(BF16), w_down.astype(BF16)
    row = lambda a, l: a[l][None, :]

    yp = x_prompt.reshape(n_p * seq, dm)
    ys = x_sample.reshape(n_s * t_new, dm)
    zeros_state = jnp.zeros((n_p, n_rec_heads, REC_EXPAND, REC_DV), F32)
    rec_p, k_p, v_p, rec_s, k_s, v_s = [], [], [], [], [], []
    for l in range(depth):
        g_pre, g_rec, g_att = row(g_mix_pre, l), row(g_rec_out, l), row(g_attn_out, l)
        post = (row(g_mix_post, l), row(g_mlp_pre, l), row(g_mlp_post, l))

        qr, kr, vr, lf, gate, qpm, kpm, vpm, knat, vnat = _inproj(yp, g_pre, w_in_b[l], lb_raw, l, tm)
        rec, s_new = _hgrn(qr, kr, vr, lf, gate, zeros_state, g_rec, CHUNK, tm, seq // tm)
        attn = _prompt_attention(qpm, kpm, vpm, band_bias, g_att, n_p, seq)
        yp = _mlp(yp, rec, attn, w_out_b[l], w_up_b[l], w_dn_b[l], *post, tm)
        rec_p.append(s_new)
        k_p.append(knat.reshape(n_p, seq, n_heads, dh)[:, -keep:])
        v_p.append(vnat.reshape(n_p, seq, n_heads, dh)[:, -keep:])

        qr, kr, vr, lf, gate, qpm, kpm, vpm, knat, vnat = _inproj(ys, g_pre, w_in_b[l], lb_raw, l, tm_s)
        rec, s_new = _hgrn(qr, kr, vr, lf, gate, state_hgrn[l].astype(F32), g_rec, t_new, CHUNK, 1)
        q3 = qpm.transpose(1, 0, 2).reshape(n_s, t_new, d_attn)
        attn = _sample_attention(q3, knat.reshape(n_s, t_new, d_attn), vnat.reshape(n_s, t_new, d_attn),
                                 cache_k[l].reshape(n_s, n_past, d_attn), cache_v[l].reshape(n_s, n_past, d_attn),
                                 sample_bias, g_att)
        attn_pm = attn.reshape(n_s * t_new, d_attn // LANES, LANES).transpose(1, 0, 2)
        ys = _mlp(ys, rec, attn_pm, w_out_b[l], w_up_b[l], w_dn_b[l], *post, tm_s)
        rec_s.append(s_new)
        k_s.append(knat.reshape(n_s, t_new, n_heads, dh))
        v_s.append(vnat.reshape(n_s, t_new, n_heads, dh))

    return (yp.reshape(n_p, seq, dm), ys.reshape(n_s, t_new, dm), jnp.stack(rec_p), jnp.stack(k_p),
            jnp.stack(v_p), jnp.stack(rec_s), jnp.stack(k_s), jnp.stack(v_s))
```

```python
import functools
import math

import numpy as np
import jax
import jax.numpy as jnp
from jax import lax
from jax.experimental import pallas as pl
from jax.experimental.pallas import tpu as pltpu

F32 = jnp.float32
BF16 = jnp.bfloat16

ATTN_HEAD_DIM = 64
REC_EXPAND = 128
REC_DV = 128
DILATED_PATTERNS = ((128, 1), (512, 4), (2048, 16))
MAX_WINDOW = max(w for w, _ in DILATED_PATTERNS)
BAND_BLOCK = 128
NUM_BUCKETS = 32
MAX_DISTANCE = MAX_WINDOW
RMS_EPS = 1e-6
NEG_INF = -1e30
MIN_FORGET = 1e-30

LANES = 128
CHUNK = 128
VMEM_LIMIT = 56 * 1024 * 1024


def _dot(a, b):
    return jnp.dot(a, b, preferred_element_type=F32)


def _dot_nt(a, b):
    return lax.dot_general(a, b, (((1,), (1,)), ((), ())), preferred_element_type=F32)


def _split2(x):
    hi = x.astype(BF16)
    lo = (x - hi.astype(F32)).astype(BF16)
    return jnp.concatenate([hi, lo], axis=1)


def _sum2(y):
    n = y.shape[1] // 2
    return y[:, :n] + y[:, n:]


def _rms(x, g):
    ms = jnp.mean(x * x, axis=-1, keepdims=True)
    return x * lax.rsqrt(ms + RMS_EPS) * g


def _rel_bucket_np(dist):
    exact = NUM_BUCKETS // 2
    d = np.maximum(dist, exact).astype(np.float64)
    large = exact + (np.log(d / exact) / math.log(MAX_DISTANCE / exact) * (NUM_BUCKETS - exact)).astype(np.int64)
    large = np.minimum(large, NUM_BUCKETS - 1)
    return np.where(dist < exact, dist, large).astype(np.int32)


def _band_tables():
    qi = np.arange(BAND_BLOCK)[:, None]
    ki = np.arange(2 * BAND_BLOCK)[None, :]
    idx, add = [], []
    for w, d in DILATED_PATTERNS:
        taps = w // d
        tap = qi + BAND_BLOCK - ki
        valid = (tap >= 0) & (tap <= taps)
        bucket = _rel_bucket_np(np.maximum(tap, 0) * d)
        for first in (False, True):
            v = valid & (ki >= BAND_BLOCK) if first else valid
            idx.append(bucket)
            add.append(np.where(v, 0.0, NEG_INF))
    return np.stack(idx).astype(np.int32), np.stack(add).astype(np.float32)


def _sample_tables(n_past, t_new):
    t = np.arange(t_new)[:, None]
    col = np.arange(n_past + LANES)[None, :]
    dist = np.where(col < n_past, n_past + t - col, t - (col - n_past))
    ok = (dist >= 0) & ((col < n_past) | (col - n_past < t_new))
    cnt = np.zeros(dist.shape, np.int64)
    for w, d in DILATED_PATTERNS:
        cnt += (ok & (dist % d == 0) & (dist <= w)).astype(np.int64)
    idx = _rel_bucket_np(np.maximum(dist, 0))
    add = np.where(cnt > 0, np.log(np.maximum(cnt, 1)), NEG_INF)
    return idx[None].astype(np.int32), add[None].astype(np.float32)


def _levels(seg):
    out, h = [], seg // 2
    while h >= 1:
        out.append(h)
        h //= 2
    return out


def _hgrn_tables(seg):
    r = np.arange(CHUNK)
    same_seg = (r[:, None] // seg) == (r[None, :] // seg)
    tri = (same_seg & (r[None, :] <= r[:, None])).astype(np.float32)
    eye = np.eye(CHUNK, dtype=np.float32)
    blocks = []
    lv = np.full((CHUNK, CHUNK), -1, np.int32)
    levels = _levels(seg)
    for i, h in enumerate(levels):
        ref = r - (r % (2 * h)) + h - 1
        g = np.zeros((CHUNK, CHUNK), np.float32)
        g[r, ref] = 1.0
        second = (r % (2 * h)) >= h
        sign = np.where(second, 1.0, -1.0)[:, None].astype(np.float32)
        blocks.append((sign * (eye - g)) @ tri)
        pair = ((r[:, None] // (2 * h)) == (r[None, :] // (2 * h))) & second[:, None] & (~second)[None, :]
        lv[pair] = i
    lv[r, r] = len(levels)
    seg_end = r - (r % seg) + seg - 1
    gl = np.zeros((CHUNK, CHUNK), np.float32)
    gl[r, seg_end] = 1.0
    blocks.append(tri)
    blocks.append((gl - eye) @ tri)
    mat = np.concatenate(blocks, axis=0)
    assert np.all(np.isin(mat, (-1.0, 0.0, 1.0)))
    return mat, lv


def _bias_kernel(rb_ref, idx_ref, add_ref, out_ref):
    h = pl.program_id(1)
    idx = idx_ref[0]
    acc = add_ref[0]
    for b in range(NUM_BUCKETS):
        acc = acc + jnp.where(idx == b, rb_ref[b, h], 0.0)
    out_ref[0, 0] = acc


def _expand_bias(rel_bias, idx, add):
    n, r, c = idx.shape
    n_heads = rel_bias.shape[1]
    return pl.pallas_call(
        _bias_kernel,
        grid=(n, n_heads),
        in_specs=[
            pl.BlockSpec(memory_space=pltpu.SMEM),
            pl.BlockSpec((1, r, c), lambda i, h: (i, 0, 0)),
            pl.BlockSpec((1, r, c), lambda i, h: (i, 0, 0)),
        ],
        out_specs=pl.BlockSpec((1, 1, r, c), lambda i, h: (i, h, 0, 0)),
        out_shape=jax.ShapeDtypeStruct((n, n_heads, r, c), F32),
        name="bias_expand",
    )(rel_bias.astype(F32), jnp.asarray(idx), jnp.asarray(add))


def _inproj_kernel(x_ref, g_ref, w_ref, wkvt_ref, lbraw_ref,
                   qr_ref, kr_ref, vr_ref, lf_ref, gate_ref,
                   qpm_ref, kpm_ref, vpm_ref, kout_ref, vout_ref, *, layer, d_rec, d_attn, cache_t):
    h = _rms(x_ref[...], g_ref[...]).astype(BF16)

    raw = lbraw_ref[...]
    e = jnp.exp(raw - jnp.max(raw, axis=0, keepdims=True))
    soft = e / jnp.sum(e, axis=0, keepdims=True)
    cum = soft[0:1]
    for i in range(1, layer + 1):
        cum = cum + soft[i:i + 1]
    lb = cum - soft[0:1]

    def proj(c0, width):
        return _dot(h, w_ref[:, c0:c0 + width])

    p = proj(0, d_rec)
    qr_ref[...] = p * jax.nn.sigmoid(p)
    z = proj(d_rec, d_rec)
    f = lb + (1.0 - lb) * jax.nn.sigmoid(z)
    lf_ref[...] = jnp.log(jnp.maximum(f, MIN_FORGET))
    kr_ref[...] = (1.0 - lb) * jax.nn.sigmoid(-z)
    vr_ref[...] = proj(2 * d_rec, d_rec)
    p = proj(3 * d_rec, d_rec)
    gate_ref[...] = p * jax.nn.sigmoid(p)

    c0 = 4 * d_rec
    qa = proj(c0, d_attn) * (ATTN_HEAD_DIM ** -0.5)
    ka = proj(c0 + d_attn, d_attn)
    va = proj(c0 + 2 * d_attn, d_attn)
    for pr in range(d_attn // LANES):
        sl = slice(pr * LANES, (pr + 1) * LANES)
        qpm_ref[pr] = qa[:, sl]
        kpm_ref[pr] = ka[:, sl]
        vpm_ref[pr] = va[:, sl]
    if cache_t is None:
        kout_ref[...] = ka
        vout_ref[...] = va
    else:
        blocks_per_seq, first_kept = cache_t

        @pl.when(pl.program_id(0) % blocks_per_seq >= first_kept)
        def _():
            kout_ref[0] = _dot_nt(wkvt_ref[0:d_attn, :], h)
            vout_ref[0] = _dot_nt(wkvt_ref[d_attn:2 * d_attn, :], h)


def _inproj(x2d, g, w_bf, wkvt_bf, lb_raw, layer, tm, seq_keep=None):
    t, dm = x2d.shape
    d_rec = lb_raw.shape[1]
    d_attn = (w_bf.shape[1] - 4 * d_rec) // 3
    n_pair = d_attn // LANES
    nat = lambda wd: pl.BlockSpec((tm, wd), lambda i: (i, 0))
    pm = pl.BlockSpec((n_pair, tm, LANES), lambda i: (0, i, 0))
    full = lambda a: pl.BlockSpec(a.shape, lambda i: (0,) * a.ndim)
    f32 = lambda *s: jax.ShapeDtypeStruct(s, F32)
    if seq_keep is None:
        cache_t, kv_spec, kv_shape = None, nat(d_attn), f32(t, d_attn)
    else:
        seq, keep = seq_keep
        assert seq % tm == 0 and keep % tm == 0
        bps, first = seq // tm, (seq - keep) // tm
        cache_t = (bps, first)
        kv_spec = pl.BlockSpec((1, d_attn, tm), lambda i: (i // bps, 0, jnp.maximum(i % bps - first, 0)))
        kv_shape = f32(t // seq, d_attn, keep)
    return pl.pallas_call(
        functools.partial(_inproj_kernel, layer=layer, d_rec=d_rec, d_attn=d_attn, cache_t=cache_t),
        grid=(t // tm,),
        in_specs=[nat(dm), full(g), full(w_bf), full(wkvt_bf), full(lb_raw)],
        out_specs=[nat(d_rec)] * 5 + [pm] * 3 + [kv_spec] * 2,
        out_shape=[f32(t, d_rec)] * 5 + [f32(n_pair, t, LANES)] * 3 + [kv_shape] * 2,
        compiler_params=pltpu.CompilerParams(dimension_semantics=("arbitrary",), vmem_limit_bytes=VMEM_LIMIT),
        name="inproj",
    )(x2d, g, w_bf, wkvt_bf, lb_raw)


def _hgrn_kernel(q_ref, k_ref, v_ref, lf_ref, gate_ref, s0_ref, gn_ref, mat_ref, lv_ref,
                 o_ref, sout_ref, st_ref, *, seg, n_chunks):
    nseg = CHUNK // seg
    seg_shift = seg.bit_length() - 1
    levels = _levels(seg)
    nl = len(levels)
    ti = pl.program_id(2)

    @pl.when(ti == 0)
    def _():
        st_ref[...] = s0_ref[0, :, 0]

    rowi = lax.broadcasted_iota(jnp.int32, (CHUNK, CHUNK), 0)
    coli = lax.broadcasted_iota(jnp.int32, (CHUNK, CHUNK), 1)
    mat = mat_ref[...]
    lv = lv_ref[...]
    gn = gn_ref[...]

    for c in range(n_chunks):
        rows = slice(c * CHUNK, (c + 1) * CHUNK)
        q = q_ref[rows, :]
        k = k_ref[rows, :]
        v = v_ref[rows, :]
        e = _sum2(_dot(mat, _split2(lf_ref[rows, :])))
        vb = v.astype(BF16)

        a = jnp.zeros((CHUNK, CHUNK), F32)
        for i, h in enumerate(levels):
            second = (rowi & (2 * h - 1)) >= h
            x = (jnp.where(second, q, k) * jnp.exp(e[i * CHUNK:(i + 1) * CHUNK])).astype(BF16)
            a = jnp.where(lv == i, _dot_nt(x, x), a)
        a = jnp.where(lv == nl, _dot_nt(q.astype(BF16), k.astype(BF16)), a)
        o = _dot(a.astype(BF16), vb)

        eb = jnp.exp(e[nl * CHUNK:(nl + 1) * CHUNK])
        qe = (q * eb).astype(BF16)
        kd_t = (k * jnp.exp(e[(nl + 1) * CHUNK:(nl + 2) * CHUNK])).T
        dec_t = eb.T
        for j in range(nseg):
            s_old = st_ref[j]
            oj = _dot(qe, s_old.astype(BF16))
            kd_j = kd_t
            if nseg > 1:
                oj = jnp.where((rowi >> seg_shift) == j, oj, 0.0)
                kd_j = jnp.where((coli >> seg_shift) == j, kd_t, 0.0)
            o = o + oj
            col = j * seg + seg - 1
            st_ref[j] = s_old * dec_t[:, col:col + 1] + _dot(kd_j.astype(BF16), vb)

        o_ref[rows, :] = (_rms(o, gn) * gate_ref[rows, :]).astype(o_ref.dtype)

    @pl.when(ti == pl.num_programs(2) - 1)
    def _():
        sout_ref[:, 0] = st_ref[...]


def _hgrn(qr, kr, vr, lf, gate, s0_all, layer, g_norm, seg, rows_per_step, steps_per_seq):
    t, d_rec = qr.shape
    n_heads = d_rec // LANES
    nseg = CHUNK // seg
    n_seq = s0_all.shape[1]
    mat, lv = _hgrn_tables(seg)
    tok = pl.BlockSpec((rows_per_step, LANES), lambda bi, h, ti: (bi * steps_per_seq + ti, h))
    st_in = pl.BlockSpec((1, nseg, 1, REC_EXPAND, REC_DV), lambda bi, h, ti: (layer, bi, h, 0, 0))
    st_out = pl.BlockSpec((nseg, 1, REC_EXPAND, REC_DV), lambda bi, h, ti: (bi, h, 0, 0))
    full = lambda a: pl.BlockSpec(a.shape, lambda bi, h, ti: (0,) * a.ndim)
    mat_j, lv_j = jnp.asarray(mat, BF16), jnp.asarray(lv)
    return pl.pallas_call(
        functools.partial(_hgrn_kernel, seg=seg, n_chunks=rows_per_step // CHUNK),
        grid=(n_seq // nseg, n_heads, steps_per_seq),
        in_specs=[tok] * 5 + [st_in, pl.BlockSpec((1, LANES), lambda bi, h, ti: (0, h)),
                              full(mat_j), full(lv_j)],
        out_specs=[tok, st_out],
        out_shape=[jax.ShapeDtypeStruct((t, d_rec), BF16),
                   jax.ShapeDtypeStruct((n_seq, n_heads, REC_EXPAND, REC_DV), F32)],
        scratch_shapes=[pltpu.VMEM((nseg, REC_EXPAND, REC_DV), F32)],
        compiler_params=pltpu.CompilerParams(
            dimension_semantics=("parallel", "parallel", "arbitrary"), vmem_limit_bytes=VMEM_LIMIT),
        name="hgrn",
    )(qr, kr, vr, lf, gate, s0_all, g_norm, mat_j, lv_j)


def _head_softmax_pv(q2, k2, v2, bias_pair, lo_half):
    outs, lses = [], []
    for hh in range(2):
        keep = lo_half if hh == 0 else jnp.logical_not(lo_half)
        qm = jnp.where(keep, q2, 0.0).astype(BF16)
        s = _dot_nt(qm, k2) + bias_pair(hh)
        m = jnp.max(s, axis=-1, keepdims=True)
        p = jnp.exp(s - m)
        l = jnp.sum(p, axis=-1, keepdims=True)
        outs.append(_dot(p.astype(BF16), v2) * (1.0 / l))
        lses.append(m + jnp.log(l))
    o2 = jnp.where(lo_half, outs[0], outs[1])
    l2 = jnp.where(lo_half, lses[0], lses[1])
    return o2, l2


ATTN_UNROLL = 4


def _prompt_attn_kernel(q_ref, k_ref, v_ref, bias_ref, g_ref, bd_ref, out_ref, oacc_ref, lacc_ref, *, seq):
    lane = lax.broadcasted_iota(jnp.int32, (1, LANES), 1)
    lo_half = lane < ATTN_HEAD_DIM
    bb = BAND_BLOCK

    for pi, (_, d) in enumerate(DILATED_PATTERNS):
        nb = seq // (d * bb)

        def block_group(gi, carry, pi=pi, d=d, nb=nb):
            for u in range(ATTN_UNROLL):
                n = gi * ATTN_UNROLL + u
                r = n // nb
                i = n % nb
                base = r + i * (d * bb)
                prev = jnp.maximum(base - d * bb, r)
                first = jnp.where(i == 0, 1, 0)
                qsl = pl.ds(base, bb, stride=d)
                psl = pl.ds(prev, bb, stride=d)
                k2 = jnp.concatenate([k_ref[0, psl, :], k_ref[0, qsl, :]], axis=0).astype(BF16)
                v2 = jnp.concatenate([v_ref[0, psl, :], v_ref[0, qsl, :]], axis=0).astype(BF16)
                o2, l2 = _head_softmax_pv(q_ref[0, qsl, :], k2, v2,
                                          lambda hh: bias_ref[pi, first, hh], lo_half)
                oacc_ref[pi, qsl, :] = o2
                lacc_ref[pi, qsl, :] = l2
            return carry

        lax.fori_loop(0, (d * nb) // ATTN_UNROLL, block_group, 0)

    g = g_ref[...]
    bd = bd_ref[...]

    def merge(i, carry):
        rows = pl.ds(pl.multiple_of(i * bb, bb), bb)
        ls = [lacc_ref[pi, rows, :] for pi in range(len(DILATED_PATTERNS))]
        mx = functools.reduce(jnp.maximum, ls)
        ws = [jnp.exp(l - mx) for l in ls]
        den = functools.reduce(lambda a, b: a + b, ws)
        num = functools.reduce(lambda a, b: a + b,
                               [w * oacc_ref[pi, rows, :] for pi, w in enumerate(ws)])
        o = num / den
        ms = _dot((o * o).astype(BF16), bd)
        out_ref[0, rows, :] = (o * lax.rsqrt(ms + RMS_EPS) * g).astype(out_ref.dtype)
        return carry

    lax.fori_loop(0, seq // bb, merge, 0)


def _head_mean_matrix(width):
    r = np.arange(width)
    return jnp.asarray(((r[:, None] // ATTN_HEAD_DIM) == (r[None, :] // ATTN_HEAD_DIM)) / ATTN_HEAD_DIM, BF16)


def _prompt_attention(qpm, kpm, vpm, band_bias, g_attn, n_seq, seq):
    n_pair = qpm.shape[0]
    n_pat = len(DILATED_PATTERNS)
    for _, d in DILATED_PATTERNS:
        assert seq % (d * BAND_BLOCK) == 0 and (seq // BAND_BLOCK) % ATTN_UNROLL == 0
    tok = pl.BlockSpec((1, seq, LANES), lambda p, b: (p, b, 0))
    bd = _head_mean_matrix(LANES)
    return pl.pallas_call(
        functools.partial(_prompt_attn_kernel, seq=seq),
        grid=(n_pair, n_seq),
        in_specs=[tok, tok, tok,
                  pl.BlockSpec((n_pat, 2, 2, BAND_BLOCK, 2 * BAND_BLOCK), lambda p, b: (0, 0, p, 0, 0)),
                  pl.BlockSpec((1, LANES), lambda p, b: (0, p)),
                  pl.BlockSpec(bd.shape, lambda p, b: (0, 0))],
        out_specs=tok,
        out_shape=jax.ShapeDtypeStruct(qpm.shape, BF16),
        scratch_shapes=[pltpu.VMEM((n_pat, seq, LANES), F32), pltpu.VMEM((n_pat, seq, LANES), F32)],
        compiler_params=pltpu.CompilerParams(
            dimension_semantics=("parallel", "parallel"), vmem_limit_bytes=VMEM_LIMIT),
        name="prompt_attn",
    )(qpm, kpm, vpm, band_bias, g_attn, bd)


def _sample_attn_kernel(q_ref, kn_ref, vn_ref, ckt_ref, cvt_ref, bias_ref, g_ref, bd_ref, o_ref, *, n_heads):
    t_new, width = q_ref.shape[1], q_ref.shape[2]
    n_past = ckt_ref.shape[3]
    rows = n_heads * t_new
    own = (lax.broadcasted_iota(jnp.int32, (rows, width), 0) // t_new
           == lax.broadcasted_iota(jnp.int32, (rows, width), 1) // ATTN_HEAD_DIM)
    qbd = jnp.where(own, jnp.concatenate([q_ref[0]] * n_heads, axis=0), 0.0).astype(BF16)

    pad = jnp.zeros((LANES - t_new, width), F32)
    kn = jnp.concatenate([kn_ref[0], pad], axis=0).astype(BF16)
    vn = jnp.concatenate([vn_ref[0], pad], axis=0).astype(BF16)
    s_c = _dot(qbd, ckt_ref[0, 0].astype(BF16)) + bias_ref[:, :n_past]
    s_n = _dot_nt(qbd, kn) + bias_ref[:, n_past:]
    m = jnp.maximum(jnp.max(s_c, axis=-1, keepdims=True), jnp.max(s_n, axis=-1, keepdims=True))
    p_c = jnp.exp(s_c - m)
    p_n = jnp.exp(s_n - m)
    l = jnp.sum(p_c, axis=-1, keepdims=True) + jnp.sum(p_n, axis=-1, keepdims=True)
    acc = _dot_nt(p_c.astype(BF16), cvt_ref[0, 0].astype(BF16)) + _dot(p_n.astype(BF16), vn)
    o = jnp.where(own, acc * (1.0 / l), 0.0)
    ms = _dot((o * o).astype(BF16), bd_ref[...])
    o = o * lax.rsqrt(ms + RMS_EPS) * g_ref[...]
    out = o[0:t_new]
    for h in range(1, n_heads):
        out = out + o[h * t_new:(h + 1) * t_new]
    o_ref[0] = out


def _sample_attention(q3, kn3, vn3, cache_kt, cache_vt, layer, bias, g_attn):
    n_seq, t_new, width = q3.shape
    n_past = cache_kt.shape[3]
    n_heads = width // ATTN_HEAD_DIM
    bd = _head_mean_matrix(width)
    tok = pl.BlockSpec((1, t_new, width), lambda b: (b, 0, 0))
    cache = pl.BlockSpec((1, 1, width, n_past), lambda b: (layer, b, 0, 0))
    full = lambda a: pl.BlockSpec(a.shape, lambda b: (0,) * a.ndim)
    return pl.pallas_call(
        functools.partial(_sample_attn_kernel, n_heads=n_heads),
        grid=(n_seq,),
        in_specs=[tok, tok, tok, cache, cache, full(bias), full(g_attn), full(bd)],
        out_specs=tok,
        out_shape=jax.ShapeDtypeStruct(q3.shape, F32),
        compiler_params=pltpu.CompilerParams(dimension_semantics=("parallel",), vmem_limit_bytes=VMEM_LIMIT),
        name="sample_attn",
    )(q3, kn3, vn3, cache_kt, cache_vt, bias, g_attn, bd)


def _mlp_kernel(x_ref, rec_ref, attn_ref, wout_ref, wup_ref, wdn_ref, gpost_ref, gpre_ref, gmpost_ref,
                y_ref, *, ff_chunk):
    d_rec = rec_ref.shape[1]
    mixed = _dot(rec_ref[...].astype(BF16), wout_ref[0:d_rec, :])
    for p in range(attn_ref.shape[0]):
        r0 = d_rec + p * LANES
        mixed = mixed + _dot(attn_ref[p].astype(BF16), wout_ref[r0:r0 + LANES, :])
    x1 = x_ref[...] + _rms(mixed, gpost_ref[...])
    h = _rms(x1, gpre_ref[...]).astype(BF16)
    acc = jnp.zeros(x1.shape, F32)
    for c0 in range(0, wup_ref.shape[1], ff_chunk):
        u = jnp.square(jnp.maximum(_dot(h, wup_ref[:, c0:c0 + ff_chunk]), 0.0))
        acc = acc + _dot(u.astype(BF16), wdn_ref[c0:c0 + ff_chunk, :])
    y_ref[...] = x1 + _rms(acc, gmpost_ref[...])


def _mlp(x2d, rec, attn_pm, wout, wup, wdn, g_post, g_pre, g_mpost, tm):
    t, dm = x2d.shape
    n_pair = attn_pm.shape[0]
    row = lambda wd: pl.BlockSpec((tm, wd), lambda i: (i, 0))
    once = lambda a: pl.BlockSpec(a.shape, lambda i: (0,) * a.ndim, pipeline_mode=pl.Buffered(1))
    return pl.pallas_call(
        functools.partial(_mlp_kernel, ff_chunk=1024),
        grid=(t // tm,),
        in_specs=[row(dm), row(rec.shape[1]), pl.BlockSpec((n_pair, tm, LANES), lambda i: (0, i, 0)),
                  once(wout), once(wup), once(wdn), once(g_post), once(g_pre), once(g_mpost)],
        out_specs=row(dm),
        out_shape=jax.ShapeDtypeStruct((t, dm), F32),
        compiler_params=pltpu.CompilerParams(dimension_semantics=("parallel",), vmem_limit_bytes=VMEM_LIMIT),
        name="mlp",
    )(x2d, rec, attn_pm, wout, wup, wdn, g_post, g_pre, g_mpost)


def kernel(x_prompt, x_sample, state_hgrn, cache_k, cache_v, rel_bias, lb_raw, w_in, w_out, w_up, w_down,
           g_mix_pre, g_mix_post, g_mlp_pre, g_mlp_post, g_rec_out, g_attn_out):
    n_p, seq, dm = x_prompt.shape
    n_s, t_new, _ = x_sample.shape
    depth = w_in.shape[0]
    n_past = cache_k.shape[2]
    n_heads, dh = cache_k.shape[3], cache_k.shape[4]
    d_attn = n_heads * dh
    d_rec = lb_raw.shape[1]
    n_rec_heads = d_rec // REC_EXPAND
    keep = min(MAX_WINDOW, seq)
    tm = 512
    tm_s = min(tm, n_s * t_new)
    assert dh == ATTN_HEAD_DIM and CHUNK % t_new == 0 and seq % tm == 0
    assert (n_s * t_new) % CHUNK == 0 and (n_s * t_new) % tm_s == 0

    band_idx, band_add = _band_tables()
    band_bias = _expand_bias(rel_bias, band_idx, band_add)
    band_bias = band_bias.reshape(len(DILATED_PATTERNS), 2, n_heads, BAND_BLOCK, 2 * BAND_BLOCK)
    s_idx, s_add = _sample_tables(n_past, t_new)
    sample_bias = _expand_bias(rel_bias, s_idx, s_add).reshape(n_heads * t_new, n_past + LANES)

    w_in_b, w_out_b = w_in.astype(BF16), w_out.astype(BF16)
    w_up_b, w_dn_b = w_up.astype(BF16), w_down.astype(BF16)
    c_k = 4 * d_rec + d_attn
    wkvt_b = jnp.swapaxes(w_in_b[:, :, c_k:c_k + 2 * d_attn], 1, 2)
    row = lambda a, l: a[l][None, :]

    cache_kt = cache_k.transpose(0, 1, 3, 4, 2).reshape(depth, n_s, d_attn, n_past).astype(F32)
    cache_vt = cache_v.transpose(0, 1, 3, 4, 2).reshape(depth, n_s, d_attn, n_past).astype(F32)

    yp = x_prompt.reshape(n_p * seq, dm)
    ys = x_sample.reshape(n_s * t_new, dm)
    zeros_state = jnp.zeros((1, n_p, n_rec_heads, REC_EXPAND, REC_DV), F32)
    state_in = state_hgrn.astype(F32)
    rec_p, k_p, v_p, rec_s, k_s, v_s = [], [], [], [], [], []
    for l in range(depth):
        g_pre, g_rec, g_att = row(g_mix_pre, l), row(g_rec_out, l), row(g_attn_out, l)
        post = (row(g_mix_post, l), row(g_mlp_pre, l), row(g_mlp_post, l))

        qr, kr, vr, lf, gate, qpm, kpm, vpm, kt, vt = _inproj(
            yp, g_pre, w_in_b[l], wkvt_b[l], lb_raw, l, tm, seq_keep=(seq, keep))
        rec, s_new = _hgrn(qr, kr, vr, lf, gate, zeros_state, 0, g_rec, CHUNK, tm, seq // tm)
        attn = _prompt_attention(qpm, kpm, vpm, band_bias, g_att, n_p, seq)
        yp = _mlp(yp, rec, attn, w_out_b[l], w_up_b[l], w_dn_b[l], *post, tm)
        rec_p.append(s_new)
        k_p.append(kt)
        v_p.append(vt)

        qr, kr, vr, lf, gate, qpm, kpm, vpm, knat, vnat = _inproj(
            ys, g_pre, w_in_b[l], wkvt_b[l], lb_raw, l, tm_s)
        rec, s_new = _hgrn(qr, kr, vr, lf, gate, state_in, l, g_rec, t_new, CHUNK, 1)
        q3 = qpm.transpose(1, 0, 2).reshape(n_s, t_new, d_attn)
        attn = _sample_attention(q3, knat.reshape(n_s, t_new, d_attn), vnat.reshape(n_s, t_new, d_attn),
                                 cache_kt, cache_vt, l, sample_bias, g_att)
        attn_pm = attn.reshape(n_s * t_new, d_attn // LANES, LANES).transpose(1, 0, 2)
        ys = _mlp(ys, rec, attn_pm, w_out_b[l], w_up_b[l], w_dn_b[l], *post, tm_s)
        rec_s.append(s_new)
        k_s.append(knat.reshape(n_s, t_new, n_heads, dh))
        v_s.append(vnat.reshape(n_s, t_new, n_heads, dh))

    to_cache = lambda xs: jnp.stack(xs).reshape(depth, n_p, n_heads, dh, keep).transpose(0, 1, 4, 2, 3)
    return (yp.reshape(n_p, seq, dm), ys.reshape(n_s, t_new, dm), jnp.stack(rec_p), to_cache(k_p),
            to_cache(v_p), jnp.stack(rec_s), jnp.stack(k_s), jnp.stack(v_s))
```

```python
import functools
import math

import numpy as np
import jax
import jax.numpy as jnp
from jax import lax
from jax.experimental import pallas as pl
from jax.experimental.pallas import tpu as pltpu

F32 = jnp.float32
BF16 = jnp.bfloat16

ATTN_HEAD_DIM = 64
REC_EXPAND = 128
REC_DV = 128
DILATED_PATTERNS = ((128, 1), (512, 4), (2048, 16))
MAX_WINDOW = max(w for w, _ in DILATED_PATTERNS)
BAND_BLOCK = 128
NUM_BUCKETS = 32
MAX_DISTANCE = MAX_WINDOW
RMS_EPS = 1e-6
NEG_INF = -1e30
MIN_FORGET = 1e-30

LANES = 128
CHUNK = 128
HGRN_ROWS = 512
VMEM_LIMIT = 56 * 1024 * 1024


def _dot(a, b):
    return jnp.dot(a, b, preferred_element_type=F32)


def _dot_nt(a, b):
    return lax.dot_general(a, b, (((1,), (1,)), ((), ())), preferred_element_type=F32)


def _split2(x):
    hi = x.astype(BF16)
    lo = (x - hi.astype(F32)).astype(BF16)
    return jnp.concatenate([hi, lo], axis=1)


def _sum2(y):
    n = y.shape[1] // 2
    return y[:, :n] + y[:, n:]


def _rms(x, g):
    ms = jnp.mean(x * x, axis=-1, keepdims=True)
    return x * lax.rsqrt(ms + RMS_EPS) * g


def _rel_bucket_np(dist):
    exact = NUM_BUCKETS // 2
    d = np.maximum(dist, exact).astype(np.float64)
    large = exact + (np.log(d / exact) / math.log(MAX_DISTANCE / exact) * (NUM_BUCKETS - exact)).astype(np.int64)
    large = np.minimum(large, NUM_BUCKETS - 1)
    return np.where(dist < exact, dist, large).astype(np.int32)


def _band_tables():
    qi = np.arange(BAND_BLOCK)[:, None]
    ki = np.arange(2 * BAND_BLOCK)[None, :]
    idx, add = [], []
    for w, d in DILATED_PATTERNS:
        taps = w // d
        tap = qi + BAND_BLOCK - ki
        valid = (tap >= 0) & (tap <= taps)
        bucket = _rel_bucket_np(np.maximum(tap, 0) * d)
        for first in (False, True):
            v = valid & (ki >= BAND_BLOCK) if first else valid
            idx.append(bucket)
            add.append(np.where(v, 0.0, NEG_INF))
    return np.stack(idx).astype(np.int32), np.stack(add).astype(np.float32)


def _sample_tables(n_past, t_new):
    t = np.arange(t_new)[:, None]
    col = np.arange(n_past + LANES)[None, :]
    dist = np.where(col < n_past, n_past + t - col, t - (col - n_past))
    ok = (dist >= 0) & ((col < n_past) | (col - n_past < t_new))
    cnt = np.zeros(dist.shape, np.int64)
    for w, d in DILATED_PATTERNS:
        cnt += (ok & (dist % d == 0) & (dist <= w)).astype(np.int64)
    idx = _rel_bucket_np(np.maximum(dist, 0))
    add = np.where(cnt > 0, np.log(np.maximum(cnt, 1)), NEG_INF)
    return idx[None].astype(np.int32), add[None].astype(np.float32)


def _levels(seg):
    out, h = [], seg // 2
    while h >= 1:
        out.append(h)
        h //= 2
    return out


def _hgrn_tables(seg):
    r = np.arange(CHUNK)
    same_seg = (r[:, None] // seg) == (r[None, :] // seg)
    tri = (same_seg & (r[None, :] <= r[:, None])).astype(np.float32)
    eye = np.eye(CHUNK, dtype=np.float32)
    blocks = []
    lv = np.full((CHUNK, CHUNK), -1, np.int32)
    levels = _levels(seg)
    for i, h in enumerate(levels):
        ref = r - (r % (2 * h)) + h - 1
        g = np.zeros((CHUNK, CHUNK), np.float32)
        g[r, ref] = 1.0
        second = (r % (2 * h)) >= h
        sign = np.where(second, 1.0, -1.0)[:, None].astype(np.float32)
        blocks.append((sign * (eye - g)) @ tri)
        pair = ((r[:, None] // (2 * h)) == (r[None, :] // (2 * h))) & second[:, None] & (~second)[None, :]
        lv[pair] = i
    lv[r, r] = len(levels)
    seg_end = r - (r % seg) + seg - 1
    gl = np.zeros((CHUNK, CHUNK), np.float32)
    gl[r, seg_end] = 1.0
    blocks.append(tri)
    blocks.append((gl - eye) @ tri)
    mat = np.concatenate(blocks, axis=0)
    assert np.all(np.isin(mat, (-1.0, 0.0, 1.0)))
    return mat, lv


def _bias_kernel(rb_ref, idx_ref, add_ref, out_ref):
    h = pl.program_id(1)
    idx = idx_ref[0]
    acc = add_ref[0]
    for b in range(NUM_BUCKETS):
        acc = acc + jnp.where(idx == b, rb_ref[b, h], 0.0)
    out_ref[0, 0] = acc


def _expand_bias(rel_bias, idx, add):
    n, r, c = idx.shape
    n_heads = rel_bias.shape[1]
    return pl.pallas_call(
        _bias_kernel,
        grid=(n, n_heads),
        in_specs=[
            pl.BlockSpec(memory_space=pltpu.SMEM),
            pl.BlockSpec((1, r, c), lambda i, h: (i, 0, 0)),
            pl.BlockSpec((1, r, c), lambda i, h: (i, 0, 0)),
        ],
        out_specs=pl.BlockSpec((1, 1, r, c), lambda i, h: (i, h, 0, 0)),
        out_shape=jax.ShapeDtypeStruct((n, n_heads, r, c), F32),
        name="bias_expand",
    )(rel_bias.astype(F32), jnp.asarray(idx), jnp.asarray(add))


def _inproj_kernel(x_ref, g_ref, w_ref, wkvt_ref, lbraw_ref,
                   qr_ref, kr_ref, vr_ref, lf_ref, gate_ref,
                   qpm_ref, kpm_ref, vpm_ref, kout_ref, vout_ref, *, layer, d_rec, d_attn, cache_t):
    h = _rms(x_ref[...], g_ref[...]).astype(BF16)

    raw = lbraw_ref[...]
    e = jnp.exp(raw - jnp.max(raw, axis=0, keepdims=True))
    soft = e / jnp.sum(e, axis=0, keepdims=True)
    cum = soft[0:1]
    for i in range(1, layer + 1):
        cum = cum + soft[i:i + 1]
    lb = cum - soft[0:1]

    def proj(c0, width):
        return _dot(h, w_ref[:, c0:c0 + width])

    p = proj(0, d_rec)
    qr_ref[...] = p * jax.nn.sigmoid(p)
    z = proj(d_rec, d_rec)
    f = lb + (1.0 - lb) * jax.nn.sigmoid(z)
    lf_ref[...] = jnp.log(jnp.maximum(f, MIN_FORGET))
    kr_ref[...] = (1.0 - lb) * jax.nn.sigmoid(-z)
    vr_ref[...] = proj(2 * d_rec, d_rec).astype(vr_ref.dtype)
    p = proj(3 * d_rec, d_rec)
    gate_ref[...] = (p * jax.nn.sigmoid(p)).astype(gate_ref.dtype)

    c0 = 4 * d_rec
    qa = proj(c0, d_attn) * (ATTN_HEAD_DIM ** -0.5)
    ka = proj(c0 + d_attn, d_attn)
    va = proj(c0 + 2 * d_attn, d_attn)
    for pr in range(d_attn // LANES):
        sl = slice(pr * LANES, (pr + 1) * LANES)
        qpm_ref[pr] = qa[:, sl]
        kpm_ref[pr] = ka[:, sl]
        vpm_ref[pr] = va[:, sl]
    if cache_t is None:
        kout_ref[...] = ka
        vout_ref[...] = va
    else:
        blocks_per_seq, first_kept = cache_t

        @pl.when(pl.program_id(0) % blocks_per_seq >= first_kept)
        def _():
            kout_ref[0] = _dot_nt(wkvt_ref[0:d_attn, :], h)
            vout_ref[0] = _dot_nt(wkvt_ref[d_attn:2 * d_attn, :], h)


def _inproj(x2d, g, w_bf, wkvt_bf, lb_raw, layer, tm, seq_keep=None):
    t, dm = x2d.shape
    d_rec = lb_raw.shape[1]
    d_attn = (w_bf.shape[1] - 4 * d_rec) // 3
    n_pair = d_attn // LANES
    nat = lambda wd: pl.BlockSpec((tm, wd), lambda i: (i, 0))
    pm = pl.BlockSpec((n_pair, tm, LANES), lambda i: (0, i, 0))
    full = lambda a: pl.BlockSpec(a.shape, lambda i: (0,) * a.ndim)
    f32 = lambda *s: jax.ShapeDtypeStruct(s, F32)
    if seq_keep is None:
        cache_t, kv_spec, kv_shape = None, nat(d_attn), f32(t, d_attn)
    else:
        seq, keep = seq_keep
        assert seq % tm == 0 and keep % tm == 0
        bps, first = seq // tm, (seq - keep) // tm
        cache_t = (bps, first)
        kv_spec = pl.BlockSpec((1, d_attn, tm), lambda i: (i // bps, 0, jnp.maximum(i % bps - first, 0)))
        kv_shape = f32(t // seq, d_attn, keep)
    return pl.pallas_call(
        functools.partial(_inproj_kernel, layer=layer, d_rec=d_rec, d_attn=d_attn, cache_t=cache_t),
        grid=(t // tm,),
        in_specs=[nat(dm), full(g), full(w_bf), full(wkvt_bf), full(lb_raw)],
        out_specs=[nat(d_rec)] * 5 + [pm] * 3 + [kv_spec] * 2,
        out_shape=[f32(t, d_rec), f32(t, d_rec), jax.ShapeDtypeStruct((t, d_rec), BF16), f32(t, d_rec),
                   jax.ShapeDtypeStruct((t, d_rec), BF16)] + [f32(n_pair, t, LANES)] * 3 + [kv_shape] * 2,
        compiler_params=pltpu.CompilerParams(dimension_semantics=("arbitrary",), vmem_limit_bytes=VMEM_LIMIT),
        name="inproj",
    )(x2d, g, w_bf, wkvt_bf, lb_raw)


def _hgrn_kernel(q_ref, k_ref, v_ref, lf_ref, gate_ref, s0_ref, gn_ref, mat_ref, lv_ref,
                 o_ref, sout_ref, st_ref, *, seg, n_chunks):
    nseg = CHUNK // seg
    seg_shift = seg.bit_length() - 1
    levels = _levels(seg)
    nl = len(levels)
    ti = pl.program_id(2)

    @pl.when(ti == 0)
    def _():
        st_ref[...] = s0_ref[0, :, 0]

    rowi = lax.broadcasted_iota(jnp.int32, (CHUNK, CHUNK), 0)
    coli = lax.broadcasted_iota(jnp.int32, (CHUNK, CHUNK), 1)
    mat = mat_ref[...]
    lv = lv_ref[...]
    gn = gn_ref[...]

    chunk_rows = [slice(c * CHUNK, (c + 1) * CHUNK) for c in range(n_chunks)]
    lf_split = [_split2(lf_ref[rows, :]) for rows in chunk_rows]
    e_lv = _dot(mat[:nl * CHUNK], jnp.concatenate([s[:, :LANES] for s in lf_split], axis=1))
    e_bl = _dot(mat[nl * CHUNK:], jnp.concatenate(lf_split, axis=1))

    intra = []
    for c, rows in enumerate(chunk_rows):
        q = q_ref[rows, :]
        k = k_ref[rows, :]
        a = jnp.zeros((CHUNK, CHUNK), F32)
        for i, h in enumerate(levels):
            second = (rowi & (2 * h - 1)) >= h
            e = e_lv[i * CHUNK:(i + 1) * CHUNK, c * LANES:(c + 1) * LANES]
            x = (jnp.where(second, q, k) * jnp.exp(e)).astype(BF16)
            a = jnp.where(lv == i, _dot_nt(x, x), a)
        a = jnp.where(lv == nl, _dot_nt(q.astype(BF16), k.astype(BF16)), a)
        intra.append(a.astype(BF16))

    states = [st_ref[j] for j in range(nseg)]
    for c, rows in enumerate(chunk_rows):
        q = q_ref[rows, :]
        k = k_ref[rows, :]
        vb = v_ref[rows, :].astype(BF16)
        e = _sum2(e_bl[:, 2 * c * LANES:2 * (c + 1) * LANES])
        eb = jnp.exp(e[:CHUNK])
        qe = (q * eb).astype(BF16)
        kd_t = (k * jnp.exp(e[CHUNK:])).T
        dec_t = eb.T
        o = _dot(intra[c], vb)
        for j in range(nseg):
            s_old = states[j]
            oj = _dot(qe, s_old.astype(BF16))
            kd_j = kd_t
            if nseg > 1:
                oj = jnp.where((rowi >> seg_shift) == j, oj, 0.0)
                kd_j = jnp.where((coli >> seg_shift) == j, kd_t, 0.0)
            o = o + oj
            col = j * seg + seg - 1
            states[j] = s_old * dec_t[:, col:col + 1] + _dot(kd_j.astype(BF16), vb)
        o_ref[rows, :] = (_rms(o, gn) * gate_ref[rows, :]).astype(o_ref.dtype)

    for j in range(nseg):
        st_ref[j] = states[j]

    @pl.when(ti == pl.num_programs(2) - 1)
    def _():
        sout_ref[:, 0] = st_ref[...]


def _hgrn(qr, kr, vr, lf, gate, s0_all, layer, g_norm, seg, rows_per_step, steps_per_seq):
    t, d_rec = qr.shape
    n_heads = d_rec // LANES
    nseg = CHUNK // seg
    n_seq = s0_all.shape[1]
    mat, lv = _hgrn_tables(seg)
    tok = pl.BlockSpec((rows_per_step, LANES), lambda bi, h, ti: (bi * steps_per_seq + ti, h))
    st_in = pl.BlockSpec((1, nseg, 1, REC_EXPAND, REC_DV), lambda bi, h, ti: (layer, bi, h, 0, 0))
    st_out = pl.BlockSpec((nseg, 1, REC_EXPAND, REC_DV), lambda bi, h, ti: (bi, h, 0, 0))
    full = lambda a: pl.BlockSpec(a.shape, lambda bi, h, ti: (0,) * a.ndim)
    mat_j, lv_j = jnp.asarray(mat, BF16), jnp.asarray(lv)
    return pl.pallas_call(
        functools.partial(_hgrn_kernel, seg=seg, n_chunks=rows_per_step // CHUNK),
        grid=(n_seq // nseg, n_heads, steps_per_seq),
        in_specs=[tok] * 5 + [st_in, pl.BlockSpec((1, LANES), lambda bi, h, ti: (0, h)),
                              full(mat_j), full(lv_j)],
        out_specs=[tok, st_out],
        out_shape=[jax.ShapeDtypeStruct((t, d_rec), BF16),
                   jax.ShapeDtypeStruct((n_seq, n_heads, REC_EXPAND, REC_DV), F32)],
        scratch_shapes=[pltpu.VMEM((nseg, REC_EXPAND, REC_DV), F32)],
        compiler_params=pltpu.CompilerParams(
            dimension_semantics=("parallel", "parallel", "arbitrary"), vmem_limit_bytes=VMEM_LIMIT),
        name="hgrn",
    )(qr, kr, vr, lf, gate, s0_all, g_norm, mat_j, lv_j)


ATTN_UNROLL = 4
MERGE_UNROLL = 4


def _prompt_attn_kernel(q_ref, k_ref, v_ref, bias_ref, g_ref, bd_ref, out_ref,
                        oacc_ref, macc_ref, lacc_ref, p_ref, *, seq):
    lane = lax.broadcasted_iota(jnp.int32, (1, LANES), 1)
    lo_half = lane < ATTN_HEAD_DIM
    keeps = (lo_half, jnp.logical_not(lo_half))
    one = jnp.ones((), BF16)
    bb = BAND_BLOCK

    def runs(gi, d, nb):
        run = min(ATTN_UNROLL, nb)
        runs_per_res = nb // run
        for w in range(ATTN_UNROLL // run):
            n = gi * (ATTN_UNROLL // run) + w
            r = n // runs_per_res
            i0 = (n % runs_per_res) * run
            base = r + i0 * (d * bb)
            psl = pl.ds(jnp.maximum(base - d * bb, r), bb, stride=d)
            yield base, i0, psl, run, w * run

    def stage_scores(pi, d, nb, gi, slot):
        for base, i0, psl, run, u0 in runs(gi, d, nb):
            k_prev = k_ref[0, psl, :].astype(BF16)
            for j in range(run):
                qsl = pl.ds(base + j * (d * bb), bb, stride=d)
                k_cur = k_ref[0, qsl, :].astype(BF16)
                k2 = jnp.concatenate([k_prev, k_cur], axis=0)
                q2 = q_ref[0, qsl, :]
                variant = jnp.where(i0 == 0, 1, 0) if j == 0 else 0
                ms = []
                for hh in range(2):
                    qm = jnp.where(keeps[hh], q2, 0.0).astype(BF16)
                    s = _dot_nt(qm, k2) + bias_ref[pi, variant, hh]
                    m = jnp.max(s, axis=-1, keepdims=True)
                    p_ref[slot, 2 * (u0 + j) + hh] = jnp.exp(s - m).astype(BF16)
                    ms.append(m)
                macc_ref[pi, qsl, :] = jnp.where(lo_half, ms[0], ms[1])
                k_prev = k_cur

    def stage_values(pi, d, nb, gi, slot):
        for base, i0, psl, run, u0 in runs(gi, d, nb):
            v_prev = v_ref[0, psl, :].astype(BF16)
            for j in range(run):
                qsl = pl.ds(base + j * (d * bb), bb, stride=d)
                v_cur = v_ref[0, qsl, :].astype(BF16)
                v2 = jnp.concatenate([v_prev, v_cur], axis=0)
                pvs = [_dot(p_ref[slot, 2 * (u0 + j) + hh], jnp.where(keeps[hh], v2, one)) for hh in range(2)]
                oacc_ref[pi, qsl, :] = jnp.where(lo_half, pvs[0], pvs[1])
                lacc_ref[pi, qsl, :] = pltpu.roll(jnp.where(lo_half, pvs[1], pvs[0]), ATTN_HEAD_DIM, axis=1)
                v_prev = v_cur

    for pi, (_, d) in enumerate(DILATED_PATTERNS):
        nb = seq // (d * bb)
        n_groups = (d * nb) // ATTN_UNROLL
        stage_scores(pi, d, nb, 0, 0)

        def overlapped(k, carry, pi=pi, d=d, nb=nb):
            stage_scores(pi, d, nb, 2 * k + 1, 1)
            stage_values(pi, d, nb, 2 * k, 0)
            stage_scores(pi, d, nb, 2 * k + 2, 0)
            stage_values(pi, d, nb, 2 * k + 1, 1)
            return carry

        lax.fori_loop(0, n_groups // 2 - 1, overlapped, 0)
        stage_scores(pi, d, nb, n_groups - 1, 1)
        stage_values(pi, d, nb, n_groups - 2, 0)
        stage_values(pi, d, nb, n_groups - 1, 1)

    g = g_ref[...]
    bd = bd_ref[...]

    def merge(gi, carry):
        for u in range(MERGE_UNROLL):
            rows = pl.ds(pl.multiple_of((gi * MERGE_UNROLL + u) * bb, bb), bb)
            ms = [macc_ref[pi, rows, :] for pi in range(len(DILATED_PATTERNS))]
            mx = functools.reduce(jnp.maximum, ms)
            ws = [jnp.exp(m - mx) for m in ms]
            add = lambda a, b: a + b
            den = functools.reduce(add, [w * lacc_ref[pi, rows, :] for pi, w in enumerate(ws)])
            num = functools.reduce(add, [w * oacc_ref[pi, rows, :] for pi, w in enumerate(ws)])
            o = num / den
            ms = _dot((o * o).astype(BF16), bd)
            out_ref[0, rows, :] = (o * lax.rsqrt(ms + RMS_EPS) * g).astype(out_ref.dtype)
        return carry

    lax.fori_loop(0, seq // (bb * MERGE_UNROLL), merge, 0)


def _head_mean_matrix(width):
    r = np.arange(width)
    return jnp.asarray(((r[:, None] // ATTN_HEAD_DIM) == (r[None, :] // ATTN_HEAD_DIM)) / ATTN_HEAD_DIM, BF16)


def _prompt_attention(qpm, kpm, vpm, band_bias, g_attn, n_seq, seq):
    n_pair = qpm.shape[0]
    n_pat = len(DILATED_PATTERNS)
    for _, d in DILATED_PATTERNS:
        assert seq % (d * BAND_BLOCK) == 0 and (seq // BAND_BLOCK) % (2 * ATTN_UNROLL) == 0
        nb = seq // (d * BAND_BLOCK)
        assert nb % min(ATTN_UNROLL, nb) == 0 and ATTN_UNROLL % min(ATTN_UNROLL, nb) == 0
    assert (seq // BAND_BLOCK) % MERGE_UNROLL == 0
    tok = pl.BlockSpec((1, seq, LANES), lambda p, b: (p, b, 0))
    bd = _head_mean_matrix(LANES)
    return pl.pallas_call(
        functools.partial(_prompt_attn_kernel, seq=seq),
        grid=(n_pair, n_seq),
        in_specs=[tok, tok, tok,
                  pl.BlockSpec((n_pat, 2, 2, BAND_BLOCK, 2 * BAND_BLOCK), lambda p, b: (0, 0, p, 0, 0)),
                  pl.BlockSpec((1, LANES), lambda p, b: (0, p)),
                  pl.BlockSpec(bd.shape, lambda p, b: (0, 0))],
        out_specs=tok,
        out_shape=jax.ShapeDtypeStruct(qpm.shape, BF16),
        scratch_shapes=[pltpu.VMEM((n_pat, seq, LANES), F32)] * 3
        + [pltpu.VMEM((2, 2 * ATTN_UNROLL, BAND_BLOCK, 2 * BAND_BLOCK), BF16)],
        compiler_params=pltpu.CompilerParams(
            dimension_semantics=("parallel", "parallel"), vmem_limit_bytes=VMEM_LIMIT),
        name="prompt_attn",
    )(qpm, kpm, vpm, band_bias, g_attn, bd)


def _sample_attn_kernel(q_ref, kn_ref, vn_ref, ckt_ref, cvt_ref, bias_ref, g_ref, bd_ref, o_ref, *, n_heads):
    t_new, width = q_ref.shape[1], q_ref.shape[2]
    n_past = ckt_ref.shape[3]
    rows = n_heads * t_new
    own = (lax.broadcasted_iota(jnp.int32, (rows, width), 0) // t_new
           == lax.broadcasted_iota(jnp.int32, (rows, width), 1) // ATTN_HEAD_DIM)
    qbd = jnp.where(own, jnp.concatenate([q_ref[0]] * n_heads, axis=0), 0.0).astype(BF16)

    pad = jnp.zeros((LANES - t_new, width), F32)
    kn = jnp.concatenate([kn_ref[0], pad], axis=0).astype(BF16)
    vn = jnp.concatenate([vn_ref[0], pad], axis=0).astype(BF16)
    s_c = _dot(qbd, ckt_ref[0, 0].astype(BF16)) + bias_ref[:, :n_past]
    s_n = _dot_nt(qbd, kn) + bias_ref[:, n_past:]
    m = jnp.maximum(jnp.max(s_c, axis=-1, keepdims=True), jnp.max(s_n, axis=-1, keepdims=True))
    p_c = jnp.exp(s_c - m)
    p_n = jnp.exp(s_n - m)
    l = jnp.sum(p_c, axis=-1, keepdims=True) + jnp.sum(p_n, axis=-1, keepdims=True)
    acc = _dot_nt(p_c.astype(BF16), cvt_ref[0, 0].astype(BF16)) + _dot(p_n.astype(BF16), vn)
    o = jnp.where(own, acc * (1.0 / l), 0.0)
    ms = _dot((o * o).astype(BF16), bd_ref[...])
    o = o * lax.rsqrt(ms + RMS_EPS) * g_ref[...]
    out = o[0:t_new]
    for h in range(1, n_heads):
        out = out + o[h * t_new:(h + 1) * t_new]
    o_ref[0] = out


def _sample_attention(q3, kn3, vn3, cache_kt, cache_vt, layer, bias, g_attn):
    n_seq, t_new, width = q3.shape
    n_past = cache_kt.shape[3]
    n_heads = width // ATTN_HEAD_DIM
    bd = _head_mean_matrix(width)
    tok = pl.BlockSpec((1, t_new, width), lambda b: (b, 0, 0))
    cache = pl.BlockSpec((1, 1, width, n_past), lambda b: (layer, b, 0, 0))
    full = lambda a: pl.BlockSpec(a.shape, lambda b: (0,) * a.ndim)
    return pl.pallas_call(
        functools.partial(_sample_attn_kernel, n_heads=n_heads),
        grid=(n_seq,),
        in_specs=[tok, tok, tok, cache, cache, full(bias), full(g_attn), full(bd)],
        out_specs=tok,
        out_shape=jax.ShapeDtypeStruct(q3.shape, F32),
        compiler_params=pltpu.CompilerParams(dimension_semantics=("parallel",), vmem_limit_bytes=VMEM_LIMIT),
        name="sample_attn",
    )(q3, kn3, vn3, cache_kt, cache_vt, bias, g_attn, bd)


def _mlp_kernel(x_ref, rec_ref, attn_ref, wout_ref, wup_ref, wdn_ref, gpost_ref, gpre_ref, gmpost_ref,
                y_ref, *, ff_chunk):
    d_rec = rec_ref.shape[1]
    mixed = _dot(rec_ref[...].astype(BF16), wout_ref[0:d_rec, :])
    for p in range(attn_ref.shape[0]):
        r0 = d_rec + p * LANES
        mixed = mixed + _dot(attn_ref[p].astype(BF16), wout_ref[r0:r0 + LANES, :])
    x1 = x_ref[...] + _rms(mixed, gpost_ref[...])
    h = _rms(x1, gpre_ref[...]).astype(BF16)
    acc = jnp.zeros(x1.shape, F32)
    for c0 in range(0, wup_ref.shape[1], ff_chunk):
        u = jnp.square(jnp.maximum(_dot(h, wup_ref[:, c0:c0 + ff_chunk]), 0.0))
        acc = acc + _dot(u.astype(BF16), wdn_ref[c0:c0 + ff_chunk, :])
    y_ref[...] = x1 + _rms(acc, gmpost_ref[...])


def _mlp(x2d, rec, attn_pm, wout, wup, wdn, g_post, g_pre, g_mpost, tm):
    t, dm = x2d.shape
    n_pair = attn_pm.shape[0]
    row = lambda wd: pl.BlockSpec((tm, wd), lambda i: (i, 0))
    once = lambda a: pl.BlockSpec(a.shape, lambda i: (0,) * a.ndim, pipeline_mode=pl.Buffered(1))
    return pl.pallas_call(
        functools.partial(_mlp_kernel, ff_chunk=1024),
        grid=(t // tm,),
        in_specs=[row(dm), row(rec.shape[1]), pl.BlockSpec((n_pair, tm, LANES), lambda i: (0, i, 0)),
                  once(wout), once(wup), once(wdn), once(g_post), once(g_pre), once(g_mpost)],
        out_specs=row(dm),
        out_shape=jax.ShapeDtypeStruct((t, dm), F32),
        compiler_params=pltpu.CompilerParams(dimension_semantics=("parallel",), vmem_limit_bytes=VMEM_LIMIT),
        name="mlp",
    )(x2d, rec, attn_pm, wout, wup, wdn, g_post, g_pre, g_mpost)


def kernel(x_prompt, x_sample, state_hgrn, cache_k, cache_v, rel_bias, lb_raw, w_in, w_out, w_up, w_down,
           g_mix_pre, g_mix_post, g_mlp_pre, g_mlp_post, g_rec_out, g_attn_out):
    n_p, seq, dm = x_prompt.shape
    n_s, t_new, _ = x_sample.shape
    depth = w_in.shape[0]
    n_past = cache_k.shape[2]
    n_heads, dh = cache_k.shape[3], cache_k.shape[4]
    d_attn = n_heads * dh
    d_rec = lb_raw.shape[1]
    n_rec_heads = d_rec // REC_EXPAND
    keep = min(MAX_WINDOW, seq)
    tm = 512
    tm_s = min(tm, n_s * t_new)
    assert dh == ATTN_HEAD_DIM and CHUNK % t_new == 0 and seq % tm == 0
    assert (n_s * t_new) % CHUNK == 0 and (n_s * t_new) % tm_s == 0

    band_idx, band_add = _band_tables()
    band_bias = _expand_bias(rel_bias, band_idx, band_add)
    band_bias = band_bias.reshape(len(DILATED_PATTERNS), 2, n_heads, BAND_BLOCK, 2 * BAND_BLOCK)
    s_idx, s_add = _sample_tables(n_past, t_new)
    sample_bias = _expand_bias(rel_bias, s_idx, s_add).reshape(n_heads * t_new, n_past + LANES)

    w_in_b, w_out_b = w_in.astype(BF16), w_out.astype(BF16)
    w_up_b, w_dn_b = w_up.astype(BF16), w_down.astype(BF16)
    c_k = 4 * d_rec + d_attn
    wkvt_b = jnp.swapaxes(w_in_b[:, :, c_k:c_k + 2 * d_attn], 1, 2)
    row = lambda a, l: a[l][None, :]

    cache_kt = cache_k.transpose(0, 1, 3, 4, 2).reshape(depth, n_s, d_attn, n_past).astype(F32)
    cache_vt = cache_v.transpose(0, 1, 3, 4, 2).reshape(depth, n_s, d_attn, n_past).astype(F32)

    yp = x_prompt.reshape(n_p * seq, dm)
    ys = x_sample.reshape(n_s * t_new, dm)
    zeros_state = jnp.zeros((1, n_p, n_rec_heads, REC_EXPAND, REC_DV), F32)
    state_in = state_hgrn.astype(F32)
    rec_p, k_p, v_p, rec_s, k_s, v_s = [], [], [], [], [], []
    for l in range(depth):
        g_pre, g_rec, g_att = row(g_mix_pre, l), row(g_rec_out, l), row(g_attn_out, l)
        post = (row(g_mix_post, l), row(g_mlp_pre, l), row(g_mlp_post, l))

        qr, kr, vr, lf, gate, qpm, kpm, vpm, kt, vt = _inproj(
            yp, g_pre, w_in_b[l], wkvt_b[l], lb_raw, l, tm, seq_keep=(seq, keep))
        rec, s_new = _hgrn(qr, kr, vr, lf, gate, zeros_state, 0, g_rec, CHUNK, HGRN_ROWS, seq // HGRN_ROWS)
        attn = _prompt_attention(qpm, kpm, vpm, band_bias, g_att, n_p, seq)
        yp = _mlp(yp, rec, attn, w_out_b[l], w_up_b[l], w_dn_b[l], *post, tm)
        rec_p.append(s_new)
        k_p.append(kt)
        v_p.append(vt)

        qr, kr, vr, lf, gate, qpm, kpm, vpm, knat, vnat = _inproj(
            ys, g_pre, w_in_b[l], wkvt_b[l], lb_raw, l, tm_s)
        rec, s_new = _hgrn(qr, kr, vr, lf, gate, state_in, l, g_rec, t_new, CHUNK, 1)
        q3 = qpm.transpose(1, 0, 2).reshape(n_s, t_new, d_attn)
        attn = _sample_attention(q3, knat.reshape(n_s, t_new, d_attn), vnat.reshape(n_s, t_new, d_attn),
                                 cache_kt, cache_vt, l, sample_bias, g_att)
        attn_pm = attn.reshape(n_s * t_new, d_attn // LANES, LANES).transpose(1, 0, 2)
        ys = _mlp(ys, rec, attn_pm, w_out_b[l], w_up_b[l], w_dn_b[l], *post, tm_s)
        rec_s.append(s_new)
        k_s.append(knat.reshape(n_s, t_new, n_heads, dh))
        v_s.append(vnat.reshape(n_s, t_new, n_heads, dh))

    to_cache = lambda xs: jnp.stack(xs).reshape(depth, n_p, n_heads, dh, keep).transpose(0, 1, 4, 2, 3)
    return (yp.reshape(n_p, seq, dm), ys.reshape(n_s, t_new, dm), jnp.stack(rec_p), to_cache(k_p),
            to_cache(v_p), jnp.stack(rec_s), jnp.stack(k_s), jnp.stack(v_s))
```

```python
import functools
import math

import numpy as np
import jax
import jax.numpy as jnp
from jax import lax
from jax.experimental import pallas as pl
from jax.experimental.pallas import tpu as pltpu

F32 = jnp.float32
BF16 = jnp.bfloat16

ATTN_HEAD_DIM = 64
REC_EXPAND = 128
REC_DV = 128
DILATED_PATTERNS = ((128, 1), (512, 4), (2048, 16))
MAX_WINDOW = max(w for w, _ in DILATED_PATTERNS)
BAND_BLOCK = 128
NUM_BUCKETS = 32
MAX_DISTANCE = MAX_WINDOW
RMS_EPS = 1e-6
NEG_INF = -1e30
MIN_FORGET = 1e-30

LANES = 128
CHUNK = 128
HGRN_ROWS = 512
VMEM_LIMIT = 56 * 1024 * 1024


def _dot(a, b):
    return jnp.dot(a, b, preferred_element_type=F32)


def _dot_nt(a, b):
    return lax.dot_general(a, b, (((1,), (1,)), ((), ())), preferred_element_type=F32)


def _split2(x):
    hi = x.astype(BF16)
    lo = (x - hi.astype(F32)).astype(BF16)
    return jnp.concatenate([hi, lo], axis=1)


def _sum2(y):
    n = y.shape[1] // 2
    return y[:, :n] + y[:, n:]


def _rms(x, g):
    ms = jnp.mean(x * x, axis=-1, keepdims=True)
    return x * lax.rsqrt(ms + RMS_EPS) * g


def _rel_bucket_np(dist):
    exact = NUM_BUCKETS // 2
    d = np.maximum(dist, exact).astype(np.float64)
    large = exact + (np.log(d / exact) / math.log(MAX_DISTANCE / exact) * (NUM_BUCKETS - exact)).astype(np.int64)
    large = np.minimum(large, NUM_BUCKETS - 1)
    return np.where(dist < exact, dist, large).astype(np.int32)


def _band_tables():
    qi = np.arange(BAND_BLOCK)[:, None]
    ki = np.arange(2 * BAND_BLOCK)[None, :]
    idx, add = [], []
    for w, d in DILATED_PATTERNS:
        taps = w // d
        tap = qi + BAND_BLOCK - ki
        valid = (tap >= 0) & (tap <= taps)
        bucket = _rel_bucket_np(np.maximum(tap, 0) * d)
        for first in (False, True):
            v = valid & (ki >= BAND_BLOCK) if first else valid
            idx.append(bucket)
            add.append(np.where(v, 0.0, NEG_INF))
    return np.stack(idx).astype(np.int32), np.stack(add).astype(np.float32)


def _sample_tables(n_past, t_new):
    t = np.arange(t_new)[:, None]
    col = np.arange(n_past + LANES)[None, :]
    dist = np.where(col < n_past, n_past + t - col, t - (col - n_past))
    ok = (dist >= 0) & ((col < n_past) | (col - n_past < t_new))
    cnt = np.zeros(dist.shape, np.int64)
    for w, d in DILATED_PATTERNS:
        cnt += (ok & (dist % d == 0) & (dist <= w)).astype(np.int64)
    idx = _rel_bucket_np(np.maximum(dist, 0))
    add = np.where(cnt > 0, np.log(np.maximum(cnt, 1)), NEG_INF)
    return idx[None].astype(np.int32), add[None].astype(np.float32)


def _levels(seg):
    out, h = [], seg // 2
    while h >= 1:
        out.append(h)
        h //= 2
    return out


def _hgrn_tables(seg):
    r = np.arange(CHUNK)
    same_seg = (r[:, None] // seg) == (r[None, :] // seg)
    tri = (same_seg & (r[None, :] <= r[:, None])).astype(np.float32)
    eye = np.eye(CHUNK, dtype=np.float32)
    blocks = []
    lv = np.full((CHUNK, CHUNK), -1, np.int32)
    levels = _levels(seg)
    for i, h in enumerate(levels):
        ref = r - (r % (2 * h)) + h - 1
        g = np.zeros((CHUNK, CHUNK), np.float32)
        g[r, ref] = 1.0
        second = (r % (2 * h)) >= h
        sign = np.where(second, 1.0, -1.0)[:, None].astype(np.float32)
        blocks.append((sign * (eye - g)) @ tri)
        pair = ((r[:, None] // (2 * h)) == (r[None, :] // (2 * h))) & second[:, None] & (~second)[None, :]
        lv[pair] = i
    lv[r, r] = len(levels)
    seg_end = r - (r % seg) + seg - 1
    gl = np.zeros((CHUNK, CHUNK), np.float32)
    gl[r, seg_end] = 1.0
    blocks.append(tri)
    blocks.append((gl - eye) @ tri)
    mat = np.concatenate(blocks, axis=0)
    assert np.all(np.isin(mat, (-1.0, 0.0, 1.0)))
    return mat, lv


def _bias_kernel(rb_ref, idx_ref, add_ref, out_ref):
    idx = idx_ref[0]
    for h in range(out_ref.shape[1]):
        acc = add_ref[0]
        for b in range(NUM_BUCKETS):
            acc = acc + jnp.where(idx == b, rb_ref[b, h], 0.0)
        out_ref[0, h] = acc


def _expand_bias(rel_bias, idx, add):
    n, r, c = idx.shape
    n_heads = rel_bias.shape[1]
    return pl.pallas_call(
        _bias_kernel,
        grid=(n,),
        in_specs=[
            pl.BlockSpec(memory_space=pltpu.SMEM),
            pl.BlockSpec((1, r, c), lambda i: (i, 0, 0)),
            pl.BlockSpec((1, r, c), lambda i: (i, 0, 0)),
        ],
        out_specs=pl.BlockSpec((1, n_heads, r, c), lambda i: (i, 0, 0, 0)),
        out_shape=jax.ShapeDtypeStruct((n, n_heads, r, c), F32),
        name="bias_expand",
    )(rel_bias.astype(F32), jnp.asarray(idx), jnp.asarray(add))


def _inproj_kernel(x_ref, g_ref, w_ref, wkvt_ref, lbraw_ref,
                   qr_ref, kr_ref, vr_ref, lf_ref, gate_ref,
                   qpm_ref, kpm_ref, vpm_ref, kout_ref, vout_ref, *, layer, d_rec, d_attn, cache_t):
    h = _rms(x_ref[...], g_ref[...]).astype(BF16)

    raw = lbraw_ref[...]
    e = jnp.exp(raw - jnp.max(raw, axis=0, keepdims=True))
    soft = e / jnp.sum(e, axis=0, keepdims=True)
    cum = soft[0:1]
    for i in range(1, layer + 1):
        cum = cum + soft[i:i + 1]
    lb = cum - soft[0:1]

    def proj(c0, width):
        return _dot(h, w_ref[0, :, c0:c0 + width])

    p = proj(0, d_rec)
    qr_ref[...] = p * jax.nn.sigmoid(p)
    z = proj(d_rec, d_rec)
    f = lb + (1.0 - lb) * jax.nn.sigmoid(z)
    lf_ref[...] = jnp.log(jnp.maximum(f, MIN_FORGET))
    kr_ref[...] = (1.0 - lb) * jax.nn.sigmoid(-z)
    vr_ref[...] = proj(2 * d_rec, d_rec).astype(vr_ref.dtype)
    p = proj(3 * d_rec, d_rec)
    gate_ref[...] = (p * jax.nn.sigmoid(p)).astype(gate_ref.dtype)

    c0 = 4 * d_rec
    qa = proj(c0, d_attn) * (ATTN_HEAD_DIM ** -0.5)
    ka = proj(c0 + d_attn, d_attn)
    va = proj(c0 + 2 * d_attn, d_attn)
    for pr in range(d_attn // LANES):
        sl = slice(pr * LANES, (pr + 1) * LANES)
        qpm_ref[pr] = qa[:, sl]
        kpm_ref[pr] = ka[:, sl]
        vpm_ref[pr] = va[:, sl]
    if cache_t is None:
        kout_ref[...] = ka
        vout_ref[...] = va
    else:
        blocks_per_seq, first_kept = cache_t

        @pl.when(pl.program_id(0) % blocks_per_seq >= first_kept)
        def _():
            kout_ref[0] = _dot_nt(wkvt_ref[0, 0:d_attn, :], h)
            vout_ref[0] = _dot_nt(wkvt_ref[0, d_attn:2 * d_attn, :], h)


def _layer_slab(a, layer):
    return pl.BlockSpec((1,) + a.shape[1:], lambda i: (layer, 0, 0), pipeline_mode=pl.Buffered(1))


def _inproj(x2d, g, w_bf, wkvt_bf, lb_raw, layer, tm, seq_keep=None):
    t, dm = x2d.shape
    d_rec = lb_raw.shape[1]
    d_attn = (w_bf.shape[2] - 4 * d_rec) // 3
    n_pair = d_attn // LANES
    nat = lambda wd: pl.BlockSpec((tm, wd), lambda i: (i, 0))
    pm = pl.BlockSpec((n_pair, tm, LANES), lambda i: (0, i, 0))
    full = lambda a: pl.BlockSpec(a.shape, lambda i: (0,) * a.ndim)
    f32 = lambda *s: jax.ShapeDtypeStruct(s, F32)
    if seq_keep is None:
        cache_t, kv_spec, kv_shape = None, nat(d_attn), f32(t, d_attn)
    else:
        seq, keep = seq_keep
        assert seq % tm == 0 and keep % tm == 0
        bps, first = seq // tm, (seq - keep) // tm
        cache_t = (bps, first)
        kv_spec = pl.BlockSpec((1, d_attn, tm), lambda i: (i // bps, 0, jnp.maximum(i % bps - first, 0)))
        kv_shape = f32(t // seq, d_attn, keep)
    return pl.pallas_call(
        functools.partial(_inproj_kernel, layer=layer, d_rec=d_rec, d_attn=d_attn, cache_t=cache_t),
        grid=(t // tm,),
        in_specs=[nat(dm), full(g), _layer_slab(w_bf, layer), _layer_slab(wkvt_bf, layer), full(lb_raw)],
        out_specs=[nat(d_rec)] * 5 + [pm] * 3 + [kv_spec] * 2,
        out_shape=[f32(t, d_rec), f32(t, d_rec), jax.ShapeDtypeStruct((t, d_rec), BF16), f32(t, d_rec),
                   jax.ShapeDtypeStruct((t, d_rec), BF16)] + [f32(n_pair, t, LANES)] * 3 + [kv_shape] * 2,
        compiler_params=pltpu.CompilerParams(dimension_semantics=("arbitrary",), vmem_limit_bytes=VMEM_LIMIT),
        name="inproj",
    )(x2d, g, w_bf, wkvt_bf, lb_raw)


def _hgrn_kernel(q_ref, k_ref, v_ref, lf_ref, gate_ref, s0_ref, gn_ref, mat_ref, lv_ref,
                 o_ref, sout_ref, st_ref, *, seg, n_chunks):
    nseg = CHUNK // seg
    seg_shift = seg.bit_length() - 1
    levels = _levels(seg)
    nl = len(levels)
    ti = pl.program_id(2)

    @pl.when(ti == 0)
    def _():
        st_ref[...] = s0_ref[0, :, 0]

    rowi = lax.broadcasted_iota(jnp.int32, (CHUNK, CHUNK), 0)
    coli = lax.broadcasted_iota(jnp.int32, (CHUNK, CHUNK), 1)
    mat = mat_ref[...]
    lv = lv_ref[...]
    gn = gn_ref[...]

    chunk_rows = [slice(c * CHUNK, (c + 1) * CHUNK) for c in range(n_chunks)]
    lf_split = [_split2(lf_ref[rows, :]) for rows in chunk_rows]
    e_lv = _dot(mat[:nl * CHUNK], jnp.concatenate([s[:, :LANES] for s in lf_split], axis=1))
    e_bl = _dot(mat[nl * CHUNK:], jnp.concatenate(lf_split, axis=1))

    intra = []
    for c, rows in enumerate(chunk_rows):
        q = q_ref[rows, :]
        k = k_ref[rows, :]
        a = jnp.zeros((CHUNK, CHUNK), F32)
        for i, h in enumerate(levels):
            second = (rowi & (2 * h - 1)) >= h
            e = e_lv[i * CHUNK:(i + 1) * CHUNK, c * LANES:(c + 1) * LANES]
            x = (jnp.where(second, q, k) * jnp.exp(e)).astype(BF16)
            a = jnp.where(lv == i, _dot_nt(x, x), a)
        a = jnp.where(lv == nl, _dot_nt(q.astype(BF16), k.astype(BF16)), a)
        intra.append(a.astype(BF16))

    states = [st_ref[j] for j in range(nseg)]
    for c, rows in enumerate(chunk_rows):
        q = q_ref[rows, :]
        k = k_ref[rows, :]
        vb = v_ref[rows, :].astype(BF16)
        e = _sum2(e_bl[:, 2 * c * LANES:2 * (c + 1) * LANES])
        eb = jnp.exp(e[:CHUNK])
        qe = (q * eb).astype(BF16)
        kd_t = (k * jnp.exp(e[CHUNK:])).T
        dec_t = eb.T
        o = _dot(intra[c], vb)
        for j in range(nseg):
            s_old = states[j]
            oj = _dot(qe, s_old.astype(BF16))
            kd_j = kd_t
            if nseg > 1:
                oj = jnp.where((rowi >> seg_shift) == j, oj, 0.0)
                kd_j = jnp.where((coli >> seg_shift) == j, kd_t, 0.0)
            o = o + oj
            col = j * seg + seg - 1
            states[j] = s_old * dec_t[:, col:col + 1] + _dot(kd_j.astype(BF16), vb)
        o_ref[rows, :] = (_rms(o, gn) * gate_ref[rows, :]).astype(o_ref.dtype)

    for j in range(nseg):
        st_ref[j] = states[j]

    @pl.when(ti == pl.num_programs(2) - 1)
    def _():
        sout_ref[:, 0] = st_ref[...]


N_HGRN_IN = 9
N_ATTN_IN = 8


def _hgrn_with_sample_attn_kernel(*refs, seg, n_chunks, n_attn_heads):
    hgrn_in, attn_in = refs[:N_HGRN_IN], refs[N_HGRN_IN:N_HGRN_IN + N_ATTN_IN]
    o_ref, sout_ref, attn_out_ref, st_ref = refs[N_HGRN_IN + N_ATTN_IN:]
    _sample_attn_kernel(*attn_in, attn_out_ref, n_heads=n_attn_heads)
    _hgrn_kernel(*hgrn_in, o_ref, sout_ref, st_ref, seg=seg, n_chunks=n_chunks)


def _hgrn(qr, kr, vr, lf, gate, s0_all, layer, g_norm, seg, rows_per_step, steps_per_seq, sample_attn=None):
    t, d_rec = qr.shape
    n_heads = d_rec // LANES
    nseg = CHUNK // seg
    n_seq = s0_all.shape[1]
    mat, lv = _hgrn_tables(seg)
    grid = (n_seq // nseg, n_heads, steps_per_seq)
    tok = pl.BlockSpec((rows_per_step, LANES), lambda bi, h, ti: (bi * steps_per_seq + ti, h))
    st_in = pl.BlockSpec((1, nseg, 1, REC_EXPAND, REC_DV), lambda bi, h, ti: (layer, bi, h, 0, 0))
    st_out = pl.BlockSpec((nseg, 1, REC_EXPAND, REC_DV), lambda bi, h, ti: (bi, h, 0, 0))
    full = lambda a: pl.BlockSpec(a.shape, lambda bi, h, ti: (0,) * a.ndim)
    mat_j, lv_j = jnp.asarray(mat, BF16), jnp.asarray(lv)
    operands = [qr, kr, vr, lf, gate, s0_all, g_norm, mat_j, lv_j]
    in_specs = [tok] * 5 + [st_in, pl.BlockSpec((1, LANES), lambda bi, h, ti: (0, h)), full(mat_j), full(lv_j)]
    out_specs = [tok, st_out]
    out_shape = [jax.ShapeDtypeStruct((t, d_rec), BF16),
                 jax.ShapeDtypeStruct((n_seq, n_heads, REC_EXPAND, REC_DV), F32)]
    body = functools.partial(_hgrn_kernel, seg=seg, n_chunks=rows_per_step // CHUNK)
    if sample_attn is not None:
        q3, kn3, vn3, cache_kt, cache_vt, cache_layer, bias, g_attn = sample_attn
        n_seq_s, t_new, width = q3.shape
        n_past = cache_kt.shape[3]
        n_steps = grid[0] * grid[1] * grid[2]
        assert n_seq_s % n_steps == 0
        per_step = n_seq_s // n_steps
        step = lambda bi, h, ti: (bi * n_heads + h) * steps_per_seq + ti
        tok_s = pl.BlockSpec((per_step, t_new, width), lambda bi, h, ti: (step(bi, h, ti), 0, 0))
        cache = pl.BlockSpec((1, per_step, width, n_past), lambda bi, h, ti: (cache_layer, step(bi, h, ti), 0, 0))
        bd = _head_mean_matrix(width)
        operands += [q3, kn3, vn3, cache_kt, cache_vt, bias, g_attn, bd]
        in_specs += [tok_s, tok_s, tok_s, cache, cache, full(bias), full(g_attn), full(bd)]
        out_specs.append(tok_s)
        out_shape.append(jax.ShapeDtypeStruct(q3.shape, F32))
        body = functools.partial(_hgrn_with_sample_attn_kernel, seg=seg, n_chunks=rows_per_step // CHUNK,
                                 n_attn_heads=width // ATTN_HEAD_DIM)
    return pl.pallas_call(
        body,
        grid=grid,
        in_specs=in_specs,
        out_specs=out_specs,
        out_shape=out_shape,
        scratch_shapes=[pltpu.VMEM((nseg, REC_EXPAND, REC_DV), F32)],
        compiler_params=pltpu.CompilerParams(
            dimension_semantics=("parallel", "parallel", "arbitrary"), vmem_limit_bytes=VMEM_LIMIT),
        name="hgrn",
    )(*operands)


ATTN_UNROLL = 4
MERGE_UNROLL = 4


def _prompt_attn_kernel(q_ref, k_ref, v_ref, bias_ref, g_ref, bd_ref, out_ref,
                        oacc_ref, macc_ref, lacc_ref, p_ref, *, seq):
    lane = lax.broadcasted_iota(jnp.int32, (1, LANES), 1)
    lo_half = lane < ATTN_HEAD_DIM
    keeps = (lo_half, jnp.logical_not(lo_half))
    one = jnp.ones((), BF16)
    bb = BAND_BLOCK

    def runs(gi, d, nb):
        run = min(ATTN_UNROLL, nb)
        runs_per_res = nb // run
        for w in range(ATTN_UNROLL // run):
            n = gi * (ATTN_UNROLL // run) + w
            r = n // runs_per_res
            i0 = (n % runs_per_res) * run
            base = r + i0 * (d * bb)
            psl = pl.ds(jnp.maximum(base - d * bb, r), bb, stride=d)
            yield base, i0, psl, run, w * run

    def stage_scores(pi, d, nb, gi, slot):
        for base, i0, psl, run, u0 in runs(gi, d, nb):
            k_prev = k_ref[0, psl, :].astype(BF16)
            for j in range(run):
                qsl = pl.ds(base + j * (d * bb), bb, stride=d)
                k_cur = k_ref[0, qsl, :].astype(BF16)
                k2 = jnp.concatenate([k_prev, k_cur], axis=0)
                q2 = q_ref[0, qsl, :]
                variant = jnp.where(i0 == 0, 1, 0) if j == 0 else 0
                ms = []
                for hh in range(2):
                    qm = jnp.where(keeps[hh], q2, 0.0).astype(BF16)
                    s = _dot_nt(qm, k2) + bias_ref[pi, variant, hh]
                    m = jnp.max(s, axis=-1, keepdims=True)
                    p_ref[slot, 2 * (u0 + j) + hh] = jnp.exp(s - m).astype(BF16)
                    ms.append(m)
                macc_ref[pi, qsl, :] = jnp.where(lo_half, ms[0], ms[1])
                k_prev = k_cur

    def stage_values(pi, d, nb, gi, slot):
        for base, i0, psl, run, u0 in runs(gi, d, nb):
            v_prev = v_ref[0, psl, :].astype(BF16)
            for j in range(run):
                qsl = pl.ds(base + j * (d * bb), bb, stride=d)
                v_cur = v_ref[0, qsl, :].astype(BF16)
                v2 = jnp.concatenate([v_prev, v_cur], axis=0)
                pvs = [_dot(p_ref[slot, 2 * (u0 + j) + hh], jnp.where(keeps[hh], v2, one)) for hh in range(2)]
                oacc_ref[pi, qsl, :] = jnp.where(lo_half, pvs[0], pvs[1])
                lacc_ref[pi, qsl, :] = pltpu.roll(jnp.where(lo_half, pvs[1], pvs[0]), ATTN_HEAD_DIM, axis=1)
                v_prev = v_cur

    for pi, (_, d) in enumerate(DILATED_PATTERNS):
        nb = seq // (d * bb)
        n_groups = (d * nb) // ATTN_UNROLL
        stage_scores(pi, d, nb, 0, 0)

        def overlapped(k, carry, pi=pi, d=d, nb=nb):
            stage_scores(pi, d, nb, 2 * k + 1, 1)
            stage_values(pi, d, nb, 2 * k, 0)
            stage_scores(pi, d, nb, 2 * k + 2, 0)
            stage_values(pi, d, nb, 2 * k + 1, 1)
            return carry

        lax.fori_loop(0, n_groups // 2 - 1, overlapped, 0)
        stage_scores(pi, d, nb, n_groups - 1, 1)
        stage_values(pi, d, nb, n_groups - 2, 0)
        stage_values(pi, d, nb, n_groups - 1, 1)

    g = g_ref[...]
    bd = bd_ref[...]

    def merge(gi, carry):
        for u in range(MERGE_UNROLL):
            rows = pl.ds(pl.multiple_of((gi * MERGE_UNROLL + u) * bb, bb), bb)
            ms = [macc_ref[pi, rows, :] for pi in range(len(DILATED_PATTERNS))]
            mx = functools.reduce(jnp.maximum, ms)
            ws = [jnp.exp(m - mx) for m in ms]
            add = lambda a, b: a + b
            den = functools.reduce(add, [w * lacc_ref[pi, rows, :] for pi, w in enumerate(ws)])
            num = functools.reduce(add, [w * oacc_ref[pi, rows, :] for pi, w in enumerate(ws)])
            o = num / den
            ms = _dot((o * o).astype(BF16), bd)
            out_ref[0, rows, :] = (o * lax.rsqrt(ms + RMS_EPS) * g).astype(out_ref.dtype)
        return carry

    lax.fori_loop(0, seq // (bb * MERGE_UNROLL), merge, 0)


def _head_mean_matrix(width):
    r = np.arange(width)
    return jnp.asarray(((r[:, None] // ATTN_HEAD_DIM) == (r[None, :] // ATTN_HEAD_DIM)) / ATTN_HEAD_DIM, BF16)


def _prompt_attention(qpm, kpm, vpm, band_bias, g_attn, n_seq, seq):
    n_pair = qpm.shape[0]
    n_pat = len(DILATED_PATTERNS)
    for _, d in DILATED_PATTERNS:
        assert seq % (d * BAND_BLOCK) == 0 and (seq // BAND_BLOCK) % (2 * ATTN_UNROLL) == 0
        nb = seq // (d * BAND_BLOCK)
        assert nb % min(ATTN_UNROLL, nb) == 0 and ATTN_UNROLL % min(ATTN_UNROLL, nb) == 0
    assert (seq // BAND_BLOCK) % MERGE_UNROLL == 0
    tok = pl.BlockSpec((1, seq, LANES), lambda p, b: (p, b, 0))
    bd = _head_mean_matrix(LANES)
    return pl.pallas_call(
        functools.partial(_prompt_attn_kernel, seq=seq),
        grid=(n_pair, n_seq),
        in_specs=[tok, tok, tok,
                  pl.BlockSpec((n_pat, 2, 2, BAND_BLOCK, 2 * BAND_BLOCK), lambda p, b: (0, 0, p, 0, 0)),
                  pl.BlockSpec((1, LANES), lambda p, b: (0, p)),
                  pl.BlockSpec(bd.shape, lambda p, b: (0, 0))],
        out_specs=tok,
        out_shape=jax.ShapeDtypeStruct(qpm.shape, BF16),
        scratch_shapes=[pltpu.VMEM((n_pat, seq, LANES), F32)] * 3
        + [pltpu.VMEM((2, 2 * ATTN_UNROLL, BAND_BLOCK, 2 * BAND_BLOCK), BF16)],
        compiler_params=pltpu.CompilerParams(
            dimension_semantics=("parallel", "parallel"), vmem_limit_bytes=VMEM_LIMIT),
        name="prompt_attn",
    )(qpm, kpm, vpm, band_bias, g_attn, bd)


def _sample_attn_kernel(q_ref, kn_ref, vn_ref, ckt_ref, cvt_ref, bias_ref, g_ref, bd_ref, o_ref, *, n_heads):
    n_seq, t_new, width = q_ref.shape
    n_past = ckt_ref.shape[3]
    rows = n_heads * t_new
    own = (lax.broadcasted_iota(jnp.int32, (rows, width), 0) // t_new
           == lax.broadcasted_iota(jnp.int32, (rows, width), 1) // ATTN_HEAD_DIM)
    pad = jnp.zeros((LANES - t_new, width), F32)
    for s in range(n_seq):
        qbd = jnp.where(own, jnp.concatenate([q_ref[s]] * n_heads, axis=0), 0.0).astype(BF16)
        kn = jnp.concatenate([kn_ref[s], pad], axis=0).astype(BF16)
        vn = jnp.concatenate([vn_ref[s], pad], axis=0).astype(BF16)
        s_c = _dot(qbd, ckt_ref[0, s].astype(BF16)) + bias_ref[:, :n_past]
        s_n = _dot_nt(qbd, kn) + bias_ref[:, n_past:]
        m = jnp.maximum(jnp.max(s_c, axis=-1, keepdims=True), jnp.max(s_n, axis=-1, keepdims=True))
        p_c = jnp.exp(s_c - m)
        p_n = jnp.exp(s_n - m)
        l = jnp.sum(p_c, axis=-1, keepdims=True) + jnp.sum(p_n, axis=-1, keepdims=True)
        p_cb = p_c.astype(BF16)
        acc = jnp.concatenate(
            [_dot_nt(p_cb, cvt_ref[0, s, f0:f0 + LANES, :].astype(BF16)) for f0 in range(0, width, LANES)],
            axis=1) + _dot(p_n.astype(BF16), vn)
        o = jnp.where(own, acc * (1.0 / l), 0.0)
        ms = _dot((o * o).astype(BF16), bd_ref[...])
        o = o * lax.rsqrt(ms + RMS_EPS) * g_ref[...]
        out = o[0:t_new]
        for h in range(1, n_heads):
            out = out + o[h * t_new:(h + 1) * t_new]
        o_ref[s] = out


def _mlp_kernel(x_ref, rec_ref, attn_ref, wout_ref, wup_ref, wdn_ref, gpost_ref, gpre_ref, gmpost_ref,
                y_ref, *, ff_chunk):
    d_rec = rec_ref.shape[1]
    mixed = _dot(rec_ref[...].astype(BF16), wout_ref[0, 0:d_rec, :])
    for p in range(attn_ref.shape[0]):
        r0 = d_rec + p * LANES
        mixed = mixed + _dot(attn_ref[p].astype(BF16), wout_ref[0, r0:r0 + LANES, :])
    x1 = x_ref[...] + _rms(mixed, gpost_ref[...])
    h = _rms(x1, gpre_ref[...]).astype(BF16)
    acc = jnp.zeros(x1.shape, F32)
    for c0 in range(0, wup_ref.shape[2], ff_chunk):
        u = jnp.square(jnp.maximum(_dot(h, wup_ref[0, :, c0:c0 + ff_chunk]), 0.0))
        acc = acc + _dot(u.astype(BF16), wdn_ref[0, c0:c0 + ff_chunk, :])
    y_ref[...] = x1 + _rms(acc, gmpost_ref[...])


def _mlp(x2d, rec, attn_pm, wout, wup, wdn, layer, g_post, g_pre, g_mpost, tm):
    t, dm = x2d.shape
    n_pair = attn_pm.shape[0]
    row = lambda wd: pl.BlockSpec((tm, wd), lambda i: (i, 0))
    once = lambda a: pl.BlockSpec(a.shape, lambda i: (0,) * a.ndim, pipeline_mode=pl.Buffered(1))
    return pl.pallas_call(
        functools.partial(_mlp_kernel, ff_chunk=1024),
        grid=(t // tm,),
        in_specs=[row(dm), row(rec.shape[1]), pl.BlockSpec((n_pair, tm, LANES), lambda i: (0, i, 0)),
                  _layer_slab(wout, layer), _layer_slab(wup, layer), _layer_slab(wdn, layer),
                  once(g_post), once(g_pre), once(g_mpost)],
        out_specs=row(dm),
        out_shape=jax.ShapeDtypeStruct((t, dm), F32),
        compiler_params=pltpu.CompilerParams(dimension_semantics=("parallel",), vmem_limit_bytes=VMEM_LIMIT),
        name="mlp",
    )(x2d, rec, attn_pm, wout, wup, wdn, g_post, g_pre, g_mpost)


def kernel(x_prompt, x_sample, state_hgrn, cache_k, cache_v, rel_bias, lb_raw, w_in, w_out, w_up, w_down,
           g_mix_pre, g_mix_post, g_mlp_pre, g_mlp_post, g_rec_out, g_attn_out):
    n_p, seq, dm = x_prompt.shape
    n_s, t_new, _ = x_sample.shape
    depth = w_in.shape[0]
    n_past = cache_k.shape[2]
    n_heads, dh = cache_k.shape[3], cache_k.shape[4]
    d_attn = n_heads * dh
    d_rec = lb_raw.shape[1]
    n_rec_heads = d_rec // REC_EXPAND
    keep = min(MAX_WINDOW, seq)
    tm = 512
    tm_s = min(tm, n_s * t_new)
    assert dh == ATTN_HEAD_DIM and CHUNK % t_new == 0 and seq % tm == 0
    assert (n_s * t_new) % CHUNK == 0 and (n_s * t_new) % tm_s == 0

    band_idx, band_add = _band_tables()
    band_bias = _expand_bias(rel_bias, band_idx, band_add)
    band_bias = band_bias.reshape(len(DILATED_PATTERNS), 2, n_heads, BAND_BLOCK, 2 * BAND_BLOCK)
    s_idx, s_add = _sample_tables(n_past, t_new)
    sample_bias = _expand_bias(rel_bias, s_idx, s_add).reshape(n_heads * t_new, n_past + LANES)

    w_in_b, w_out_b = w_in.astype(BF16), w_out.astype(BF16)
    w_up_b, w_dn_b = w_up.astype(BF16), w_down.astype(BF16)
    c_k = 4 * d_rec + d_attn
    wkvt_b = jnp.swapaxes(w_in[:, :, c_k:c_k + 2 * d_attn], 1, 2).astype(BF16)
    row = lambda a, l: a[l][None, :]

    cache_kt = cache_k.transpose(0, 1, 3, 4, 2).reshape(depth, n_s, d_attn, n_past).astype(F32)
    cache_vt = cache_v.transpose(0, 1, 3, 4, 2).reshape(depth, n_s, d_attn, n_past).astype(F32)

    yp = x_prompt.reshape(n_p * seq, dm)
    ys = x_sample.reshape(n_s * t_new, dm)
    zeros_state = jnp.zeros((1, n_p, n_rec_heads, REC_EXPAND, REC_DV), F32)
    state_in = state_hgrn.astype(F32)
    rec_p, k_p, v_p, rec_s, k_s, v_s = [], [], [], [], [], []
    for l in range(depth):
        g_pre, g_rec, g_att = row(g_mix_pre, l), row(g_rec_out, l), row(g_attn_out, l)
        post = (row(g_mix_post, l), row(g_mlp_pre, l), row(g_mlp_post, l))

        qr, kr, vr, lf, gate, qpm, kpm, vpm, kt, vt = _inproj(
            yp, g_pre, w_in_b, wkvt_b, lb_raw, l, tm, seq_keep=(seq, keep))
        qr_s, kr_s, vr_s, lf_s, gate_s, qpm_s, _, _, knat, vnat = _inproj(
            ys, g_pre, w_in_b, wkvt_b, lb_raw, l, tm_s)
        q3 = qpm_s.transpose(1, 0, 2).reshape(n_s, t_new, d_attn)
        rider = (q3, knat.reshape(n_s, t_new, d_attn), vnat.reshape(n_s, t_new, d_attn),
                 cache_kt, cache_vt, l, sample_bias, g_att)

        rec, s_new, attn_s = _hgrn(qr, kr, vr, lf, gate, zeros_state, 0, g_rec, CHUNK, HGRN_ROWS,
                                   seq // HGRN_ROWS, sample_attn=rider)
        attn = _prompt_attention(qpm, kpm, vpm, band_bias, g_att, n_p, seq)
        yp = _mlp(yp, rec, attn, w_out_b, w_up_b, w_dn_b, l, *post, tm)
        rec_p.append(s_new)
        k_p.append(kt)
        v_p.append(vt)

        rec, s_new = _hgrn(qr_s, kr_s, vr_s, lf_s, gate_s, state_in, l, g_rec, t_new, CHUNK, 1)
        attn_pm = attn_s.reshape(n_s * t_new, d_attn // LANES, LANES).transpose(1, 0, 2)
        ys = _mlp(ys, rec, attn_pm, w_out_b, w_up_b, w_dn_b, l, *post, tm_s)
        rec_s.append(s_new)
        k_s.append(knat.reshape(n_s, t_new, n_heads, dh))
        v_s.append(vnat.reshape(n_s, t_new, n_heads, dh))

    to_cache = lambda xs: jnp.stack(xs).reshape(depth, n_p, n_heads, dh, keep).transpose(0, 1, 4, 2, 3)
    return (yp.reshape(n_p, seq, dm), ys.reshape(n_s, t_new, dm), jnp.stack(rec_p), to_cache(k_p),
            to_cache(v_p), jnp.stack(rec_s), jnp.stack(k_s), jnp.stack(v_s))
```

```python
import functools
import math

import numpy as np
import jax
import jax.numpy as jnp
from jax import lax
from jax.experimental import pallas as pl
from jax.experimental.pallas import tpu as pltpu

F32 = jnp.float32
BF16 = jnp.bfloat16

ATTN_HEAD_DIM = 64
REC_EXPAND = 128
REC_DV = 128
DILATED_PATTERNS = ((128, 1), (512, 4), (2048, 16))
MAX_WINDOW = max(w for w, _ in DILATED_PATTERNS)
BAND_BLOCK = 128
NUM_BUCKETS = 32
MAX_DISTANCE = MAX_WINDOW
RMS_EPS = 1e-6
NEG_INF = -1e30
MIN_FORGET = 1e-30

LANES = 128
CHUNK = 128
HGRN_ROWS = 512
VMEM_LIMIT = 56 * 1024 * 1024


def _dot(a, b):
    return jnp.dot(a, b, preferred_element_type=F32)


def _dot_nt(a, b):
    return lax.dot_general(a, b, (((1,), (1,)), ((), ())), preferred_element_type=F32)


def _split2(x):
    hi = x.astype(BF16)
    lo = (x - hi.astype(F32)).astype(BF16)
    return jnp.concatenate([hi, lo], axis=1)


def _sum2(y):
    n = y.shape[1] // 2
    return y[:, :n] + y[:, n:]


def _rms(x, g):
    ms = jnp.mean(x * x, axis=-1, keepdims=True)
    return x * lax.rsqrt(ms + RMS_EPS) * g


def _rel_bucket_np(dist):
    exact = NUM_BUCKETS // 2
    d = np.maximum(dist, exact).astype(np.float64)
    large = exact + (np.log(d / exact) / math.log(MAX_DISTANCE / exact) * (NUM_BUCKETS - exact)).astype(np.int64)
    large = np.minimum(large, NUM_BUCKETS - 1)
    return np.where(dist < exact, dist, large).astype(np.int32)


def _band_tables():
    qi = np.arange(BAND_BLOCK)[:, None]
    ki = np.arange(2 * BAND_BLOCK)[None, :]
    idx, add = [], []
    for w, d in DILATED_PATTERNS:
        taps = w // d
        tap = qi + BAND_BLOCK - ki
        valid = (tap >= 0) & (tap <= taps)
        bucket = _rel_bucket_np(np.maximum(tap, 0) * d)
        for first in (False, True):
            v = valid & (ki >= BAND_BLOCK) if first else valid
            idx.append(bucket)
            add.append(np.where(v, 0.0, NEG_INF))
    return np.stack(idx).astype(np.int32), np.stack(add).astype(np.float32)


def _sample_tables(n_past, t_new):
    t = np.arange(t_new)[:, None]
    col = np.arange(n_past + LANES)[None, :]
    dist = np.where(col < n_past, n_past + t - col, t - (col - n_past))
    ok = (dist >= 0) & ((col < n_past) | (col - n_past < t_new))
    cnt = np.zeros(dist.shape, np.int64)
    for w, d in DILATED_PATTERNS:
        cnt += (ok & (dist % d == 0) & (dist <= w)).astype(np.int64)
    idx = _rel_bucket_np(np.maximum(dist, 0))
    add = np.where(cnt > 0, np.log(np.maximum(cnt, 1)), NEG_INF)
    return idx[None].astype(np.int32), add[None].astype(np.float32)


def _levels(seg):
    out, h = [], seg // 2
    while h >= 1:
        out.append(h)
        h //= 2
    return out


def _hgrn_tables(seg):
    r = np.arange(CHUNK)
    same_seg = (r[:, None] // seg) == (r[None, :] // seg)
    tri = (same_seg & (r[None, :] <= r[:, None])).astype(np.float32)
    eye = np.eye(CHUNK, dtype=np.float32)
    blocks = []
    lv = np.full((CHUNK, CHUNK), -1, np.int32)
    levels = _levels(seg)
    for i, h in enumerate(levels):
        ref = r - (r % (2 * h)) + h - 1
        g = np.zeros((CHUNK, CHUNK), np.float32)
        g[r, ref] = 1.0
        second = (r % (2 * h)) >= h
        sign = np.where(second, 1.0, -1.0)[:, None].astype(np.float32)
        blocks.append((sign * (eye - g)) @ tri)
        pair = ((r[:, None] // (2 * h)) == (r[None, :] // (2 * h))) & second[:, None] & (~second)[None, :]
        lv[pair] = i
    lv[r, r] = len(levels)
    seg_end = r - (r % seg) + seg - 1
    gl = np.zeros((CHUNK, CHUNK), np.float32)
    gl[r, seg_end] = 1.0
    blocks.append(tri)
    blocks.append((gl - eye) @ tri)
    mat = np.concatenate(blocks, axis=0)
    assert np.all(np.isin(mat, (-1.0, 0.0, 1.0)))
    return mat, lv


def _bias_kernel(rb_ref, idx_ref, add_ref, out_ref):
    idx = idx_ref[0]
    for h in range(out_ref.shape[1]):
        acc = add_ref[0]
        for b in range(NUM_BUCKETS):
            acc = acc + jnp.where(idx == b, rb_ref[b, h], 0.0)
        out_ref[0, h] = acc


def _expand_bias(rel_bias, idx, add):
    n, r, c = idx.shape
    n_heads = rel_bias.shape[1]
    return pl.pallas_call(
        _bias_kernel,
        grid=(n,),
        in_specs=[
            pl.BlockSpec(memory_space=pltpu.SMEM),
            pl.BlockSpec((1, r, c), lambda i: (i, 0, 0)),
            pl.BlockSpec((1, r, c), lambda i: (i, 0, 0)),
        ],
        out_specs=pl.BlockSpec((1, n_heads, r, c), lambda i: (i, 0, 0, 0)),
        out_shape=jax.ShapeDtypeStruct((n, n_heads, r, c), F32),
        name="bias_expand",
    )(rel_bias.astype(F32), jnp.asarray(idx), jnp.asarray(add))


N_INPROJ_IN = 4


def _inproj_kernel(*refs, layer, d_rec, d_attn, cache_t):
    x_ref, g_ref, w_ref, lbraw_ref = refs[:N_INPROJ_IN]
    (qr_ref, kr_ref, vr_ref, lf_ref, gate_ref, qpm_ref, kpm_ref, vpm_ref, kout_ref, vout_ref) = refs[-10:]
    h = _rms(x_ref[...], g_ref[...]).astype(BF16)

    raw = lbraw_ref[...]
    e = jnp.exp(raw - jnp.max(raw, axis=0, keepdims=True))
    soft = e / jnp.sum(e, axis=0, keepdims=True)
    cum = soft[0:1]
    for i in range(1, layer + 1):
        cum = cum + soft[i:i + 1]
    lb = cum - soft[0:1]

    def proj(c0, width):
        return _dot(h, w_ref[0, :, c0:c0 + width])

    p = proj(0, d_rec)
    qr_ref[...] = p * jax.nn.sigmoid(p)
    z = proj(d_rec, d_rec)
    f = lb + (1.0 - lb) * jax.nn.sigmoid(z)
    lf_ref[...] = jnp.log(jnp.maximum(f, MIN_FORGET))
    kr_ref[...] = (1.0 - lb) * jax.nn.sigmoid(-z)
    vr_ref[...] = proj(2 * d_rec, d_rec).astype(vr_ref.dtype)
    p = proj(3 * d_rec, d_rec)
    gate_ref[...] = (p * jax.nn.sigmoid(p)).astype(gate_ref.dtype)

    c0 = 4 * d_rec
    qa = proj(c0, d_attn) * (ATTN_HEAD_DIM ** -0.5)
    ka = proj(c0 + d_attn, d_attn)
    va = proj(c0 + 2 * d_attn, d_attn)
    for pr in range(d_attn // LANES):
        sl = slice(pr * LANES, (pr + 1) * LANES)
        qpm_ref[pr] = qa[:, sl]
        kpm_ref[pr] = ka[:, sl]
        vpm_ref[pr] = va[:, sl]
    if cache_t is None:
        kout_ref[...] = ka
        vout_ref[...] = va
    else:
        blocks_per_seq, first_kept = cache_t

        @pl.when(pl.program_id(0) % blocks_per_seq >= first_kept)
        def _():
            kout_ref[0, 0] = ka.T
            vout_ref[0, 0] = va.T


def _layer_slab(a, layer):
    return pl.BlockSpec((1,) + a.shape[1:], lambda i: (layer, 0, 0), pipeline_mode=pl.Buffered(1))


def _inproj(x2d, g, w_bf, lb_raw, layer, tm, seq_keep=None, cache_bufs=None):
    t, dm = x2d.shape
    depth = w_bf.shape[0]
    d_rec = lb_raw.shape[1]
    d_attn = (w_bf.shape[2] - 4 * d_rec) // 3
    n_pair = d_attn // LANES
    nat = lambda wd: pl.BlockSpec((tm, wd), lambda i: (i, 0))
    pm = pl.BlockSpec((n_pair, tm, LANES), lambda i: (0, i, 0))
    full = lambda a: pl.BlockSpec(a.shape, lambda i: (0,) * a.ndim)
    f32 = lambda *s: jax.ShapeDtypeStruct(s, F32)
    operands = [x2d, g, w_bf, lb_raw]
    in_specs = [nat(dm), full(g), _layer_slab(w_bf, layer), full(lb_raw)]
    aliases = {}
    if seq_keep is None:
        cache_t, kv_spec, kv_shape = None, nat(d_attn), f32(t, d_attn)
    else:
        seq, keep = seq_keep
        assert seq % tm == 0 and keep % tm == 0
        bps, first = seq // tm, (seq - keep) // tm
        cache_t = (bps, first)
        kv_spec = pl.BlockSpec((1, 1, d_attn, tm),
                               lambda i: (layer, i // bps, 0, jnp.maximum(i % bps - first, 0)))
        kv_shape = f32(depth, t // seq, d_attn, keep)
        assert all(b.shape == kv_shape.shape for b in cache_bufs)
        operands += list(cache_bufs)
        in_specs += [pl.BlockSpec(memory_space=pl.ANY)] * 2
        aliases = {N_INPROJ_IN: 8, N_INPROJ_IN + 1: 9}
    return pl.pallas_call(
        functools.partial(_inproj_kernel, layer=layer, d_rec=d_rec, d_attn=d_attn, cache_t=cache_t),
        grid=(t // tm,),
        in_specs=in_specs,
        out_specs=[nat(d_rec)] * 5 + [pm] * 3 + [kv_spec] * 2,
        out_shape=[f32(t, d_rec), f32(t, d_rec), jax.ShapeDtypeStruct((t, d_rec), BF16), f32(t, d_rec),
                   jax.ShapeDtypeStruct((t, d_rec), BF16)] + [f32(n_pair, t, LANES)] * 3 + [kv_shape] * 2,
        input_output_aliases=aliases,
        compiler_params=pltpu.CompilerParams(dimension_semantics=("arbitrary",), vmem_limit_bytes=VMEM_LIMIT),
        name="inproj",
    )(*operands)


N_HGRN_IN = 9
N_ATTN_IN = 8


def _hgrn_kernel(*refs, seg, n_chunks, n_attn_heads):
    q_ref, k_ref, v_ref, lf_ref, gate_ref, s0_ref, gn_ref, mat_ref, lv_ref = refs[:N_HGRN_IN]
    st_ref = refs[-1]
    if n_attn_heads:
        o_ref, sout_ref, attn_out_ref = refs[-4:-1]
        rider = _sample_attn_pieces(*refs[N_HGRN_IN:N_HGRN_IN + N_ATTN_IN], attn_out_ref, n_heads=n_attn_heads)
    else:
        o_ref, sout_ref = refs[-3:-1]
        rider = iter(())
    next(rider, None)
    nseg = CHUNK // seg
    seg_shift = seg.bit_length() - 1
    levels = _levels(seg)
    nl = len(levels)
    ti = pl.program_id(2)

    @pl.when(ti == 0)
    def _():
        st_ref[...] = s0_ref[0, :, 0]

    rowi = lax.broadcasted_iota(jnp.int32, (CHUNK, CHUNK), 0)
    coli = lax.broadcasted_iota(jnp.int32, (CHUNK, CHUNK), 1)
    mat = mat_ref[...]
    lv = lv_ref[...]
    gn = gn_ref[...]

    chunk_rows = [slice(c * CHUNK, (c + 1) * CHUNK) for c in range(n_chunks)]
    lf_split = [_split2(lf_ref[rows, :]) for rows in chunk_rows]
    e_lv = _dot(mat[:nl * CHUNK], jnp.concatenate([s[:, :LANES] for s in lf_split], axis=1))
    e_bl = _dot(mat[nl * CHUNK:], jnp.concatenate(lf_split, axis=1))

    intra = []
    for c, rows in enumerate(chunk_rows):
        q = q_ref[rows, :]
        k = k_ref[rows, :]
        a = jnp.zeros((CHUNK, CHUNK), F32)
        for i, h in enumerate(levels):
            second = (rowi & (2 * h - 1)) >= h
            e = e_lv[i * CHUNK:(i + 1) * CHUNK, c * LANES:(c + 1) * LANES]
            x = (jnp.where(second, q, k) * jnp.exp(e)).astype(BF16)
            a = jnp.where(lv == i, _dot_nt(x, x), a)
        a = jnp.where(lv == nl, _dot_nt(q.astype(BF16), k.astype(BF16)), a)
        intra.append(a.astype(BF16))

    states = [st_ref[j] for j in range(nseg)]
    for c, rows in enumerate(chunk_rows):
        q = q_ref[rows, :]
        k = k_ref[rows, :]
        vb = v_ref[rows, :].astype(BF16)
        e = _sum2(e_bl[:, 2 * c * LANES:2 * (c + 1) * LANES])
        eb = jnp.exp(e[:CHUNK])
        qe = (q * eb).astype(BF16)
        kd_t = (k * jnp.exp(e[CHUNK:])).T
        dec_t = eb.T
        o = _dot(intra[c], vb)
        for j in range(nseg):
            s_old = states[j]
            oj = _dot(qe, s_old.astype(BF16))
            kd_j = kd_t
            if nseg > 1:
                oj = jnp.where((rowi >> seg_shift) == j, oj, 0.0)
                kd_j = jnp.where((coli >> seg_shift) == j, kd_t, 0.0)
            o = o + oj
            col = j * seg + seg - 1
            states[j] = s_old * dec_t[:, col:col + 1] + _dot(kd_j.astype(BF16), vb)
        o_ref[rows, :] = (_rms(o, gn) * gate_ref[rows, :]).astype(o_ref.dtype)
        next(rider, None)

    for j in range(nseg):
        st_ref[j] = states[j]
    for _ in rider:
        pass

    @pl.when(ti == pl.num_programs(2) - 1)
    def _():
        sout_ref[0, :, 0] = st_ref[...]


def _hgrn(qr, kr, vr, lf, gate, s0_all, layer, g_norm, seg, rows_per_step, steps_per_seq,
          state_buf, out_layer, sample_attn=None):
    t, d_rec = qr.shape
    n_heads = d_rec // LANES
    nseg = CHUNK // seg
    n_seq = s0_all.shape[1]
    mat, lv = _hgrn_tables(seg)
    grid = (n_seq // nseg, n_heads, steps_per_seq)
    tok = pl.BlockSpec((rows_per_step, LANES), lambda bi, h, ti: (bi * steps_per_seq + ti, h))
    st_in = pl.BlockSpec((1, nseg, 1, REC_EXPAND, REC_DV), lambda bi, h, ti: (layer, bi, h, 0, 0))
    st_out = pl.BlockSpec((1, nseg, 1, REC_EXPAND, REC_DV), lambda bi, h, ti: (out_layer, bi, h, 0, 0))
    full = lambda a: pl.BlockSpec(a.shape, lambda bi, h, ti: (0,) * a.ndim)
    mat_j, lv_j = jnp.asarray(mat, BF16), jnp.asarray(lv)
    operands = [qr, kr, vr, lf, gate, s0_all, g_norm, mat_j, lv_j]
    in_specs = [tok] * 5 + [st_in, pl.BlockSpec((1, LANES), lambda bi, h, ti: (0, h)), full(mat_j), full(lv_j)]
    out_specs = [tok, st_out]
    assert state_buf.shape[1:] == (n_seq, n_heads, REC_EXPAND, REC_DV)
    out_shape = [jax.ShapeDtypeStruct((t, d_rec), BF16), jax.ShapeDtypeStruct(state_buf.shape, F32)]
    n_attn_heads = 0
    if sample_attn is not None:
        q3, kn3, vn3, cache_kt, cache_vt, cache_layer, bias, g_attn = sample_attn
        n_seq_s, t_new, width = q3.shape
        n_past = cache_kt.shape[3]
        n_steps = grid[0] * grid[1] * grid[2]
        assert n_seq_s % n_steps == 0
        per_step = n_seq_s // n_steps
        step = lambda bi, h, ti: (bi * n_heads + h) * steps_per_seq + ti
        tok_s = pl.BlockSpec((per_step, t_new, width), lambda bi, h, ti: (step(bi, h, ti), 0, 0))
        cache = pl.BlockSpec((1, per_step, width, n_past), lambda bi, h, ti: (cache_layer, step(bi, h, ti), 0, 0))
        bd = _head_mean_matrix(width)
        operands += [q3, kn3, vn3, cache_kt, cache_vt, bias, g_attn, bd]
        in_specs += [tok_s, tok_s, tok_s, cache, cache, full(bias), full(g_attn), full(bd)]
        out_specs.append(tok_s)
        out_shape.append(jax.ShapeDtypeStruct(q3.shape, F32))
        n_attn_heads = width // ATTN_HEAD_DIM
    aliases = {len(operands): 1}
    operands.append(state_buf)
    in_specs.append(pl.BlockSpec(memory_space=pl.ANY))
    return pl.pallas_call(
        functools.partial(_hgrn_kernel, seg=seg, n_chunks=rows_per_step // CHUNK, n_attn_heads=n_attn_heads),
        grid=grid,
        in_specs=in_specs,
        out_specs=out_specs,
        out_shape=out_shape,
        input_output_aliases=aliases,
        scratch_shapes=[pltpu.VMEM((nseg, REC_EXPAND, REC_DV), F32)],
        compiler_params=pltpu.CompilerParams(
            dimension_semantics=("parallel", "parallel", "arbitrary"), vmem_limit_bytes=VMEM_LIMIT),
        name="hgrn",
    )(*operands)


ATTN_UNROLL = 4
MERGE_UNROLL = 4


def _prompt_attn_kernel(q_ref, k_ref, v_ref, bias_ref, g_ref, bd_ref, out_ref,
                        oacc_ref, macc_ref, lacc_ref, p_ref, *, seq):
    lane = lax.broadcasted_iota(jnp.int32, (1, LANES), 1)
    lo_half = lane < ATTN_HEAD_DIM
    keeps = (lo_half, jnp.logical_not(lo_half))
    one = jnp.ones((), BF16)
    bb = BAND_BLOCK

    def runs(gi, d, nb):
        run = min(ATTN_UNROLL, nb)
        runs_per_res = nb // run
        for w in range(ATTN_UNROLL // run):
            n = gi * (ATTN_UNROLL // run) + w
            r = n // runs_per_res
            i0 = (n % runs_per_res) * run
            base = r + i0 * (d * bb)
            psl = pl.ds(jnp.maximum(base - d * bb, r), bb, stride=d)
            yield base, i0, psl, run, w * run

    def stage_scores(pi, d, nb, gi, slot):
        for base, i0, psl, run, u0 in runs(gi, d, nb):
            k_prev = k_ref[0, psl, :].astype(BF16)
            for j in range(run):
                qsl = pl.ds(base + j * (d * bb), bb, stride=d)
                k_cur = k_ref[0, qsl, :].astype(BF16)
                k2 = jnp.concatenate([k_prev, k_cur], axis=0)
                q2 = q_ref[0, qsl, :]
                variant = jnp.where(i0 == 0, 1, 0) if j == 0 else 0
                ms = []
                for hh in range(2):
                    qm = jnp.where(keeps[hh], q2, 0.0).astype(BF16)
                    s = _dot_nt(qm, k2) + bias_ref[pi, variant, hh]
                    m = jnp.max(s, axis=-1, keepdims=True)
                    p_ref[slot, 2 * (u0 + j) + hh] = jnp.exp(s - m).astype(BF16)
                    ms.append(m)
                macc_ref[pi, qsl, :] = jnp.where(lo_half, ms[0], ms[1])
                k_prev = k_cur

    def stage_values(pi, d, nb, gi, slot):
        for base, i0, psl, run, u0 in runs(gi, d, nb):
            v_prev = v_ref[0, psl, :].astype(BF16)
            for j in range(run):
                qsl = pl.ds(base + j * (d * bb), bb, stride=d)
                v_cur = v_ref[0, qsl, :].astype(BF16)
                v2 = jnp.concatenate([v_prev, v_cur], axis=0)
                pvs = [_dot(p_ref[slot, 2 * (u0 + j) + hh], jnp.where(keeps[hh], v2, one)) for hh in range(2)]
                oacc_ref[pi, qsl, :] = jnp.where(lo_half, pvs[0], pvs[1])
                lacc_ref[pi, qsl, :] = pltpu.roll(jnp.where(lo_half, pvs[1], pvs[0]), ATTN_HEAD_DIM, axis=1)
                v_prev = v_cur

    for pi, (_, d) in enumerate(DILATED_PATTERNS):
        nb = seq // (d * bb)
        n_groups = (d * nb) // ATTN_UNROLL
        stage_scores(pi, d, nb, 0, 0)

        def overlapped(k, carry, pi=pi, d=d, nb=nb):
            stage_scores(pi, d, nb, 2 * k + 1, 1)
            stage_values(pi, d, nb, 2 * k, 0)
            stage_scores(pi, d, nb, 2 * k + 2, 0)
            stage_values(pi, d, nb, 2 * k + 1, 1)
            return carry

        lax.fori_loop(0, n_groups // 2 - 1, overlapped, 0)
        stage_scores(pi, d, nb, n_groups - 1, 1)
        stage_values(pi, d, nb, n_groups - 2, 0)
        stage_values(pi, d, nb, n_groups - 1, 1)

    g = g_ref[...]
    bd = bd_ref[...]

    def merge(gi, carry):
        for u in range(MERGE_UNROLL):
            rows = pl.ds(pl.multiple_of((gi * MERGE_UNROLL + u) * bb, bb), bb)
            ms = [macc_ref[pi, rows, :] for pi in range(len(DILATED_PATTERNS))]
            mx = functools.reduce(jnp.maximum, ms)
            ws = [jnp.exp(m - mx) for m in ms]
            add = lambda a, b: a + b
            den = functools.reduce(add, [w * lacc_ref[pi, rows, :] for pi, w in enumerate(ws)])
            num = functools.reduce(add, [w * oacc_ref[pi, rows, :] for pi, w in enumerate(ws)])
            o = num / den
            ms = _dot((o * o).astype(BF16), bd)
            out_ref[0, rows, :] = (o * lax.rsqrt(ms + RMS_EPS) * g).astype(out_ref.dtype)
        return carry

    lax.fori_loop(0, seq // (bb * MERGE_UNROLL), merge, 0)


def _head_mean_matrix(width):
    r = np.arange(width)
    return jnp.asarray(((r[:, None] // ATTN_HEAD_DIM) == (r[None, :] // ATTN_HEAD_DIM)) / ATTN_HEAD_DIM, BF16)


def _prompt_attention(qpm, kpm, vpm, band_bias, g_attn, n_seq, seq):
    n_pair = qpm.shape[0]
    n_pat = len(DILATED_PATTERNS)
    for _, d in DILATED_PATTERNS:
        assert seq % (d * BAND_BLOCK) == 0 and (seq // BAND_BLOCK) % (2 * ATTN_UNROLL) == 0
        nb = seq // (d * BAND_BLOCK)
        assert nb % min(ATTN_UNROLL, nb) == 0 and ATTN_UNROLL % min(ATTN_UNROLL, nb) == 0
    assert (seq // BAND_BLOCK) % MERGE_UNROLL == 0
    tok = pl.BlockSpec((1, seq, LANES), lambda p, b: (p, b, 0))
    bd = _head_mean_matrix(LANES)
    return pl.pallas_call(
        functools.partial(_prompt_attn_kernel, seq=seq),
        grid=(n_pair, n_seq),
        in_specs=[tok, tok, tok,
                  pl.BlockSpec((n_pat, 2, 2, BAND_BLOCK, 2 * BAND_BLOCK), lambda p, b: (0, 0, p, 0, 0)),
                  pl.BlockSpec((1, LANES), lambda p, b: (0, p)),
                  pl.BlockSpec(bd.shape, lambda p, b: (0, 0))],
        out_specs=tok,
        out_shape=jax.ShapeDtypeStruct(qpm.shape, BF16),
        scratch_shapes=[pltpu.VMEM((n_pat, seq, LANES), F32)] * 3
        + [pltpu.VMEM((2, 2 * ATTN_UNROLL, BAND_BLOCK, 2 * BAND_BLOCK), BF16)],
        compiler_params=pltpu.CompilerParams(
            dimension_semantics=("parallel", "parallel"), vmem_limit_bytes=VMEM_LIMIT),
        name="prompt_attn",
    )(qpm, kpm, vpm, band_bias, g_attn, bd)


def _sample_attn_pieces(q_ref, kn_ref, vn_ref, ckt_ref, cvt_ref, bias_ref, g_ref, bd_ref, o_ref, *, n_heads):
    n_seq, t_new, width = q_ref.shape
    n_past = ckt_ref.shape[3]
    rows = n_heads * t_new
    own = (lax.broadcasted_iota(jnp.int32, (rows, width), 0) // t_new
           == lax.broadcasted_iota(jnp.int32, (rows, width), 1) // ATTN_HEAD_DIM)
    pad = jnp.zeros((LANES - t_new, width), F32)
    for s in range(n_seq):
        qbd = jnp.where(own, jnp.concatenate([q_ref[s]] * n_heads, axis=0), 0.0).astype(BF16)
        kn = jnp.concatenate([kn_ref[s], pad], axis=0).astype(BF16)
        vn = jnp.concatenate([vn_ref[s], pad], axis=0).astype(BF16)
        s_c = _dot(qbd, ckt_ref[0, s].astype(BF16)) + bias_ref[:, :n_past]
        s_n = _dot_nt(qbd, kn) + bias_ref[:, n_past:]
        m = jnp.maximum(jnp.max(s_c, axis=-1, keepdims=True), jnp.max(s_n, axis=-1, keepdims=True))
        p_c = jnp.exp(s_c - m)
        p_n = jnp.exp(s_n - m)
        l = jnp.sum(p_c, axis=-1, keepdims=True) + jnp.sum(p_n, axis=-1, keepdims=True)
        p_cb = p_c.astype(BF16)
        yield
        parts = []
        for f0 in range(0, width, LANES):
            parts.append(_dot_nt(p_cb, cvt_ref[0, s, f0:f0 + LANES, :].astype(BF16)))
            if f0 + LANES < width:
                yield
        acc = jnp.concatenate(parts, axis=1) + _dot(p_n.astype(BF16), vn)
        o = jnp.where(own, acc * (1.0 / l), 0.0)
        ms = _dot((o * o).astype(BF16), bd_ref[...])
        o = o * lax.rsqrt(ms + RMS_EPS) * g_ref[...]
        out = o[0:t_new]
        for h in range(1, n_heads):
            out = out + o[h * t_new:(h + 1) * t_new]
        o_ref[s] = out


def _mlp_kernel(x_ref, rec_ref, attn_ref, wout_ref, wup_ref, wdn_ref, gpost_ref, gpre_ref, gmpost_ref,
                y_ref, *, ff_chunk):
    d_rec = rec_ref.shape[1]
    mixed = _dot(rec_ref[...].astype(BF16), wout_ref[0, 0:d_rec, :])
    for p in range(attn_ref.shape[0]):
        r0 = d_rec + p * LANES
        mixed = mixed + _dot(attn_ref[p].astype(BF16), wout_ref[0, r0:r0 + LANES, :])
    x1 = x_ref[...] + _rms(mixed, gpost_ref[...])
    h = _rms(x1, gpre_ref[...]).astype(BF16)
    acc = jnp.zeros(x1.shape, F32)
    for c0 in range(0, wup_ref.shape[2], ff_chunk):
        u = jnp.square(jnp.maximum(_dot(h, wup_ref[0, :, c0:c0 + ff_chunk]), 0.0))
        acc = acc + _dot(u.astype(BF16), wdn_ref[0, c0:c0 + ff_chunk, :])
    y_ref[...] = x1 + _rms(acc, gmpost_ref[...])


def _mlp(x2d, rec, attn_pm, wout, wup, wdn, layer, g_post, g_pre, g_mpost, tm):
    t, dm = x2d.shape
    n_pair = attn_pm.shape[0]
    row = lambda wd: pl.BlockSpec((tm, wd), lambda i: (i, 0))
    once = lambda a: pl.BlockSpec(a.shape, lambda i: (0,) * a.ndim, pipeline_mode=pl.Buffered(1))
    return pl.pallas_call(
        functools.partial(_mlp_kernel, ff_chunk=1024),
        grid=(t // tm,),
        in_specs=[row(dm), row(rec.shape[1]), pl.BlockSpec((n_pair, tm, LANES), lambda i: (0, i, 0)),
                  _layer_slab(wout, layer), _layer_slab(wup, layer), _layer_slab(wdn, layer),
                  once(g_post), once(g_pre), once(g_mpost)],
        out_specs=row(dm),
        out_shape=jax.ShapeDtypeStruct((t, dm), F32),
        compiler_params=pltpu.CompilerParams(dimension_semantics=("parallel",), vmem_limit_bytes=VMEM_LIMIT),
        name="mlp",
    )(x2d, rec, attn_pm, wout, wup, wdn, g_post, g_pre, g_mpost)


def kernel(x_prompt, x_sample, state_hgrn, cache_k, cache_v, rel_bias, lb_raw, w_in, w_out, w_up, w_down,
           g_mix_pre, g_mix_post, g_mlp_pre, g_mlp_post, g_rec_out, g_attn_out):
    n_p, seq, dm = x_prompt.shape
    n_s, t_new, _ = x_sample.shape
    depth = w_in.shape[0]
    n_past = cache_k.shape[2]
    n_heads, dh = cache_k.shape[3], cache_k.shape[4]
    d_attn = n_heads * dh
    d_rec = lb_raw.shape[1]
    n_rec_heads = d_rec // REC_EXPAND
    keep = min(MAX_WINDOW, seq)
    tm = 512
    tm_s = min(tm, n_s * t_new)
    assert dh == ATTN_HEAD_DIM and CHUNK % t_new == 0 and seq % tm == 0
    assert (n_s * t_new) % CHUNK == 0 and (n_s * t_new) % tm_s == 0

    band_idx, band_add = _band_tables()
    band_bias = _expand_bias(rel_bias, band_idx, band_add)
    band_bias = band_bias.reshape(len(DILATED_PATTERNS), 2, n_heads, BAND_BLOCK, 2 * BAND_BLOCK)
    s_idx, s_add = _sample_tables(n_past, t_new)
    sample_bias = _expand_bias(rel_bias, s_idx, s_add).reshape(n_heads * t_new, n_past + LANES)

    w_in_b, w_out_b = w_in.astype(BF16), w_out.astype(BF16)
    w_up_b, w_dn_b = w_up.astype(BF16), w_down.astype(BF16)
    row = lambda a, l: a[l][None, :]

    cache_kt = cache_k.transpose(0, 1, 3, 4, 2).reshape(depth, n_s, d_attn, n_past).astype(F32)
    cache_vt = cache_v.transpose(0, 1, 3, 4, 2).reshape(depth, n_s, d_attn, n_past).astype(F32)

    yp = x_prompt.reshape(n_p * seq, dm)
    ys = x_sample.reshape(n_s * t_new, dm)
    zeros_state = jnp.zeros((1, n_p, n_rec_heads, REC_EXPAND, REC_DV), F32)
    state_in = state_hgrn.astype(F32)
    cache_p = (jnp.zeros((depth, n_p, d_attn, keep), F32), jnp.zeros((depth, n_p, d_attn, keep), F32))
    rec_p = jnp.zeros((depth, n_p, n_rec_heads, REC_EXPAND, REC_DV), F32)
    rec_s = jnp.zeros((depth, n_s, n_rec_heads, REC_EXPAND, REC_DV), F32)
    k_s, v_s = [], []
    for l in range(depth):
        g_pre, g_rec, g_att = row(g_mix_pre, l), row(g_rec_out, l), row(g_attn_out, l)
        post = (row(g_mix_post, l), row(g_mlp_pre, l), row(g_mlp_post, l))

        qr, kr, vr, lf, gate, qpm, kpm, vpm, kt, vt = _inproj(
            yp, g_pre, w_in_b, lb_raw, l, tm, seq_keep=(seq, keep), cache_bufs=cache_p)
        cache_p = (kt, vt)
        qr_s, kr_s, vr_s, lf_s, gate_s, qpm_s, _, _, knat, vnat = _inproj(ys, g_pre, w_in_b, lb_raw, l, tm_s)
        q3 = qpm_s.transpose(1, 0, 2).reshape(n_s, t_new, d_attn)
        rider = (q3, knat.reshape(n_s, t_new, d_attn), vnat.reshape(n_s, t_new, d_attn),
                 cache_kt, cache_vt, l, sample_bias, g_att)

        rec, rec_p, attn_s = _hgrn(qr, kr, vr, lf, gate, zeros_state, 0, g_rec, CHUNK, HGRN_ROWS,
                                   seq // HGRN_ROWS, rec_p, l, sample_attn=rider)
        attn = _prompt_attention(qpm, kpm, vpm, band_bias, g_att, n_p, seq)
        yp = _mlp(yp, rec, attn, w_out_b, w_up_b, w_dn_b, l, *post, tm)

        rec, rec_s = _hgrn(qr_s, kr_s, vr_s, lf_s, gate_s, state_in, l, g_rec, t_new, CHUNK, 1, rec_s, l)
        attn_pm = attn_s.reshape(n_s * t_new, d_attn // LANES, LANES).transpose(1, 0, 2)
        ys = _mlp(ys, rec, attn_pm, w_out_b, w_up_b, w_dn_b, l, *post, tm_s)
        k_s.append(knat.reshape(n_s, t_new, n_heads, dh))
        v_s.append(vnat.reshape(n_s, t_new, n_heads, dh))

    to_cache = lambda a: a.reshape(depth, n_p, n_heads, dh, keep).transpose(0, 1, 4, 2, 3)
    return (yp.reshape(n_p, seq, dm), ys.reshape(n_s, t_new, dm), rec_p, to_cache(cache_p[0]),
            to_cache(cache_p[1]), rec_s, jnp.stack(k_s), jnp.stack(v_s))
```

```python
import functools
import math

import numpy as np
import jax
import jax.numpy as jnp
from jax import lax
from jax.experimental import pallas as pl
from jax.experimental.pallas import tpu as pltpu

F32 = jnp.float32
BF16 = jnp.bfloat16

ATTN_HEAD_DIM = 64
REC_EXPAND = 128
REC_DV = 128
DILATED_PATTERNS = ((128, 1), (512, 4), (2048, 16))
MAX_WINDOW = max(w for w, _ in DILATED_PATTERNS)
BAND_BLOCK = 128
NUM_BUCKETS = 32
MAX_DISTANCE = MAX_WINDOW
RMS_EPS = 1e-6
NEG_INF = -1e30
MIN_FORGET = 1e-30

LANES = 128
CHUNK = 128
HGRN_ROWS = 512
VMEM_LIMIT = 56 * 1024 * 1024


def _dot(a, b):
    return jnp.dot(a, b, preferred_element_type=F32)


def _dot_nt(a, b):
    return lax.dot_general(a, b, (((1,), (1,)), ((), ())), preferred_element_type=F32)


def _split2(x):
    hi = x.astype(BF16)
    lo = (x - hi.astype(F32)).astype(BF16)
    return jnp.concatenate([hi, lo], axis=1)


def _sum2(y):
    n = y.shape[1] // 2
    return y[:, :n] + y[:, n:]


def _rms(x, g):
    ms = jnp.mean(x * x, axis=-1, keepdims=True)
    return x * lax.rsqrt(ms + RMS_EPS) * g


def _rel_bucket_np(dist):
    exact = NUM_BUCKETS // 2
    d = np.maximum(dist, exact).astype(np.float64)
    large = exact + (np.log(d / exact) / math.log(MAX_DISTANCE / exact) * (NUM_BUCKETS - exact)).astype(np.int64)
    large = np.minimum(large, NUM_BUCKETS - 1)
    return np.where(dist < exact, dist, large).astype(np.int32)


def _band_tables():
    qi = np.arange(BAND_BLOCK)[:, None]
    ki = np.arange(2 * BAND_BLOCK)[None, :]
    idx, add = [], []
    for w, d in DILATED_PATTERNS:
        taps = w // d
        tap = qi + BAND_BLOCK - ki
        valid = (tap >= 0) & (tap <= taps)
        bucket = _rel_bucket_np(np.maximum(tap, 0) * d)
        for first in (False, True):
            v = valid & (ki >= BAND_BLOCK) if first else valid
            idx.append(bucket)
            add.append(np.where(v, 0.0, NEG_INF))
    return np.stack(idx).astype(np.int32), np.stack(add).astype(np.float32)


def _sample_tables(n_past, t_new):
    t = np.arange(t_new)[:, None]
    col = np.arange(n_past + LANES)[None, :]
    dist = np.where(col < n_past, n_past + t - col, t - (col - n_past))
    ok = (dist >= 0) & ((col < n_past) | (col - n_past < t_new))
    cnt = np.zeros(dist.shape, np.int64)
    for w, d in DILATED_PATTERNS:
        cnt += (ok & (dist % d == 0) & (dist <= w)).astype(np.int64)
    idx = _rel_bucket_np(np.maximum(dist, 0))
    add = np.where(cnt > 0, np.log(np.maximum(cnt, 1)), NEG_INF)
    return idx[None].astype(np.int32), add[None].astype(np.float32)


def _levels(seg):
    out, h = [], seg // 2
    while h >= 1:
        out.append(h)
        h //= 2
    return out


def _hgrn_tables(seg):
    r = np.arange(CHUNK)
    same_seg = (r[:, None] // seg) == (r[None, :] // seg)
    tri = (same_seg & (r[None, :] <= r[:, None])).astype(np.float32)
    eye = np.eye(CHUNK, dtype=np.float32)
    blocks = []
    lv = np.full((CHUNK, CHUNK), -1, np.int32)
    levels = _levels(seg)
    for i, h in enumerate(levels):
        ref = r - (r % (2 * h)) + h - 1
        g = np.zeros((CHUNK, CHUNK), np.float32)
        g[r, ref] = 1.0
        second = (r % (2 * h)) >= h
        sign = np.where(second, 1.0, -1.0)[:, None].astype(np.float32)
        blocks.append((sign * (eye - g)) @ tri)
        pair = ((r[:, None] // (2 * h)) == (r[None, :] // (2 * h))) & second[:, None] & (~second)[None, :]
        lv[pair] = i
    lv[r, r] = len(levels)
    seg_end = r - (r % seg) + seg - 1
    gl = np.zeros((CHUNK, CHUNK), np.float32)
    gl[r, seg_end] = 1.0
    blocks.append(tri)
    blocks.append((gl - eye) @ tri)
    mat = np.concatenate(blocks, axis=0)
    assert np.all(np.isin(mat, (-1.0, 0.0, 1.0)))
    return mat, lv


def _bias_kernel(rb_ref, idx_ref, add_ref, out_ref):
    idx = idx_ref[0]
    for h in range(out_ref.shape[1]):
        acc = add_ref[0]
        for b in range(NUM_BUCKETS):
            acc = acc + jnp.where(idx == b, rb_ref[b, h], 0.0)
        out_ref[0, h] = acc


def _expand_bias(rel_bias, idx, add):
    n, r, c = idx.shape
    n_heads = rel_bias.shape[1]
    return pl.pallas_call(
        _bias_kernel,
        grid=(n,),
        in_specs=[
            pl.BlockSpec(memory_space=pltpu.SMEM),
            pl.BlockSpec((1, r, c), lambda i: (i, 0, 0)),
            pl.BlockSpec((1, r, c), lambda i: (i, 0, 0)),
        ],
        out_specs=pl.BlockSpec((1, n_heads, r, c), lambda i: (i, 0, 0, 0)),
        out_shape=jax.ShapeDtypeStruct((n, n_heads, r, c), F32),
        name="bias_expand",
    )(rel_bias.astype(F32), jnp.asarray(idx), jnp.asarray(add))


N_INPROJ_IN = 4


def _inproj_kernel(*refs, layer, d_rec, d_attn, cache_t):
    x_ref, g_ref, w_ref, lbraw_ref = refs[:N_INPROJ_IN]
    (qr_ref, kr_ref, vr_ref, lf_ref, gate_ref, qpm_ref, kpm_ref, vpm_ref, kout_ref, vout_ref) = refs[-10:]
    h = _rms(x_ref[...], g_ref[...]).astype(BF16)

    raw = lbraw_ref[...]
    e = jnp.exp(raw - jnp.max(raw, axis=0, keepdims=True))
    soft = e / jnp.sum(e, axis=0, keepdims=True)
    cum = soft[0:1]
    for i in range(1, layer + 1):
        cum = cum + soft[i:i + 1]
    lb = cum - soft[0:1]

    def proj(c0, width):
        return _dot(h, w_ref[0, :, c0:c0 + width])

    p = proj(0, d_rec)
    qr_ref[...] = p * jax.nn.sigmoid(p)
    z = proj(d_rec, d_rec)
    f = lb + (1.0 - lb) * jax.nn.sigmoid(z)
    lf_ref[...] = jnp.log(jnp.maximum(f, MIN_FORGET))
    kr_ref[...] = (1.0 - lb) * jax.nn.sigmoid(-z)
    vr_ref[...] = proj(2 * d_rec, d_rec).astype(vr_ref.dtype)
    p = proj(3 * d_rec, d_rec)
    gate_ref[...] = (p * jax.nn.sigmoid(p)).astype(gate_ref.dtype)

    c0 = 4 * d_rec
    qa = proj(c0, d_attn) * (ATTN_HEAD_DIM ** -0.5)
    ka = proj(c0 + d_attn, d_attn)
    va = proj(c0 + 2 * d_attn, d_attn)
    for pr in range(d_attn // LANES):
        sl = slice(pr * LANES, (pr + 1) * LANES)
        qpm_ref[pr] = qa[:, sl]
        kpm_ref[pr] = ka[:, sl]
        vpm_ref[pr] = va[:, sl]
    if cache_t is None:
        kout_ref[...] = ka
        vout_ref[...] = va
    else:
        blocks_per_seq, first_kept = cache_t

        @pl.when(pl.program_id(0) % blocks_per_seq >= first_kept)
        def _():
            kout_ref[0, 0] = ka.T
            vout_ref[0, 0] = va.T


def _layer_slab(a, layer):
    return pl.BlockSpec((1,) + a.shape[1:], lambda i: (layer, 0, 0), pipeline_mode=pl.Buffered(1))


def _inproj(x2d, g, w_bf, lb_raw, layer, tm, seq_keep=None, cache_bufs=None):
    t, dm = x2d.shape
    depth = w_bf.shape[0]
    d_rec = lb_raw.shape[1]
    d_attn = (w_bf.shape[2] - 4 * d_rec) // 3
    n_pair = d_attn // LANES
    nat = lambda wd: pl.BlockSpec((tm, wd), lambda i: (i, 0))
    pm = pl.BlockSpec((n_pair, tm, LANES), lambda i: (0, i, 0))
    full = lambda a: pl.BlockSpec(a.shape, lambda i: (0,) * a.ndim)
    f32 = lambda *s: jax.ShapeDtypeStruct(s, F32)
    operands = [x2d, g, w_bf, lb_raw]
    in_specs = [nat(dm), full(g), _layer_slab(w_bf, layer), full(lb_raw)]
    aliases = {}
    if seq_keep is None:
        cache_t, kv_spec, kv_shape = None, nat(d_attn), f32(t, d_attn)
    else:
        seq, keep = seq_keep
        assert seq % tm == 0 and keep % tm == 0
        bps, first = seq // tm, (seq - keep) // tm
        cache_t = (bps, first)
        kv_spec = pl.BlockSpec((1, 1, d_attn, tm),
                               lambda i: (layer, i // bps, 0, jnp.maximum(i % bps - first, 0)))
        kv_shape = f32(depth, t // seq, d_attn, keep)
        assert all(b.shape == kv_shape.shape for b in cache_bufs)
        operands += list(cache_bufs)
        in_specs += [pl.BlockSpec(memory_space=pl.ANY)] * 2
        aliases = {N_INPROJ_IN: 8, N_INPROJ_IN + 1: 9}
    return pl.pallas_call(
        functools.partial(_inproj_kernel, layer=layer, d_rec=d_rec, d_attn=d_attn, cache_t=cache_t),
        grid=(t // tm,),
        in_specs=in_specs,
        out_specs=[nat(d_rec)] * 5 + [pm] * 3 + [kv_spec] * 2,
        out_shape=[f32(t, d_rec), f32(t, d_rec), jax.ShapeDtypeStruct((t, d_rec), BF16), f32(t, d_rec),
                   jax.ShapeDtypeStruct((t, d_rec), BF16)] + [f32(n_pair, t, LANES)] * 3 + [kv_shape] * 2,
        input_output_aliases=aliases,
        compiler_params=pltpu.CompilerParams(dimension_semantics=("arbitrary",), vmem_limit_bytes=VMEM_LIMIT),
        name="inproj",
    )(*operands)


N_HGRN_IN = 9
N_ATTN_IN = 8


def _hgrn_kernel(*refs, seg, n_chunks, n_attn_heads):
    q_ref, k_ref, v_ref, lf_ref, gate_ref, s0_ref, gn_ref, mat_ref, lv_ref = refs[:N_HGRN_IN]
    st_ref = refs[-1]
    if n_attn_heads:
        o_ref, sout_ref, attn_out_ref = refs[-4:-1]
        rider = _sample_attn_pieces(*refs[N_HGRN_IN:N_HGRN_IN + N_ATTN_IN], attn_out_ref, n_heads=n_attn_heads)
    else:
        o_ref, sout_ref = refs[-3:-1]
        rider = iter(())
    nseg = CHUNK // seg
    seg_shift = seg.bit_length() - 1
    levels = _levels(seg)
    nl = len(levels)
    ti = pl.program_id(2)

    @pl.when(ti == 0)
    def _():
        st_ref[...] = s0_ref[0, :, 0]

    rowi = lax.broadcasted_iota(jnp.int32, (CHUNK, CHUNK), 0)
    coli = lax.broadcasted_iota(jnp.int32, (CHUNK, CHUNK), 1)
    mat = mat_ref[...]
    lv = lv_ref[...]
    gn = gn_ref[...]

    chunk_rows = [slice(c * CHUNK, (c + 1) * CHUNK) for c in range(n_chunks)]
    lf_split = [_split2(lf_ref[rows, :]) for rows in chunk_rows]
    e_lv = _dot(mat[:nl * CHUNK], jnp.concatenate([s[:, :LANES] for s in lf_split], axis=1))
    e_bl = _dot(mat[nl * CHUNK:], jnp.concatenate(lf_split, axis=1))
    next(rider, None)

    intra = []
    for c, rows in enumerate(chunk_rows):
        q = q_ref[rows, :]
        k = k_ref[rows, :]
        a = jnp.zeros((CHUNK, CHUNK), F32)
        for i, h in enumerate(levels):
            second = (rowi & (2 * h - 1)) >= h
            e = e_lv[i * CHUNK:(i + 1) * CHUNK, c * LANES:(c + 1) * LANES]
            x = (jnp.where(second, q, k) * jnp.exp(e)).astype(BF16)
            a = jnp.where(lv == i, _dot_nt(x, x), a)
        a = jnp.where(lv == nl, _dot_nt(q.astype(BF16), k.astype(BF16)), a)
        intra.append(a.astype(BF16))

    states = [st_ref[j] for j in range(nseg)]
    for c, rows in enumerate(chunk_rows):
        q = q_ref[rows, :]
        k = k_ref[rows, :]
        vb = v_ref[rows, :].astype(BF16)
        e = _sum2(e_bl[:, 2 * c * LANES:2 * (c + 1) * LANES])
        eb = jnp.exp(e[:CHUNK])
        qe = (q * eb).astype(BF16)
        kd_t = (k * jnp.exp(e[CHUNK:])).T
        dec_t = eb.T
        o = _dot(intra[c], vb)
        for j in range(nseg):
            s_old = states[j]
            oj = _dot(qe, s_old.astype(BF16))
            kd_j = kd_t
            if nseg > 1:
                oj = jnp.where((rowi >> seg_shift) == j, oj, 0.0)
                kd_j = jnp.where((coli >> seg_shift) == j, kd_t, 0.0)
            o = o + oj
            col = j * seg + seg - 1
            states[j] = s_old * dec_t[:, col:col + 1] + _dot(kd_j.astype(BF16), vb)
        o_ref[rows, :] = (_rms(o, gn) * gate_ref[rows, :]).astype(o_ref.dtype)
        next(rider, None)

    for j in range(nseg):
        st_ref[j] = states[j]
    for _ in rider:
        pass

    @pl.when(ti == pl.num_programs(2) - 1)
    def _():
        sout_ref[0, :, 0] = st_ref[...]


def _hgrn(qr, kr, vr, lf, gate, s0_all, layer, g_norm, seg, rows_per_step, steps_per_seq,
          state_buf, out_layer, sample_attn=None):
    t, d_rec = qr.shape
    n_heads = d_rec // LANES
    nseg = CHUNK // seg
    n_seq = s0_all.shape[1]
    mat, lv = _hgrn_tables(seg)
    grid = (n_seq // nseg, n_heads, steps_per_seq)
    tok = pl.BlockSpec((rows_per_step, LANES), lambda bi, h, ti: (bi * steps_per_seq + ti, h))
    st_in = pl.BlockSpec((1, nseg, 1, REC_EXPAND, REC_DV), lambda bi, h, ti: (layer, bi, h, 0, 0))
    st_out = pl.BlockSpec((1, nseg, 1, REC_EXPAND, REC_DV), lambda bi, h, ti: (out_layer, bi, h, 0, 0))
    full = lambda a: pl.BlockSpec(a.shape, lambda bi, h, ti: (0,) * a.ndim)
    mat_j, lv_j = jnp.asarray(mat, BF16), jnp.asarray(lv)
    operands = [qr, kr, vr, lf, gate, s0_all, g_norm, mat_j, lv_j]
    in_specs = [tok] * 5 + [st_in, pl.BlockSpec((1, LANES), lambda bi, h, ti: (0, h)), full(mat_j), full(lv_j)]
    out_specs = [tok, st_out]
    assert state_buf.shape[1:] == (n_seq, n_heads, REC_EXPAND, REC_DV)
    out_shape = [jax.ShapeDtypeStruct((t, d_rec), BF16), jax.ShapeDtypeStruct(state_buf.shape, F32)]
    n_attn_heads = 0
    if sample_attn is not None:
        q3, kn3, vn3, cache_kt, cache_vt, cache_layer, bias, g_attn = sample_attn
        n_seq_s, t_new, width = q3.shape
        n_past = cache_kt.shape[3]
        n_steps = grid[0] * grid[1] * grid[2]
        assert n_seq_s % n_steps == 0
        per_step = n_seq_s // n_steps
        step = lambda bi, h, ti: (bi * n_heads + h) * steps_per_seq + ti
        tok_s = pl.BlockSpec((per_step, t_new, width), lambda bi, h, ti: (step(bi, h, ti), 0, 0))
        cache = pl.BlockSpec((1, per_step, width, n_past), lambda bi, h, ti: (cache_layer, step(bi, h, ti), 0, 0))
        bd = _head_mean_matrix(width)
        operands += [q3, kn3, vn3, cache_kt, cache_vt, bias, g_attn, bd]
        in_specs += [tok_s, tok_s, tok_s, cache, cache, full(bias), full(g_attn), full(bd)]
        out_specs.append(tok_s)
        out_shape.append(jax.ShapeDtypeStruct(q3.shape, F32))
        n_attn_heads = width // ATTN_HEAD_DIM
    aliases = {len(operands): 1}
    operands.append(state_buf)
    in_specs.append(pl.BlockSpec(memory_space=pl.ANY))
    return pl.pallas_call(
        functools.partial(_hgrn_kernel, seg=seg, n_chunks=rows_per_step // CHUNK, n_attn_heads=n_attn_heads),
        grid=grid,
        in_specs=in_specs,
        out_specs=out_specs,
        out_shape=out_shape,
        input_output_aliases=aliases,
        scratch_shapes=[pltpu.VMEM((nseg, REC_EXPAND, REC_DV), F32)],
        compiler_params=pltpu.CompilerParams(
            dimension_semantics=("parallel", "parallel", "arbitrary"), vmem_limit_bytes=VMEM_LIMIT),
        name="hgrn",
    )(*operands)


ATTN_UNROLL = 4
MERGE_UNROLL = 4


def _prompt_attn_kernel(q_ref, k_ref, v_ref, bias_ref, g_ref, bd_ref, out_ref,
                        oacc_ref, macc_ref, lacc_ref, p_ref, *, seq):
    lane = lax.broadcasted_iota(jnp.int32, (1, LANES), 1)
    lo_half = lane < ATTN_HEAD_DIM
    keeps = (lo_half, jnp.logical_not(lo_half))
    one = jnp.ones((), BF16)
    bb = BAND_BLOCK

    def runs(gi, d, nb):
        run = min(ATTN_UNROLL, nb)
        runs_per_res = nb // run
        for w in range(ATTN_UNROLL // run):
            n = gi * (ATTN_UNROLL // run) + w
            r = n // runs_per_res
            i0 = (n % runs_per_res) * run
            base = r + i0 * (d * bb)
            psl = pl.ds(jnp.maximum(base - d * bb, r), bb, stride=d)
            yield base, i0, psl, run, w * run

    def stage_scores(pi, d, nb, gi, slot):
        for base, i0, psl, run, u0 in runs(gi, d, nb):
            k_prev = k_ref[0, psl, :].astype(BF16)
            for j in range(run):
                qsl = pl.ds(base + j * (d * bb), bb, stride=d)
                k_cur = k_ref[0, qsl, :].astype(BF16)
                k2 = jnp.concatenate([k_prev, k_cur], axis=0)
                q2 = q_ref[0, qsl, :]
                variant = jnp.where(i0 == 0, 1, 0) if j == 0 else 0
                ms = []
                for hh in range(2):
                    qm = jnp.where(keeps[hh], q2, 0.0).astype(BF16)
                    s = _dot_nt(qm, k2) + bias_ref[pi, variant, hh]
                    m = jnp.max(s, axis=-1, keepdims=True)
                    p_ref[slot, 2 * (u0 + j) + hh] = jnp.exp(s - m).astype(BF16)
                    ms.append(m)
                macc_ref[pi, qsl, :] = jnp.where(lo_half, ms[0], ms[1])
                k_prev = k_cur

    def stage_values(pi, d, nb, gi, slot):
        for base, i0, psl, run, u0 in runs(gi, d, nb):
            v_prev = v_ref[0, psl, :].astype(BF16)
            for j in range(run):
                qsl = pl.ds(base + j * (d * bb), bb, stride=d)
                v_cur = v_ref[0, qsl, :].astype(BF16)
                v2 = jnp.concatenate([v_prev, v_cur], axis=0)
                pvs = [_dot(p_ref[slot, 2 * (u0 + j) + hh], jnp.where(keeps[hh], v2, one)) for hh in range(2)]
                oacc_ref[pi, qsl, :] = jnp.where(lo_half, pvs[0], pvs[1])
                lacc_ref[pi, qsl, :] = pltpu.roll(jnp.where(lo_half, pvs[1], pvs[0]), ATTN_HEAD_DIM, axis=1)
                v_prev = v_cur

    for pi, (_, d) in enumerate(DILATED_PATTERNS):
        nb = seq // (d * bb)
        n_groups = (d * nb) // ATTN_UNROLL
        stage_scores(pi, d, nb, 0, 0)

        def overlapped(k, carry, pi=pi, d=d, nb=nb):
            stage_scores(pi, d, nb, 2 * k + 1, 1)
            stage_values(pi, d, nb, 2 * k, 0)
            stage_scores(pi, d, nb, 2 * k + 2, 0)
            stage_values(pi, d, nb, 2 * k + 1, 1)
            return carry

        lax.fori_loop(0, n_groups // 2 - 1, overlapped, 0)
        stage_scores(pi, d, nb, n_groups - 1, 1)
        stage_values(pi, d, nb, n_groups - 2, 0)
        stage_values(pi, d, nb, n_groups - 1, 1)

    g = g_ref[...]
    bd = bd_ref[...]

    def merge(gi, carry):
        for u in range(MERGE_UNROLL):
            rows = pl.ds(pl.multiple_of((gi * MERGE_UNROLL + u) * bb, bb), bb)
            ms = [macc_ref[pi, rows, :] for pi in range(len(DILATED_PATTERNS))]
            mx = functools.reduce(jnp.maximum, ms)
            ws = [jnp.exp(m - mx) for m in ms]
            add = lambda a, b: a + b
            den = functools.reduce(add, [w * lacc_ref[pi, rows, :] for pi, w in enumerate(ws)])
            num = functools.reduce(add, [w * oacc_ref[pi, rows, :] for pi, w in enumerate(ws)])
            o = num / den
            ms = _dot((o * o).astype(BF16), bd)
            out_ref[0, rows, :] = (o * lax.rsqrt(ms + RMS_EPS) * g).astype(out_ref.dtype)
        return carry

    lax.fori_loop(0, seq // (bb * MERGE_UNROLL), merge, 0)


def _head_mean_matrix(width):
    r = np.arange(width)
    return jnp.asarray(((r[:, None] // ATTN_HEAD_DIM) == (r[None, :] // ATTN_HEAD_DIM)) / ATTN_HEAD_DIM, BF16)


def _prompt_attention(qpm, kpm, vpm, band_bias, g_attn, n_seq, seq):
    n_pair = qpm.shape[0]
    n_pat = len(DILATED_PATTERNS)
    for _, d in DILATED_PATTERNS:
        assert seq % (d * BAND_BLOCK) == 0 and (seq // BAND_BLOCK) % (2 * ATTN_UNROLL) == 0
        nb = seq // (d * BAND_BLOCK)
        assert nb % min(ATTN_UNROLL, nb) == 0 and ATTN_UNROLL % min(ATTN_UNROLL, nb) == 0
    assert (seq // BAND_BLOCK) % MERGE_UNROLL == 0
    tok = pl.BlockSpec((1, seq, LANES), lambda p, b: (p, b, 0))
    bd = _head_mean_matrix(LANES)
    return pl.pallas_call(
        functools.partial(_prompt_attn_kernel, seq=seq),
        grid=(n_pair, n_seq),
        in_specs=[tok, tok, tok,
                  pl.BlockSpec((n_pat, 2, 2, BAND_BLOCK, 2 * BAND_BLOCK), lambda p, b: (0, 0, p, 0, 0)),
                  pl.BlockSpec((1, LANES), lambda p, b: (0, p)),
                  pl.BlockSpec(bd.shape, lambda p, b: (0, 0))],
        out_specs=tok,
        out_shape=jax.ShapeDtypeStruct(qpm.shape, BF16),
        scratch_shapes=[pltpu.VMEM((n_pat, seq, LANES), F32)] * 3
        + [pltpu.VMEM((2, 2 * ATTN_UNROLL, BAND_BLOCK, 2 * BAND_BLOCK), BF16)],
        compiler_params=pltpu.CompilerParams(
            dimension_semantics=("parallel", "parallel"), vmem_limit_bytes=VMEM_LIMIT),
        name="prompt_attn",
    )(qpm, kpm, vpm, band_bias, g_attn, bd)


def _sample_attn_pieces(q_ref, kn_ref, vn_ref, ckt_ref, cvt_ref, bias_ref, g_ref, bd_ref, o_ref, *, n_heads):
    n_seq, t_new, width = q_ref.shape
    n_past = ckt_ref.shape[3]
    rows = n_heads * t_new
    own = (lax.broadcasted_iota(jnp.int32, (rows, width), 0) // t_new
           == lax.broadcasted_iota(jnp.int32, (rows, width), 1) // ATTN_HEAD_DIM)
    pad = jnp.zeros((LANES - t_new, width), F32)
    for s in range(n_seq):
        qbd = jnp.where(own, jnp.concatenate([q_ref[s]] * n_heads, axis=0), 0.0).astype(BF16)
        kn = jnp.concatenate([kn_ref[s], pad], axis=0).astype(BF16)
        vn = jnp.concatenate([vn_ref[s], pad], axis=0).astype(BF16)
        s_c = _dot(qbd, ckt_ref[0, s].astype(BF16)) + bias_ref[:, :n_past]
        s_n = _dot_nt(qbd, kn) + bias_ref[:, n_past:]
        m = jnp.maximum(jnp.max(s_c, axis=-1, keepdims=True), jnp.max(s_n, axis=-1, keepdims=True))
        p_c = jnp.exp(s_c - m)
        p_n = jnp.exp(s_n - m)
        l = jnp.sum(p_c, axis=-1, keepdims=True) + jnp.sum(p_n, axis=-1, keepdims=True)
        p_cb = p_c.astype(BF16)
        yield
        parts = []
        for f0 in range(0, width, LANES):
            parts.append(_dot_nt(p_cb, cvt_ref[0, s, f0:f0 + LANES, :].astype(BF16)))
            if f0 + LANES < width:
                yield
        acc = jnp.concatenate(parts, axis=1) + _dot(p_n.astype(BF16), vn)
        o = jnp.where(own, acc * (1.0 / l), 0.0)
        ms = _dot((o * o).astype(BF16), bd_ref[...])
        o = o * lax.rsqrt(ms + RMS_EPS) * g_ref[...]
        out = o[0:t_new]
        for h in range(1, n_heads):
            out = out + o[h * t_new:(h + 1) * t_new]
        o_ref[s] = out


def _mlp_kernel(x_ref, rec_ref, attn_ref, wout_ref, wup_ref, wdn_ref, gpost_ref, gpre_ref, gmpost_ref,
                y_ref, *, ff_chunk):
    d_rec = rec_ref.shape[1]
    mixed = _dot(rec_ref[...].astype(BF16), wout_ref[0, 0:d_rec, :])
    for p in range(attn_ref.shape[0]):
        r0 = d_rec + p * LANES
        mixed = mixed + _dot(attn_ref[p].astype(BF16), wout_ref[0, r0:r0 + LANES, :])
    x1 = x_ref[...] + _rms(mixed, gpost_ref[...])
    h = _rms(x1, gpre_ref[...]).astype(BF16)
    acc = jnp.zeros(x1.shape, F32)
    for c0 in range(0, wup_ref.shape[2], ff_chunk):
        u = jnp.square(jnp.maximum(_dot(h, wup_ref[0, :, c0:c0 + ff_chunk]), 0.0))
        acc = acc + _dot(u.astype(BF16), wdn_ref[0, c0:c0 + ff_chunk, :])
    y_ref[...] = x1 + _rms(acc, gmpost_ref[...])


def _mlp(x2d, rec, attn_pm, wout, wup, wdn, layer, g_post, g_pre, g_mpost, tm):
    t, dm = x2d.shape
    n_pair = attn_pm.shape[0]
    row = lambda wd: pl.BlockSpec((tm, wd), lambda i: (i, 0))
    once = lambda a: pl.BlockSpec(a.shape, lambda i: (0,) * a.ndim, pipeline_mode=pl.Buffered(1))
    return pl.pallas_call(
        functools.partial(_mlp_kernel, ff_chunk=1024),
        grid=(t // tm,),
        in_specs=[row(dm), row(rec.shape[1]), pl.BlockSpec((n_pair, tm, LANES), lambda i: (0, i, 0)),
                  _layer_slab(wout, layer), _layer_slab(wup, layer), _layer_slab(wdn, layer),
                  once(g_post), once(g_pre), once(g_mpost)],
        out_specs=row(dm),
        out_shape=jax.ShapeDtypeStruct((t, dm), F32),
        compiler_params=pltpu.CompilerParams(dimension_semantics=("parallel",), vmem_limit_bytes=VMEM_LIMIT),
        name="mlp",
    )(x2d, rec, attn_pm, wout, wup, wdn, g_post, g_pre, g_mpost)


def kernel(x_prompt, x_sample, state_hgrn, cache_k, cache_v, rel_bias, lb_raw, w_in, w_out, w_up, w_down,
           g_mix_pre, g_mix_post, g_mlp_pre, g_mlp_post, g_rec_out, g_attn_out):
    n_p, seq, dm = x_prompt.shape
    n_s, t_new, _ = x_sample.shape
    depth = w_in.shape[0]
    n_past = cache_k.shape[2]
    n_heads, dh = cache_k.shape[3], cache_k.shape[4]
    d_attn = n_heads * dh
    d_rec = lb_raw.shape[1]
    n_rec_heads = d_rec // REC_EXPAND
    keep = min(MAX_WINDOW, seq)
    tm = 512
    tm_s = min(tm, n_s * t_new)
    assert dh == ATTN_HEAD_DIM and CHUNK % t_new == 0 and seq % tm == 0
    assert (n_s * t_new) % CHUNK == 0 and (n_s * t_new) % tm_s == 0

    band_idx, band_add = _band_tables()
    band_bias = _expand_bias(rel_bias, band_idx, band_add)
    band_bias = band_bias.reshape(len(DILATED_PATTERNS), 2, n_heads, BAND_BLOCK, 2 * BAND_BLOCK)
    s_idx, s_add = _sample_tables(n_past, t_new)
    sample_bias = _expand_bias(rel_bias, s_idx, s_add).reshape(n_heads * t_new, n_past + LANES)

    w_in_b, w_out_b = w_in.astype(BF16), w_out.astype(BF16)
    w_up_b, w_dn_b = w_up.astype(BF16), w_down.astype(BF16)
    row = lambda a, l: a[l][None, :]

    cache_kt = cache_k.transpose(0, 1, 3, 4, 2).reshape(depth, n_s, d_attn, n_past).astype(F32)
    cache_vt = cache_v.transpose(0, 1, 3, 4, 2).reshape(depth, n_s, d_attn, n_past).astype(F32)

    yp = x_prompt.reshape(n_p * seq, dm)
    ys = x_sample.reshape(n_s * t_new, dm)
    zeros_state = jnp.zeros((1, n_p, n_rec_heads, REC_EXPAND, REC_DV), F32)
    state_in = state_hgrn.astype(F32)
    cache_p = (jnp.zeros((depth, n_p, d_attn, keep), F32), jnp.zeros((depth, n_p, d_attn, keep), F32))
    rec_p = jnp.zeros((depth, n_p, n_rec_heads, REC_EXPAND, REC_DV), F32)
    rec_s = jnp.zeros((depth, n_s, n_rec_heads, REC_EXPAND, REC_DV), F32)
    k_s, v_s = [], []
    for l in range(depth):
        g_pre, g_rec, g_att = row(g_mix_pre, l), row(g_rec_out, l), row(g_attn_out, l)
        post = (row(g_mix_post, l), row(g_mlp_pre, l), row(g_mlp_post, l))

        qr, kr, vr, lf, gate, qpm, kpm, vpm, kt, vt = _inproj(
            yp, g_pre, w_in_b, lb_raw, l, tm, seq_keep=(seq, keep), cache_bufs=cache_p)
        cache_p = (kt, vt)
        qr_s, kr_s, vr_s, lf_s, gate_s, qpm_s, _, _, knat, vnat = _inproj(ys, g_pre, w_in_b, lb_raw, l, tm_s)
        q3 = qpm_s.transpose(1, 0, 2).reshape(n_s, t_new, d_attn)
        rider = (q3, knat.reshape(n_s, t_new, d_attn), vnat.reshape(n_s, t_new, d_attn),
                 cache_kt, cache_vt, l, sample_bias, g_att)

        rec, rec_p, attn_s = _hgrn(qr, kr, vr, lf, gate, zeros_state, 0, g_rec, CHUNK, HGRN_ROWS,
                                   seq // HGRN_ROWS, rec_p, l, sample_attn=rider)
        attn = _prompt_attention(qpm, kpm, vpm, band_bias, g_att, n_p, seq)
        yp = _mlp(yp, rec, attn, w_out_b, w_up_b, w_dn_b, l, *post, tm)

        rec, rec_s = _hgrn(qr_s, kr_s, vr_s, lf_s, gate_s, state_in, l, g_rec, t_new, CHUNK, 1, rec_s, l)
        attn_pm = attn_s.reshape(n_s * t_new, d_attn // LANES, LANES).transpose(1, 0, 2)
        ys = _mlp(ys, rec, attn_pm, w_out_b, w_up_b, w_dn_b, l, *post, tm_s)
        k_s.append(knat.reshape(n_s, t_new, n_heads, dh))
        v_s.append(vnat.reshape(n_s, t_new, n_heads, dh))

    to_cache = lambda a: a.reshape(depth, n_p, n_heads, dh, keep).transpose(0, 1, 4, 2, 3)
    return (yp.reshape(n_p, seq, dm), ys.reshape(n_s, t_new, dm), rec_p, to_cache(cache_p[0]),
            to_cache(cache_p[1]), rec_s, jnp.stack(k_s), jnp.stack(v_s))
```

```python
import functools
import math

import numpy as np
import jax
import jax.numpy as jnp
from jax import lax
from jax.experimental import pallas as pl
from jax.experimental.pallas import tpu as pltpu

F32 = jnp.float32
BF16 = jnp.bfloat16

ATTN_HEAD_DIM = 64
REC_EXPAND = 128
REC_DV = 128
DILATED_PATTERNS = ((128, 1), (512, 4), (2048, 16))
MAX_WINDOW = max(w for w, _ in DILATED_PATTERNS)
BAND_BLOCK = 128
NUM_BUCKETS = 32
MAX_DISTANCE = MAX_WINDOW
RMS_EPS = 1e-6
NEG_INF = -1e30
MIN_FORGET = 1e-30

LANES = 128
CHUNK = 128
HGRN_ROWS = 512
VMEM_LIMIT = 56 * 1024 * 1024


def _dot(a, b):
    return jnp.dot(a, b, preferred_element_type=F32)


def _dot_nt(a, b):
    return lax.dot_general(a, b, (((1,), (1,)), ((), ())), preferred_element_type=F32)


def _split2(x):
    hi = x.astype(BF16)
    lo = (x - hi.astype(F32)).astype(BF16)
    return jnp.concatenate([hi, lo], axis=1)


def _sum2(y):
    n = y.shape[1] // 2
    return y[:, :n] + y[:, n:]


def _rms(x, g):
    ms = jnp.mean(x * x, axis=-1, keepdims=True)
    return x * lax.rsqrt(ms + RMS_EPS) * g


def _rel_bucket_np(dist):
    exact = NUM_BUCKETS // 2
    d = np.maximum(dist, exact).astype(np.float64)
    large = exact + (np.log(d / exact) / math.log(MAX_DISTANCE / exact) * (NUM_BUCKETS - exact)).astype(np.int64)
    large = np.minimum(large, NUM_BUCKETS - 1)
    return np.where(dist < exact, dist, large).astype(np.int32)


def _band_tables():
    qi = np.arange(BAND_BLOCK)[:, None]
    ki = np.arange(2 * BAND_BLOCK)[None, :]
    idx, add = [], []
    for w, d in DILATED_PATTERNS:
        taps = w // d
        tap = qi + BAND_BLOCK - ki
        valid = (tap >= 0) & (tap <= taps)
        bucket = _rel_bucket_np(np.maximum(tap, 0) * d)
        for first in (False, True):
            v = valid & (ki >= BAND_BLOCK) if first else valid
            idx.append(bucket)
            add.append(np.where(v, 0.0, NEG_INF))
    return np.stack(idx).astype(np.int32), np.stack(add).astype(np.float32)


def _sample_tables(n_past, t_new):
    t = np.arange(t_new)[:, None]
    col = np.arange(n_past + LANES)[None, :]
    dist = np.where(col < n_past, n_past + t - col, t - (col - n_past))
    ok = (dist >= 0) & ((col < n_past) | (col - n_past < t_new))
    cnt = np.zeros(dist.shape, np.int64)
    for w, d in DILATED_PATTERNS:
        cnt += (ok & (dist % d == 0) & (dist <= w)).astype(np.int64)
    idx = _rel_bucket_np(np.maximum(dist, 0))
    add = np.where(cnt > 0, np.log(np.maximum(cnt, 1)), NEG_INF)
    return idx[None].astype(np.int32), add[None].astype(np.float32)


def _levels(seg):
    out, h = [], seg // 2
    while h >= 1:
        out.append(h)
        h //= 2
    return out


def _hgrn_tables(seg):
    r = np.arange(CHUNK)
    same_seg = (r[:, None] // seg) == (r[None, :] // seg)
    tri = (same_seg & (r[None, :] <= r[:, None])).astype(np.float32)
    eye = np.eye(CHUNK, dtype=np.float32)
    blocks = []
    lv = np.full((CHUNK, CHUNK), -1, np.int32)
    levels = _levels(seg)
    for i, h in enumerate(levels):
        ref = r - (r % (2 * h)) + h - 1
        g = np.zeros((CHUNK, CHUNK), np.float32)
        g[r, ref] = 1.0
        second = (r % (2 * h)) >= h
        sign = np.where(second, 1.0, -1.0)[:, None].astype(np.float32)
        blocks.append((sign * (eye - g)) @ tri)
        pair = ((r[:, None] // (2 * h)) == (r[None, :] // (2 * h))) & second[:, None] & (~second)[None, :]
        lv[pair] = i
    lv[r, r] = len(levels)
    seg_end = r - (r % seg) + seg - 1
    gl = np.zeros((CHUNK, CHUNK), np.float32)
    gl[r, seg_end] = 1.0
    blocks.append(tri)
    blocks.append((gl - eye) @ tri)
    mat = np.concatenate(blocks, axis=0)
    assert np.all(np.isin(mat, (-1.0, 0.0, 1.0)))
    return mat, lv


def _bias_kernel(rb_ref, idx_ref, add_ref, out_ref):
    idx = idx_ref[0]
    for h in range(out_ref.shape[1]):
        acc = add_ref[0]
        for b in range(NUM_BUCKETS):
            acc = acc + jnp.where(idx == b, rb_ref[b, h], 0.0)
        out_ref[0, h] = acc


def _expand_bias(rel_bias, idx, add):
    n, r, c = idx.shape
    n_heads = rel_bias.shape[1]
    return pl.pallas_call(
        _bias_kernel,
        grid=(n,),
        in_specs=[
            pl.BlockSpec(memory_space=pltpu.SMEM),
            pl.BlockSpec((1, r, c), lambda i: (i, 0, 0)),
            pl.BlockSpec((1, r, c), lambda i: (i, 0, 0)),
        ],
        out_specs=pl.BlockSpec((1, n_heads, r, c), lambda i: (i, 0, 0, 0)),
        out_shape=jax.ShapeDtypeStruct((n, n_heads, r, c), F32),
        name="bias_expand",
    )(rel_bias.astype(F32), jnp.asarray(idx), jnp.asarray(add))


N_INPROJ_IN = 4


def _inproj_kernel(*refs, layer, d_rec, d_attn, cache_t):
    x_ref, g_ref, w_ref, lbraw_ref = refs[:N_INPROJ_IN]
    (qr_ref, kr_ref, vr_ref, lf_ref, gate_ref, qpm_ref, kpm_ref, vpm_ref, kout_ref, vout_ref) = refs[-10:]
    h = _rms(x_ref[...], g_ref[...]).astype(BF16)

    raw = lbraw_ref[...]
    e = jnp.exp(raw - jnp.max(raw, axis=0, keepdims=True))
    soft = e / jnp.sum(e, axis=0, keepdims=True)
    cum = soft[0:1]
    for i in range(1, layer + 1):
        cum = cum + soft[i:i + 1]
    lb = cum - soft[0:1]

    def proj(c0, width):
        return _dot(h, w_ref[0, :, c0:c0 + width])

    p = proj(0, d_rec)
    qr_ref[...] = p * jax.nn.sigmoid(p)
    z = proj(d_rec, d_rec)
    f = lb + (1.0 - lb) * jax.nn.sigmoid(z)
    lf_ref[...] = jnp.log(jnp.maximum(f, MIN_FORGET))
    kr_ref[...] = (1.0 - lb) * jax.nn.sigmoid(-z)
    vr_ref[...] = proj(2 * d_rec, d_rec).astype(vr_ref.dtype)
    p = proj(3 * d_rec, d_rec)
    gate_ref[...] = (p * jax.nn.sigmoid(p)).astype(gate_ref.dtype)

    c0 = 4 * d_rec
    qa = proj(c0, d_attn) * (ATTN_HEAD_DIM ** -0.5)
    ka = proj(c0 + d_attn, d_attn)
    va = proj(c0 + 2 * d_attn, d_attn)
    for pr in range(d_attn // LANES):
        sl = slice(pr * LANES, (pr + 1) * LANES)
        qpm_ref[pr] = qa[:, sl]
        kpm_ref[pr] = ka[:, sl]
        vpm_ref[pr] = va[:, sl]
    if cache_t is None:
        kout_ref[...] = ka
        vout_ref[...] = va
    else:
        blocks_per_seq, first_kept = cache_t

        @pl.when(pl.program_id(0) % blocks_per_seq >= first_kept)
        def _():
            kout_ref[0, 0] = ka.T
            vout_ref[0, 0] = va.T


def _layer_slab(a, layer):
    return pl.BlockSpec((1,) + a.shape[1:], lambda i: (layer, 0, 0), pipeline_mode=pl.Buffered(1))


def _inproj(x2d, g, w_bf, lb_raw, layer, tm, seq_keep=None, cache_bufs=None):
    t, dm = x2d.shape
    depth = w_bf.shape[0]
    d_rec = lb_raw.shape[1]
    d_attn = (w_bf.shape[2] - 4 * d_rec) // 3
    n_pair = d_attn // LANES
    nat = lambda wd: pl.BlockSpec((tm, wd), lambda i: (i, 0))
    pm = pl.BlockSpec((n_pair, tm, LANES), lambda i: (0, i, 0))
    full = lambda a: pl.BlockSpec(a.shape, lambda i: (0,) * a.ndim)
    f32 = lambda *s: jax.ShapeDtypeStruct(s, F32)
    operands = [x2d, g, w_bf, lb_raw]
    in_specs = [nat(dm), full(g), _layer_slab(w_bf, layer), full(lb_raw)]
    aliases = {}
    if seq_keep is None:
        cache_t, kv_spec, kv_shape = None, nat(d_attn), f32(t, d_attn)
    else:
        seq, keep = seq_keep
        assert seq % tm == 0 and keep % tm == 0
        bps, first = seq // tm, (seq - keep) // tm
        cache_t = (bps, first)
        kv_spec = pl.BlockSpec((1, 1, d_attn, tm),
                               lambda i: (layer, i // bps, 0, jnp.maximum(i % bps - first, 0)))
        kv_shape = f32(depth, t // seq, d_attn, keep)
        assert all(b.shape == kv_shape.shape for b in cache_bufs)
        operands += list(cache_bufs)
        in_specs += [pl.BlockSpec(memory_space=pl.ANY)] * 2
        aliases = {N_INPROJ_IN: 8, N_INPROJ_IN + 1: 9}
    return pl.pallas_call(
        functools.partial(_inproj_kernel, layer=layer, d_rec=d_rec, d_attn=d_attn, cache_t=cache_t),
        grid=(t // tm,),
        in_specs=in_specs,
        out_specs=[nat(d_rec)] * 5 + [pm] * 3 + [kv_spec] * 2,
        out_shape=[f32(t, d_rec), f32(t, d_rec), jax.ShapeDtypeStruct((t, d_rec), BF16), f32(t, d_rec),
                   jax.ShapeDtypeStruct((t, d_rec), BF16)] + [f32(n_pair, t, LANES)] * 3 + [kv_shape] * 2,
        input_output_aliases=aliases,
        compiler_params=pltpu.CompilerParams(dimension_semantics=("arbitrary",), vmem_limit_bytes=VMEM_LIMIT),
        name="inproj",
    )(*operands)


N_HGRN_IN = 9
N_ATTN_IN = 8


CACHE_RING = 3


def _cache_ring_step(ckt_hbm, cvt_hbm, ring_ref, sem_ref, cache_layer, per_step, lin, n_steps):
    def copies(step, slot):
        rows = pl.ds(step * per_step, per_step)
        return [pltpu.make_async_copy(src.at[cache_layer, rows], ring_ref.at[slot, i], sem_ref.at[slot, i])
                for i, src in enumerate((ckt_hbm, cvt_hbm))]

    @pl.when(lin == 0)
    def _():
        for step in range(min(CACHE_RING - 1, n_steps)):
            for cp in copies(step, step):
                cp.start()

    ahead = lin + (CACHE_RING - 1)

    @pl.when(ahead < n_steps)
    def _():
        for cp in copies(ahead, ahead % CACHE_RING):
            cp.start()

    slot = lin % CACHE_RING

    def wait():
        for cp in copies(lin, slot):
            cp.wait()

    return slot, wait


def _hgrn_kernel(*refs, seg, n_chunks, n_attn_heads, cache_layer, n_steps):
    q_ref, k_ref, v_ref, lf_ref, gate_ref, s0_ref, gn_ref, mat_ref, lv_ref = refs[:N_HGRN_IN]
    st_ref = refs[-1]
    nseg = CHUNK // seg
    seg_shift = seg.bit_length() - 1
    levels = _levels(seg)
    nl = len(levels)
    ti = pl.program_id(2)
    if n_attn_heads:
        o_ref, sout_ref, attn_out_ref, ring_ref, sem_ref = refs[-6:-1]
        aq_ref, akn_ref, avn_ref, ckt_hbm, cvt_hbm, abias_ref, ag_ref, abd_ref = refs[N_HGRN_IN:N_HGRN_IN + N_ATTN_IN]
        lin = (pl.program_id(0) * pl.num_programs(1) + pl.program_id(1)) * pl.num_programs(2) + ti
        slot, wait_cache = _cache_ring_step(ckt_hbm, cvt_hbm, ring_ref, sem_ref, cache_layer,
                                            aq_ref.shape[0], lin, n_steps)
        rider = _sample_attn_pieces(aq_ref, akn_ref, avn_ref, ring_ref.at[slot, 0], ring_ref.at[slot, 1],
                                    abias_ref, ag_ref, abd_ref, attn_out_ref, n_heads=n_attn_heads)
    else:
        o_ref, sout_ref = refs[-3:-1]
        rider, wait_cache = iter(()), None

    @pl.when(ti == 0)
    def _():
        st_ref[...] = s0_ref[0, :, 0]

    rowi = lax.broadcasted_iota(jnp.int32, (CHUNK, CHUNK), 0)
    coli = lax.broadcasted_iota(jnp.int32, (CHUNK, CHUNK), 1)
    mat = mat_ref[...]
    lv = lv_ref[...]
    gn = gn_ref[...]

    chunk_rows = [slice(c * CHUNK, (c + 1) * CHUNK) for c in range(n_chunks)]
    lf_split = [_split2(lf_ref[rows, :]) for rows in chunk_rows]
    e_lv = _dot(mat[:nl * CHUNK], jnp.concatenate([s[:, :LANES] for s in lf_split], axis=1))
    e_bl = _dot(mat[nl * CHUNK:], jnp.concatenate(lf_split, axis=1))
    if wait_cache is not None:
        wait_cache()
    next(rider, None)

    intra = []
    for c, rows in enumerate(chunk_rows):
        q = q_ref[rows, :]
        k = k_ref[rows, :]
        a = jnp.zeros((CHUNK, CHUNK), F32)
        for i, h in enumerate(levels):
            second = (rowi & (2 * h - 1)) >= h
            e = e_lv[i * CHUNK:(i + 1) * CHUNK, c * LANES:(c + 1) * LANES]
            x = (jnp.where(second, q, k) * jnp.exp(e)).astype(BF16)
            a = jnp.where(lv == i, _dot_nt(x, x), a)
        a = jnp.where(lv == nl, _dot_nt(q.astype(BF16), k.astype(BF16)), a)
        intra.append(a.astype(BF16))

    states = [st_ref[j] for j in range(nseg)]
    for c, rows in enumerate(chunk_rows):
        q = q_ref[rows, :]
        k = k_ref[rows, :]
        vb = v_ref[rows, :].astype(BF16)
        e = _sum2(e_bl[:, 2 * c * LANES:2 * (c + 1) * LANES])
        eb = jnp.exp(e[:CHUNK])
        qe = (q * eb).astype(BF16)
        kd_t = (k * jnp.exp(e[CHUNK:])).T
        dec_t = eb.T
        o = _dot(intra[c], vb)
        for j in range(nseg):
            s_old = states[j]
            oj = _dot(qe, s_old.astype(BF16))
            kd_j = kd_t
            if nseg > 1:
                oj = jnp.where((rowi >> seg_shift) == j, oj, 0.0)
                kd_j = jnp.where((coli >> seg_shift) == j, kd_t, 0.0)
            o = o + oj
            col = j * seg + seg - 1
            states[j] = s_old * dec_t[:, col:col + 1] + _dot(kd_j.astype(BF16), vb)
        o_ref[rows, :] = (_rms(o, gn) * gate_ref[rows, :]).astype(o_ref.dtype)
        next(rider, None)

    for j in range(nseg):
        st_ref[j] = states[j]
    for _ in rider:
        pass

    @pl.when(ti == pl.num_programs(2) - 1)
    def _():
        sout_ref[0, :, 0] = st_ref[...]


def _hgrn(qr, kr, vr, lf, gate, s0_all, layer, g_norm, seg, rows_per_step, steps_per_seq,
          state_buf, out_layer, sample_attn=None):
    t, d_rec = qr.shape
    n_heads = d_rec // LANES
    nseg = CHUNK // seg
    n_seq = s0_all.shape[1]
    mat, lv = _hgrn_tables(seg)
    grid = (n_seq // nseg, n_heads, steps_per_seq)
    tok = pl.BlockSpec((rows_per_step, LANES), lambda bi, h, ti: (bi * steps_per_seq + ti, h))
    st_in = pl.BlockSpec((1, nseg, 1, REC_EXPAND, REC_DV), lambda bi, h, ti: (layer, bi, h, 0, 0))
    st_out = pl.BlockSpec((1, nseg, 1, REC_EXPAND, REC_DV), lambda bi, h, ti: (out_layer, bi, h, 0, 0))
    full = lambda a: pl.BlockSpec(a.shape, lambda bi, h, ti: (0,) * a.ndim)
    mat_j, lv_j = jnp.asarray(mat, BF16), jnp.asarray(lv)
    operands = [qr, kr, vr, lf, gate, s0_all, g_norm, mat_j, lv_j]
    in_specs = [tok] * 5 + [st_in, pl.BlockSpec((1, LANES), lambda bi, h, ti: (0, h)), full(mat_j), full(lv_j)]
    out_specs = [tok, st_out]
    assert state_buf.shape[1:] == (n_seq, n_heads, REC_EXPAND, REC_DV)
    out_shape = [jax.ShapeDtypeStruct((t, d_rec), BF16), jax.ShapeDtypeStruct(state_buf.shape, F32)]
    n_attn_heads, cache_layer = 0, None
    n_steps = grid[0] * grid[1] * grid[2]
    scratch = []
    semantics = ("parallel", "parallel", "arbitrary")
    if sample_attn is not None:
        q3, kn3, vn3, cache_kt, cache_vt, cache_layer, bias, g_attn = sample_attn
        n_seq_s, t_new, width = q3.shape
        n_past = cache_kt.shape[3]
        assert n_seq_s % n_steps == 0
        per_step = n_seq_s // n_steps
        step = lambda bi, h, ti: (bi * n_heads + h) * steps_per_seq + ti
        tok_s = pl.BlockSpec((per_step, t_new, width), lambda bi, h, ti: (step(bi, h, ti), 0, 0))
        in_hbm = pl.BlockSpec(memory_space=pl.ANY)
        bd = _head_mean_matrix(width)
        operands += [q3, kn3, vn3, cache_kt, cache_vt, bias, g_attn, bd]
        in_specs += [tok_s, tok_s, tok_s, in_hbm, in_hbm, full(bias), full(g_attn), full(bd)]
        out_specs.append(tok_s)
        out_shape.append(jax.ShapeDtypeStruct(q3.shape, F32))
        n_attn_heads = width // ATTN_HEAD_DIM
        scratch = [pltpu.VMEM((CACHE_RING, 2, per_step, width, n_past), F32),
                   pltpu.SemaphoreType.DMA((CACHE_RING, 2))]
        semantics = ("arbitrary",) * 3
    aliases = {len(operands): 1}
    operands.append(state_buf)
    in_specs.append(pl.BlockSpec(memory_space=pl.ANY))
    return pl.pallas_call(
        functools.partial(_hgrn_kernel, seg=seg, n_chunks=rows_per_step // CHUNK, n_attn_heads=n_attn_heads,
                          cache_layer=cache_layer, n_steps=n_steps),
        grid=grid,
        in_specs=in_specs,
        out_specs=out_specs,
        out_shape=out_shape,
        input_output_aliases=aliases,
        scratch_shapes=scratch + [pltpu.VMEM((nseg, REC_EXPAND, REC_DV), F32)],
        compiler_params=pltpu.CompilerParams(dimension_semantics=semantics, vmem_limit_bytes=VMEM_LIMIT),
        name="hgrn",
    )(*operands)


ATTN_UNROLL = 4
MERGE_UNROLL = 4


def _prompt_attn_kernel(q_ref, k_ref, v_ref, bias_ref, g_ref, bd_ref, out_ref,
                        oacc_ref, macc_ref, lacc_ref, p_ref, *, seq):
    lane = lax.broadcasted_iota(jnp.int32, (1, LANES), 1)
    lo_half = lane < ATTN_HEAD_DIM
    keeps = (lo_half, jnp.logical_not(lo_half))
    one = jnp.ones((), BF16)
    bb = BAND_BLOCK

    def runs(gi, d, nb):
        run = min(ATTN_UNROLL, nb)
        runs_per_res = nb // run
        for w in range(ATTN_UNROLL // run):
            n = gi * (ATTN_UNROLL // run) + w
            r = n // runs_per_res
            i0 = (n % runs_per_res) * run
            base = r + i0 * (d * bb)
            psl = pl.ds(jnp.maximum(base - d * bb, r), bb, stride=d)
            yield base, i0, psl, run, w * run

    def stage_scores(pi, d, nb, gi, slot):
        for base, i0, psl, run, u0 in runs(gi, d, nb):
            k_prev = k_ref[0, psl, :].astype(BF16)
            for j in range(run):
                qsl = pl.ds(base + j * (d * bb), bb, stride=d)
                k_cur = k_ref[0, qsl, :].astype(BF16)
                k2 = jnp.concatenate([k_prev, k_cur], axis=0)
                q2 = q_ref[0, qsl, :]
                variant = jnp.where(i0 == 0, 1, 0) if j == 0 else 0
                ms = []
                for hh in range(2):
                    qm = jnp.where(keeps[hh], q2, 0.0).astype(BF16)
                    s = _dot_nt(qm, k2) + bias_ref[pi, variant, hh]
                    m = jnp.max(s, axis=-1, keepdims=True)
                    p_ref[slot, 2 * (u0 + j) + hh] = jnp.exp(s - m).astype(BF16)
                    ms.append(m)
                macc_ref[pi, qsl, :] = jnp.where(lo_half, ms[0], ms[1])
                k_prev = k_cur

    def stage_values(pi, d, nb, gi, slot):
        for base, i0, psl, run, u0 in runs(gi, d, nb):
            v_prev = v_ref[0, psl, :].astype(BF16)
            for j in range(run):
                qsl = pl.ds(base + j * (d * bb), bb, stride=d)
                v_cur = v_ref[0, qsl, :].astype(BF16)
                v2 = jnp.concatenate([v_prev, v_cur], axis=0)
                pvs = [_dot(p_ref[slot, 2 * (u0 + j) + hh], jnp.where(keeps[hh], v2, one)) for hh in range(2)]
                oacc_ref[pi, qsl, :] = jnp.where(lo_half, pvs[0], pvs[1])
                lacc_ref[pi, qsl, :] = pltpu.roll(jnp.where(lo_half, pvs[1], pvs[0]), ATTN_HEAD_DIM, axis=1)
                v_prev = v_cur

    for pi, (_, d) in enumerate(DILATED_PATTERNS):
        nb = seq // (d * bb)
        n_groups = (d * nb) // ATTN_UNROLL
        stage_scores(pi, d, nb, 0, 0)

        def overlapped(k, carry, pi=pi, d=d, nb=nb):
            stage_scores(pi, d, nb, 2 * k + 1, 1)
            stage_values(pi, d, nb, 2 * k, 0)
            stage_scores(pi, d, nb, 2 * k + 2, 0)
            stage_values(pi, d, nb, 2 * k + 1, 1)
            return carry

        lax.fori_loop(0, n_groups // 2 - 1, overlapped, 0)
        stage_scores(pi, d, nb, n_groups - 1, 1)
        stage_values(pi, d, nb, n_groups - 2, 0)
        stage_values(pi, d, nb, n_groups - 1, 1)

    g = g_ref[...]
    bd = bd_ref[...]

    def merge(gi, carry):
        for u in range(MERGE_UNROLL):
            rows = pl.ds(pl.multiple_of((gi * MERGE_UNROLL + u) * bb, bb), bb)
            ms = [macc_ref[pi, rows, :] for pi in range(len(DILATED_PATTERNS))]
            mx = functools.reduce(jnp.maximum, ms)
            ws = [jnp.exp(m - mx) for m in ms]
            add = lambda a, b: a + b
            den = functools.reduce(add, [w * lacc_ref[pi, rows, :] for pi, w in enumerate(ws)])
            num = functools.reduce(add, [w * oacc_ref[pi, rows, :] for pi, w in enumerate(ws)])
            o = num / den
            ms = _dot((o * o).astype(BF16), bd)
            out_ref[0, rows, :] = (o * lax.rsqrt(ms + RMS_EPS) * g).astype(out_ref.dtype)
        return carry

    lax.fori_loop(0, seq // (bb * MERGE_UNROLL), merge, 0)


def _head_mean_matrix(width):
    r = np.arange(width)
    return jnp.asarray(((r[:, None] // ATTN_HEAD_DIM) == (r[None, :] // ATTN_HEAD_DIM)) / ATTN_HEAD_DIM, BF16)


def _prompt_attention(qpm, kpm, vpm, band_bias, g_attn, n_seq, seq):
    n_pair = qpm.shape[0]
    n_pat = len(DILATED_PATTERNS)
    for _, d in DILATED_PATTERNS:
        assert seq % (d * BAND_BLOCK) == 0 and (seq // BAND_BLOCK) % (2 * ATTN_UNROLL) == 0
        nb = seq // (d * BAND_BLOCK)
        assert nb % min(ATTN_UNROLL, nb) == 0 and ATTN_UNROLL % min(ATTN_UNROLL, nb) == 0
    assert (seq // BAND_BLOCK) % MERGE_UNROLL == 0
    tok = pl.BlockSpec((1, seq, LANES), lambda p, b: (p, b, 0))
    bd = _head_mean_matrix(LANES)
    return pl.pallas_call(
        functools.partial(_prompt_attn_kernel, seq=seq),
        grid=(n_pair, n_seq),
        in_specs=[tok, tok, tok,
                  pl.BlockSpec((n_pat, 2, 2, BAND_BLOCK, 2 * BAND_BLOCK), lambda p, b: (0, 0, p, 0, 0)),
                  pl.BlockSpec((1, LANES), lambda p, b: (0, p)),
                  pl.BlockSpec(bd.shape, lambda p, b: (0, 0))],
        out_specs=tok,
        out_shape=jax.ShapeDtypeStruct(qpm.shape, BF16),
        scratch_shapes=[pltpu.VMEM((n_pat, seq, LANES), F32)] * 3
        + [pltpu.VMEM((2, 2 * ATTN_UNROLL, BAND_BLOCK, 2 * BAND_BLOCK), BF16)],
        compiler_params=pltpu.CompilerParams(
            dimension_semantics=("parallel", "parallel"), vmem_limit_bytes=VMEM_LIMIT),
        name="prompt_attn",
    )(qpm, kpm, vpm, band_bias, g_attn, bd)


def _sample_attn_pieces(q_ref, kn_ref, vn_ref, ckt_ref, cvt_ref, bias_ref, g_ref, bd_ref, o_ref, *, n_heads):
    n_seq, t_new, width = q_ref.shape
    n_past = ckt_ref.shape[2]
    rows = n_heads * t_new
    own = (lax.broadcasted_iota(jnp.int32, (rows, width), 0) // t_new
           == lax.broadcasted_iota(jnp.int32, (rows, width), 1) // ATTN_HEAD_DIM)
    pad = jnp.zeros((LANES - t_new, width), F32)
    for s in range(n_seq):
        qbd = jnp.where(own, jnp.concatenate([q_ref[s]] * n_heads, axis=0), 0.0).astype(BF16)
        kn = jnp.concatenate([kn_ref[s], pad], axis=0).astype(BF16)
        vn = jnp.concatenate([vn_ref[s], pad], axis=0).astype(BF16)
        s_c = _dot(qbd, ckt_ref[s].astype(BF16)) + bias_ref[:, :n_past]
        s_n = _dot_nt(qbd, kn) + bias_ref[:, n_past:]
        m = jnp.maximum(jnp.max(s_c, axis=-1, keepdims=True), jnp.max(s_n, axis=-1, keepdims=True))
        p_c = jnp.exp(s_c - m)
        p_n = jnp.exp(s_n - m)
        l = jnp.sum(p_c, axis=-1, keepdims=True) + jnp.sum(p_n, axis=-1, keepdims=True)
        p_cb = p_c.astype(BF16)
        yield
        parts = []
        for f0 in range(0, width, LANES):
            parts.append(_dot_nt(p_cb, cvt_ref[s, f0:f0 + LANES, :].astype(BF16)))
            if f0 + LANES < width:
                yield
        acc = jnp.concatenate(parts, axis=1) + _dot(p_n.astype(BF16), vn)
        o = jnp.where(own, acc * (1.0 / l), 0.0)
        ms = _dot((o * o).astype(BF16), bd_ref[...])
        o = o * lax.rsqrt(ms + RMS_EPS) * g_ref[...]
        out = o[0:t_new]
        for h in range(1, n_heads):
            out = out + o[h * t_new:(h + 1) * t_new]
        o_ref[s] = out


def _mlp_kernel(x_ref, rec_ref, attn_ref, wout_ref, wup_ref, wdn_ref, gpost_ref, gpre_ref, gmpost_ref,
                y_ref, *, ff_chunk):
    d_rec = rec_ref.shape[1]
    mixed = _dot(rec_ref[...].astype(BF16), wout_ref[0, 0:d_rec, :])
    for p in range(attn_ref.shape[0]):
        r0 = d_rec + p * LANES
        mixed = mixed + _dot(attn_ref[p].astype(BF16), wout_ref[0, r0:r0 + LANES, :])
    x1 = x_ref[...] + _rms(mixed, gpost_ref[...])
    h = _rms(x1, gpre_ref[...]).astype(BF16)
    acc = jnp.zeros(x1.shape, F32)
    for c0 in range(0, wup_ref.shape[2], ff_chunk):
        u = jnp.square(jnp.maximum(_dot(h, wup_ref[0, :, c0:c0 + ff_chunk]), 0.0))
        acc = acc + _dot(u.astype(BF16), wdn_ref[0, c0:c0 + ff_chunk, :])
    y_ref[...] = x1 + _rms(acc, gmpost_ref[...])


def _mlp(x2d, rec, attn_pm, wout, wup, wdn, layer, g_post, g_pre, g_mpost, tm):
    t, dm = x2d.shape
    n_pair = attn_pm.shape[0]
    row = lambda wd: pl.BlockSpec((tm, wd), lambda i: (i, 0))
    once = lambda a: pl.BlockSpec(a.shape, lambda i: (0,) * a.ndim, pipeline_mode=pl.Buffered(1))
    return pl.pallas_call(
        functools.partial(_mlp_kernel, ff_chunk=1024),
        grid=(t // tm,),
        in_specs=[row(dm), row(rec.shape[1]), pl.BlockSpec((n_pair, tm, LANES), lambda i: (0, i, 0)),
                  _layer_slab(wout, layer), _layer_slab(wup, layer), _layer_slab(wdn, layer),
                  once(g_post), once(g_pre), once(g_mpost)],
        out_specs=row(dm),
        out_shape=jax.ShapeDtypeStruct((t, dm), F32),
        compiler_params=pltpu.CompilerParams(dimension_semantics=("parallel",), vmem_limit_bytes=VMEM_LIMIT),
        name="mlp",
    )(x2d, rec, attn_pm, wout, wup, wdn, g_post, g_pre, g_mpost)


def kernel(x_prompt, x_sample, state_hgrn, cache_k, cache_v, rel_bias, lb_raw, w_in, w_out, w_up, w_down,
           g_mix_pre, g_mix_post, g_mlp_pre, g_mlp_post, g_rec_out, g_attn_out):
    n_p, seq, dm = x_prompt.shape
    n_s, t_new, _ = x_sample.shape
    depth = w_in.shape[0]
    n_past = cache_k.shape[2]
    n_heads, dh = cache_k.shape[3], cache_k.shape[4]
    d_attn = n_heads * dh
    d_rec = lb_raw.shape[1]
    n_rec_heads = d_rec // REC_EXPAND
    keep = min(MAX_WINDOW, seq)
    tm = 512
    tm_s = min(tm, n_s * t_new)
    assert dh == ATTN_HEAD_DIM and CHUNK % t_new == 0 and seq % tm == 0
    assert (n_s * t_new) % CHUNK == 0 and (n_s * t_new) % tm_s == 0

    band_idx, band_add = _band_tables()
    band_bias = _expand_bias(rel_bias, band_idx, band_add)
    band_bias = band_bias.reshape(len(DILATED_PATTERNS), 2, n_heads, BAND_BLOCK, 2 * BAND_BLOCK)
    s_idx, s_add = _sample_tables(n_past, t_new)
    sample_bias = _expand_bias(rel_bias, s_idx, s_add).reshape(n_heads * t_new, n_past + LANES)

    w_in_b, w_out_b = w_in.astype(BF16), w_out.astype(BF16)
    w_up_b, w_dn_b = w_up.astype(BF16), w_down.astype(BF16)
    row = lambda a, l: a[l][None, :]

    cache_kt = cache_k.transpose(0, 1, 3, 4, 2).reshape(depth, n_s, d_attn, n_past).astype(F32)
    cache_vt = cache_v.transpose(0, 1, 3, 4, 2).reshape(depth, n_s, d_attn, n_past).astype(F32)

    yp = x_prompt.reshape(n_p * seq, dm)
    ys = x_sample.reshape(n_s * t_new, dm)
    zeros_state = jnp.zeros((1, n_p, n_rec_heads, REC_EXPAND, REC_DV), F32)
    state_in = state_hgrn.astype(F32)
    cache_p = (jnp.zeros((depth, n_p, d_attn, keep), F32), jnp.zeros((depth, n_p, d_attn, keep), F32))
    rec_p = jnp.zeros((depth, n_p, n_rec_heads, REC_EXPAND, REC_DV), F32)
    rec_s = jnp.zeros((depth, n_s, n_rec_heads, REC_EXPAND, REC_DV), F32)
    k_s, v_s = [], []
    for l in range(depth):
        g_pre, g_rec, g_att = row(g_mix_pre, l), row(g_rec_out, l), row(g_attn_out, l)
        post = (row(g_mix_post, l), row(g_mlp_pre, l), row(g_mlp_post, l))

        qr, kr, vr, lf, gate, qpm, kpm, vpm, kt, vt = _inproj(
            yp, g_pre, w_in_b, lb_raw, l, tm, seq_keep=(seq, keep), cache_bufs=cache_p)
        cache_p = (kt, vt)
        qr_s, kr_s, vr_s, lf_s, gate_s, qpm_s, _, _, knat, vnat = _inproj(ys, g_pre, w_in_b, lb_raw, l, tm_s)
        q3 = qpm_s.transpose(1, 0, 2).reshape(n_s, t_new, d_attn)
        rider = (q3, knat.reshape(n_s, t_new, d_attn), vnat.reshape(n_s, t_new, d_attn),
                 cache_kt, cache_vt, l, sample_bias, g_att)

        rec, rec_p, attn_s = _hgrn(qr, kr, vr, lf, gate, zeros_state, 0, g_rec, CHUNK, HGRN_ROWS,
                                   seq // HGRN_ROWS, rec_p, l, sample_attn=rider)
        attn = _prompt_attention(qpm, kpm, vpm, band_bias, g_att, n_p, seq)
        yp = _mlp(yp, rec, attn, w_out_b, w_up_b, w_dn_b, l, *post, tm)

        rec, rec_s = _hgrn(qr_s, kr_s, vr_s, lf_s, gate_s, state_in, l, g_rec, t_new, CHUNK, 1, rec_s, l)
        attn_pm = attn_s.reshape(n_s * t_new, d_attn // LANES, LANES).transpose(1, 0, 2)
        ys = _mlp(ys, rec, attn_pm, w_out_b, w_up_b, w_dn_b, l, *post, tm_s)
        k_s.append(knat.reshape(n_s, t_new, n_heads, dh))
        v_s.append(vnat.reshape(n_s, t_new, n_heads, dh))

    to_cache = lambda a: a.reshape(depth, n_p, n_heads, dh, keep).transpose(0, 1, 4, 2, 3)
    return (yp.reshape(n_p, seq, dm), ys.reshape(n_s, t_new, dm), rec_p, to_cache(cache_p[0]),
            to_cache(cache_p[1]), rec_s, jnp.stack(k_s), jnp.stack(v_s))
```

```python
import functools
import math

import numpy as np
import jax
import jax.numpy as jnp
from jax import lax
from jax.experimental import pallas as pl
from jax.experimental.pallas import tpu as pltpu

F32 = jnp.float32
BF16 = jnp.bfloat16

ATTN_HEAD_DIM = 64
REC_EXPAND = 128
REC_DV = 128
DILATED_PATTERNS = ((128, 1), (512, 4), (2048, 16))
MAX_WINDOW = max(w for w, _ in DILATED_PATTERNS)
BAND_BLOCK = 128
NUM_BUCKETS = 32
MAX_DISTANCE = MAX_WINDOW
RMS_EPS = 1e-6
NEG_INF = -1e30
MIN_FORGET = 1e-30

LANES = 128
CHUNK = 128
HGRN_ROWS = 512
VMEM_LIMIT = 56 * 1024 * 1024


def _dot(a, b):
    return jnp.dot(a, b, preferred_element_type=F32)


def _dot_nt(a, b):
    return lax.dot_general(a, b, (((1,), (1,)), ((), ())), preferred_element_type=F32)


def _split2(x):
    hi = x.astype(BF16)
    lo = (x - hi.astype(F32)).astype(BF16)
    return jnp.concatenate([hi, lo], axis=1)


def _sum2(y):
    n = y.shape[1] // 2
    return y[:, :n] + y[:, n:]


def _rms(x, g):
    ms = jnp.mean(x * x, axis=-1, keepdims=True)
    return x * lax.rsqrt(ms + RMS_EPS) * g


def _rel_bucket_np(dist):
    exact = NUM_BUCKETS // 2
    d = np.maximum(dist, exact).astype(np.float64)
    large = exact + (np.log(d / exact) / math.log(MAX_DISTANCE / exact) * (NUM_BUCKETS - exact)).astype(np.int64)
    large = np.minimum(large, NUM_BUCKETS - 1)
    return np.where(dist < exact, dist, large).astype(np.int32)


def _band_tables():
    qi = np.arange(BAND_BLOCK)[:, None]
    ki = np.arange(2 * BAND_BLOCK)[None, :]
    idx, add = [], []
    for w, d in DILATED_PATTERNS:
        taps = w // d
        tap = qi + BAND_BLOCK - ki
        valid = (tap >= 0) & (tap <= taps)
        bucket = _rel_bucket_np(np.maximum(tap, 0) * d)
        for first in (False, True):
            v = valid & (ki >= BAND_BLOCK) if first else valid
            idx.append(bucket)
            add.append(np.where(v, 0.0, NEG_INF))
    return np.stack(idx).astype(np.int32), np.stack(add).astype(np.float32)


def _sample_tables(n_past, t_new):
    t = np.arange(t_new)[:, None]
    col = np.arange(n_past + LANES)[None, :]
    dist = np.where(col < n_past, n_past + t - col, t - (col - n_past))
    ok = (dist >= 0) & ((col < n_past) | (col - n_past < t_new))
    cnt = np.zeros(dist.shape, np.int64)
    for w, d in DILATED_PATTERNS:
        cnt += (ok & (dist % d == 0) & (dist <= w)).astype(np.int64)
    idx = _rel_bucket_np(np.maximum(dist, 0))
    add = np.where(cnt > 0, np.log(np.maximum(cnt, 1)), NEG_INF)
    return idx[None].astype(np.int32), add[None].astype(np.float32)


def _levels(seg):
    out, h = [], seg // 2
    while h >= 1:
        out.append(h)
        h //= 2
    return out


def _hgrn_tables(seg):
    r = np.arange(CHUNK)
    same_seg = (r[:, None] // seg) == (r[None, :] // seg)
    tri = (same_seg & (r[None, :] <= r[:, None])).astype(np.float32)
    eye = np.eye(CHUNK, dtype=np.float32)
    blocks = []
    lv = np.full((CHUNK, CHUNK), -1, np.int32)
    levels = _levels(seg)
    for i, h in enumerate(levels):
        ref = r - (r % (2 * h)) + h - 1
        g = np.zeros((CHUNK, CHUNK), np.float32)
        g[r, ref] = 1.0
        second = (r % (2 * h)) >= h
        sign = np.where(second, 1.0, -1.0)[:, None].astype(np.float32)
        blocks.append((sign * (eye - g)) @ tri)
        pair = ((r[:, None] // (2 * h)) == (r[None, :] // (2 * h))) & second[:, None] & (~second)[None, :]
        lv[pair] = i
    lv[r, r] = len(levels)
    seg_end = r - (r % seg) + seg - 1
    gl = np.zeros((CHUNK, CHUNK), np.float32)
    gl[r, seg_end] = 1.0
    blocks.append(tri)
    blocks.append((gl - eye) @ tri)
    mat = np.concatenate(blocks, axis=0)
    assert np.all(np.isin(mat, (-1.0, 0.0, 1.0)))
    return mat, lv


def _bias_kernel(rb_ref, idx_ref, add_ref, out_ref):
    idx = idx_ref[0]
    n_heads = out_ref.shape[1]
    vals = [jnp.zeros(idx.shape, F32)] * n_heads
    for b in range(NUM_BUCKETS):
        hit = idx == b
        vals = [jnp.where(hit, rb_ref[b, h], v) for h, v in enumerate(vals)]
    for h in range(n_heads):
        out_ref[0, h] = add_ref[0] + vals[h]


def _expand_bias(rel_bias, idx, add):
    n, r, c = idx.shape
    n_heads = rel_bias.shape[1]
    return pl.pallas_call(
        _bias_kernel,
        grid=(n,),
        in_specs=[
            pl.BlockSpec(memory_space=pltpu.SMEM),
            pl.BlockSpec((1, r, c), lambda i: (i, 0, 0)),
            pl.BlockSpec((1, r, c), lambda i: (i, 0, 0)),
        ],
        out_specs=pl.BlockSpec((1, n_heads, r, c), lambda i: (i, 0, 0, 0)),
        out_shape=jax.ShapeDtypeStruct((n, n_heads, r, c), F32),
        name="bias_expand",
    )(rel_bias.astype(F32), jnp.asarray(idx), jnp.asarray(add))


N_INPROJ_IN = 4


def _inproj_kernel(*refs, layer, d_rec, d_attn, cache_t):
    x_ref, g_ref, w_ref, lbraw_ref = refs[:N_INPROJ_IN]
    (qr_ref, kr_ref, vr_ref, lf_ref, gate_ref, qpm_ref, kpm_ref, vpm_ref, kout_ref, vout_ref) = refs[-10:]
    h = _rms(x_ref[...], g_ref[...]).astype(BF16)

    raw = lbraw_ref[...]
    e = jnp.exp(raw - jnp.max(raw, axis=0, keepdims=True))
    soft = e / jnp.sum(e, axis=0, keepdims=True)
    cum = soft[0:1]
    for i in range(1, layer + 1):
        cum = cum + soft[i:i + 1]
    lb = cum - soft[0:1]

    def proj(c0, width):
        return _dot(h, w_ref[0, :, c0:c0 + width])

    p = proj(0, d_rec)
    qr_ref[...] = (p * jax.nn.sigmoid(p)).astype(qr_ref.dtype)
    z = proj(d_rec, d_rec)
    f = lb + (1.0 - lb) * jax.nn.sigmoid(z)
    lf_ref[...] = jnp.log(jnp.maximum(f, MIN_FORGET))
    kr_ref[...] = ((1.0 - lb) * jax.nn.sigmoid(-z)).astype(kr_ref.dtype)
    vr_ref[...] = proj(2 * d_rec, d_rec).astype(vr_ref.dtype)
    p = proj(3 * d_rec, d_rec)
    gate_ref[...] = (p * jax.nn.sigmoid(p)).astype(gate_ref.dtype)

    c0 = 4 * d_rec
    qa = proj(c0, d_attn) * (ATTN_HEAD_DIM ** -0.5)
    ka = proj(c0 + d_attn, d_attn)
    va = proj(c0 + 2 * d_attn, d_attn)
    for pr in range(d_attn // LANES):
        sl = slice(pr * LANES, (pr + 1) * LANES)
        qpm_ref[pr] = qa[:, sl]
        kpm_ref[pr] = ka[:, sl]
        vpm_ref[pr] = va[:, sl]
    if cache_t is None:
        kout_ref[...] = ka
        vout_ref[...] = va
    else:
        blocks_per_seq, first_kept = cache_t

        @pl.when(pl.program_id(0) % blocks_per_seq >= first_kept)
        def _():
            kout_ref[0, 0] = ka.T
            vout_ref[0, 0] = va.T


def _layer_slab(a, layer):
    return pl.BlockSpec((1,) + a.shape[1:], lambda i: (layer, 0, 0), pipeline_mode=pl.Buffered(1))


def _inproj(x2d, g, w_bf, lb_raw, layer, tm, seq_keep=None, cache_bufs=None):
    t, dm = x2d.shape
    depth = w_bf.shape[0]
    d_rec = lb_raw.shape[1]
    d_attn = (w_bf.shape[2] - 4 * d_rec) // 3
    n_pair = d_attn // LANES
    nat = lambda wd: pl.BlockSpec((tm, wd), lambda i: (i, 0))
    pm = pl.BlockSpec((n_pair, tm, LANES), lambda i: (0, i, 0))
    full = lambda a: pl.BlockSpec(a.shape, lambda i: (0,) * a.ndim)
    f32 = lambda *s: jax.ShapeDtypeStruct(s, F32)
    operands = [x2d, g, w_bf, lb_raw]
    in_specs = [nat(dm), full(g), _layer_slab(w_bf, layer), full(lb_raw)]
    aliases = {}
    if seq_keep is None:
        cache_t, kv_spec, kv_shape = None, nat(d_attn), f32(t, d_attn)
    else:
        seq, keep = seq_keep
        assert seq % tm == 0 and keep % tm == 0
        bps, first = seq // tm, (seq - keep) // tm
        cache_t = (bps, first)
        kv_spec = pl.BlockSpec((1, 1, d_attn, tm),
                               lambda i: (layer, i // bps, 0, jnp.maximum(i % bps - first, 0)))
        kv_shape = f32(depth, t // seq, d_attn, keep)
        assert all(b.shape == kv_shape.shape for b in cache_bufs)
        operands += list(cache_bufs)
        in_specs += [pl.BlockSpec(memory_space=pl.ANY)] * 2
        aliases = {N_INPROJ_IN: 8, N_INPROJ_IN + 1: 9}
    return pl.pallas_call(
        functools.partial(_inproj_kernel, layer=layer, d_rec=d_rec, d_attn=d_attn, cache_t=cache_t),
        grid=(t // tm,),
        in_specs=in_specs,
        out_specs=[nat(d_rec)] * 5 + [pm] * 3 + [kv_spec] * 2,
        out_shape=[jax.ShapeDtypeStruct((t, d_rec), BF16)] * 3 + [f32(t, d_rec),
                   jax.ShapeDtypeStruct((t, d_rec), BF16)] + [f32(n_pair, t, LANES)] * 3 + [kv_shape] * 2,
        input_output_aliases=aliases,
        compiler_params=pltpu.CompilerParams(dimension_semantics=("arbitrary",), vmem_limit_bytes=VMEM_LIMIT),
        name="inproj",
    )(*operands)


N_HGRN_IN = 9
N_ATTN_IN = 8


CACHE_RING = 3


def _cache_ring_step(ckt_hbm, cvt_hbm, ring_ref, sem_ref, cache_layer, per_step, lin, n_steps):
    def copies(step, slot):
        rows = pl.ds(step * per_step, per_step)
        return [pltpu.make_async_copy(src.at[cache_layer, rows], ring_ref.at[slot, i], sem_ref.at[slot, i])
                for i, src in enumerate((ckt_hbm, cvt_hbm))]

    @pl.when(lin == 0)
    def _():
        for step in range(min(CACHE_RING - 1, n_steps)):
            for cp in copies(step, step):
                cp.start()

    ahead = lin + (CACHE_RING - 1)

    @pl.when(ahead < n_steps)
    def _():
        for cp in copies(ahead, ahead % CACHE_RING):
            cp.start()

    slot = lin % CACHE_RING

    def wait():
        for cp in copies(lin, slot):
            cp.wait()

    return slot, wait


def _hgrn_kernel(*refs, seg, n_chunks, n_attn_heads, cache_layer, n_steps):
    q_ref, k_ref, v_ref, lf_ref, gate_ref, s0_ref, gn_ref, mat_ref, lv_ref = refs[:N_HGRN_IN]
    st_ref = refs[-1]
    nseg = CHUNK // seg
    seg_shift = seg.bit_length() - 1
    levels = _levels(seg)
    nl = len(levels)
    ti = pl.program_id(2)
    if n_attn_heads:
        o_ref, sout_ref, attn_out_ref, ring_ref, sem_ref = refs[-6:-1]
        aq_ref, akn_ref, avn_ref, ckt_hbm, cvt_hbm, abias_ref, ag_ref, abd_ref = refs[N_HGRN_IN:N_HGRN_IN + N_ATTN_IN]
        lin = (pl.program_id(0) * pl.num_programs(1) + pl.program_id(1)) * pl.num_programs(2) + ti
        slot, wait_cache = _cache_ring_step(ckt_hbm, cvt_hbm, ring_ref, sem_ref, cache_layer,
                                            aq_ref.shape[0], lin, n_steps)
        rider = _sample_attn_pieces(aq_ref, akn_ref, avn_ref, ring_ref.at[slot, 0], ring_ref.at[slot, 1],
                                    abias_ref, ag_ref, abd_ref, attn_out_ref, n_heads=n_attn_heads)
    else:
        o_ref, sout_ref = refs[-3:-1]
        rider, wait_cache = iter(()), None

    @pl.when(ti == 0)
    def _():
        st_ref[...] = s0_ref[0, :, 0]

    rowi = lax.broadcasted_iota(jnp.int32, (CHUNK, CHUNK), 0)
    coli = lax.broadcasted_iota(jnp.int32, (CHUNK, CHUNK), 1)
    mat = mat_ref[...]
    lv = lv_ref[...]
    gn = gn_ref[...]

    chunk_rows = [slice(c * CHUNK, (c + 1) * CHUNK) for c in range(n_chunks)]
    lf_split = [_split2(lf_ref[rows, :]) for rows in chunk_rows]
    e_lv = _dot(mat[:nl * CHUNK], jnp.concatenate([s[:, :LANES] for s in lf_split], axis=1))
    e_bl = _dot(mat[nl * CHUNK:], jnp.concatenate(lf_split, axis=1))
    if wait_cache is not None:
        wait_cache()
    next(rider, None)

    intra = []
    for c, rows in enumerate(chunk_rows):
        q = q_ref[rows, :]
        k = k_ref[rows, :]
        a = jnp.zeros((CHUNK, CHUNK), F32)
        for i, h in enumerate(levels):
            second = (rowi & (2 * h - 1)) >= h
            e = e_lv[i * CHUNK:(i + 1) * CHUNK, c * LANES:(c + 1) * LANES]
            x = (jnp.where(second, q, k) * jnp.exp(e)).astype(BF16)
            a = jnp.where(lv == i, _dot_nt(x, x), a)
        a = jnp.where(lv == nl, _dot_nt(q.astype(BF16), k.astype(BF16)), a)
        intra.append(a.astype(BF16))

    states = [st_ref[j] for j in range(nseg)]
    for c, rows in enumerate(chunk_rows):
        q = q_ref[rows, :]
        k = k_ref[rows, :]
        vb = v_ref[rows, :].astype(BF16)
        e = _sum2(e_bl[:, 2 * c * LANES:2 * (c + 1) * LANES])
        eb = jnp.exp(e[:CHUNK])
        qe = (q * eb).astype(BF16)
        kd_t = (k * jnp.exp(e[CHUNK:])).T
        dec_t = eb.T
        o = _dot(intra[c], vb)
        for j in range(nseg):
            s_old = states[j]
            oj = _dot(qe, s_old.astype(BF16))
            kd_j = kd_t
            if nseg > 1:
                oj = jnp.where((rowi >> seg_shift) == j, oj, 0.0)
                kd_j = jnp.where((coli >> seg_shift) == j, kd_t, 0.0)
            o = o + oj
            col = j * seg + seg - 1
            states[j] = s_old * dec_t[:, col:col + 1] + _dot(kd_j.astype(BF16), vb)
        o_ref[rows, :] = (_rms(o, gn) * gate_ref[rows, :]).astype(o_ref.dtype)
        next(rider, None)

    for j in range(nseg):
        st_ref[j] = states[j]
    for _ in rider:
        pass

    @pl.when(ti == pl.num_programs(2) - 1)
    def _():
        sout_ref[0, :, 0] = st_ref[...]


def _hgrn(qr, kr, vr, lf, gate, s0_all, layer, g_norm, seg, rows_per_step, steps_per_seq,
          state_buf, out_layer, sample_attn=None):
    t, d_rec = qr.shape
    n_heads = d_rec // LANES
    nseg = CHUNK // seg
    n_seq = s0_all.shape[1]
    mat, lv = _hgrn_tables(seg)
    grid = (n_seq // nseg, n_heads, steps_per_seq)
    tok = pl.BlockSpec((rows_per_step, LANES), lambda bi, h, ti: (bi * steps_per_seq + ti, h))
    st_in = pl.BlockSpec((1, nseg, 1, REC_EXPAND, REC_DV), lambda bi, h, ti: (layer, bi, h, 0, 0))
    st_out = pl.BlockSpec((1, nseg, 1, REC_EXPAND, REC_DV), lambda bi, h, ti: (out_layer, bi, h, 0, 0))
    full = lambda a: pl.BlockSpec(a.shape, lambda bi, h, ti: (0,) * a.ndim)
    mat_j, lv_j = jnp.asarray(mat, BF16), jnp.asarray(lv)
    operands = [qr, kr, vr, lf, gate, s0_all, g_norm, mat_j, lv_j]
    in_specs = [tok] * 5 + [st_in, pl.BlockSpec((1, LANES), lambda bi, h, ti: (0, h)), full(mat_j), full(lv_j)]
    out_specs = [tok, st_out]
    assert state_buf.shape[1:] == (n_seq, n_heads, REC_EXPAND, REC_DV)
    out_shape = [jax.ShapeDtypeStruct((t, d_rec), BF16), jax.ShapeDtypeStruct(state_buf.shape, F32)]
    n_attn_heads, cache_layer = 0, None
    n_steps = grid[0] * grid[1] * grid[2]
    scratch = []
    semantics = ("parallel", "parallel", "arbitrary")
    if sample_attn is not None:
        q3, kn3, vn3, cache_kt, cache_vt, cache_layer, bias, g_attn = sample_attn
        n_seq_s, t_new, width = q3.shape
        n_past = cache_kt.shape[3]
        assert n_seq_s % n_steps == 0
        per_step = n_seq_s // n_steps
        step = lambda bi, h, ti: (bi * n_heads + h) * steps_per_seq + ti
        tok_s = pl.BlockSpec((per_step, t_new, width), lambda bi, h, ti: (step(bi, h, ti), 0, 0))
        in_hbm = pl.BlockSpec(memory_space=pl.ANY)
        bd = _head_mean_matrix(width)
        operands += [q3, kn3, vn3, cache_kt, cache_vt, bias, g_attn, bd]
        in_specs += [tok_s, tok_s, tok_s, in_hbm, in_hbm, full(bias), full(g_attn), full(bd)]
        out_specs.append(tok_s)
        out_shape.append(jax.ShapeDtypeStruct(q3.shape, F32))
        n_attn_heads = width // ATTN_HEAD_DIM
        scratch = [pltpu.VMEM((CACHE_RING, 2, per_step, width, n_past), F32),
                   pltpu.SemaphoreType.DMA((CACHE_RING, 2))]
        semantics = ("arbitrary",) * 3
    aliases = {len(operands): 1}
    operands.append(state_buf)
    in_specs.append(pl.BlockSpec(memory_space=pl.ANY))
    return pl.pallas_call(
        functools.partial(_hgrn_kernel, seg=seg, n_chunks=rows_per_step // CHUNK, n_attn_heads=n_attn_heads,
                          cache_layer=cache_layer, n_steps=n_steps),
        grid=grid,
        in_specs=in_specs,
        out_specs=out_specs,
        out_shape=out_shape,
        input_output_aliases=aliases,
        scratch_shapes=scratch + [pltpu.VMEM((nseg, REC_EXPAND, REC_DV), F32)],
        compiler_params=pltpu.CompilerParams(dimension_semantics=semantics, vmem_limit_bytes=VMEM_LIMIT),
        name="hgrn",
    )(*operands)


ATTN_UNROLL = 4
MERGE_UNROLL = 4


def _prompt_attn_kernel(q_ref, k_ref, v_ref, bias_ref, g_ref, bd_ref, out_ref,
                        oacc_ref, macc_ref, lacc_ref, p_ref, *, seq):
    lane = lax.broadcasted_iota(jnp.int32, (1, LANES), 1)
    lo_half = lane < ATTN_HEAD_DIM
    keeps = (lo_half, jnp.logical_not(lo_half))
    one = jnp.ones((), BF16)
    bb = BAND_BLOCK

    def runs(gi, d, nb):
        run = min(ATTN_UNROLL, nb)
        runs_per_res = nb // run
        for w in range(ATTN_UNROLL // run):
            n = gi * (ATTN_UNROLL // run) + w
            r = n // runs_per_res
            i0 = (n % runs_per_res) * run
            base = r + i0 * (d * bb)
            psl = pl.ds(jnp.maximum(base - d * bb, r), bb, stride=d)
            yield base, i0, psl, run, w * run

    def stage_scores(pi, d, nb, gi, slot):
        for base, i0, psl, run, u0 in runs(gi, d, nb):
            k_prev = k_ref[0, psl, :].astype(BF16)
            for j in range(run):
                qsl = pl.ds(base + j * (d * bb), bb, stride=d)
                k_cur = k_ref[0, qsl, :].astype(BF16)
                k2 = jnp.concatenate([k_prev, k_cur], axis=0)
                q2 = q_ref[0, qsl, :]
                variant = jnp.where(i0 == 0, 1, 0) if j == 0 else 0
                ms = []
                for hh in range(2):
                    qm = jnp.where(keeps[hh], q2, 0.0).astype(BF16)
                    s = _dot_nt(qm, k2) + bias_ref[pi, variant, hh]
                    m = jnp.max(s, axis=-1, keepdims=True)
                    p_ref[slot, 2 * (u0 + j) + hh] = jnp.exp(s - m).astype(BF16)
                    ms.append(m)
                macc_ref[pi, qsl, :] = jnp.where(lo_half, ms[0], ms[1])
                k_prev = k_cur

    def stage_values(pi, d, nb, gi, slot):
        for base, i0, psl, run, u0 in runs(gi, d, nb):
            v_prev = v_ref[0, psl, :].astype(BF16)
            for j in range(run):
                qsl = pl.ds(base + j * (d * bb), bb, stride=d)
                v_cur = v_ref[0, qsl, :].astype(BF16)
                v2 = jnp.concatenate([v_prev, v_cur], axis=0)
                pvs = [_dot(p_ref[slot, 2 * (u0 + j) + hh], jnp.where(keeps[hh], v2, one)) for hh in range(2)]
                oacc_ref[pi, qsl, :] = jnp.where(lo_half, pvs[0], pvs[1])
                lacc_ref[pi, qsl, :] = pltpu.roll(jnp.where(lo_half, pvs[1], pvs[0]), ATTN_HEAD_DIM, axis=1)
                v_prev = v_cur

    for pi, (_, d) in enumerate(DILATED_PATTERNS):
        nb = seq // (d * bb)
        n_groups = (d * nb) // ATTN_UNROLL
        stage_scores(pi, d, nb, 0, 0)

        def overlapped(k, carry, pi=pi, d=d, nb=nb):
            stage_scores(pi, d, nb, 2 * k + 1, 1)
            stage_values(pi, d, nb, 2 * k, 0)
            stage_scores(pi, d, nb, 2 * k + 2, 0)
            stage_values(pi, d, nb, 2 * k + 1, 1)
            return carry

        lax.fori_loop(0, n_groups // 2 - 1, overlapped, 0)
        stage_scores(pi, d, nb, n_groups - 1, 1)
        stage_values(pi, d, nb, n_groups - 2, 0)
        stage_values(pi, d, nb, n_groups - 1, 1)

    g = g_ref[...]
    bd = bd_ref[...]

    def merge(gi, carry):
        for u in range(MERGE_UNROLL):
            rows = pl.ds(pl.multiple_of((gi * MERGE_UNROLL + u) * bb, bb), bb)
            ms = [macc_ref[pi, rows, :] for pi in range(len(DILATED_PATTERNS))]
            mx = functools.reduce(jnp.maximum, ms)
            ws = [jnp.exp(m - mx) for m in ms]
            add = lambda a, b: a + b
            den = functools.reduce(add, [w * lacc_ref[pi, rows, :] for pi, w in enumerate(ws)])
            num = functools.reduce(add, [w * oacc_ref[pi, rows, :] for pi, w in enumerate(ws)])
            o = num / den
            ms = _dot((o * o).astype(BF16), bd)
            out_ref[0, rows, :] = (o * lax.rsqrt(ms + RMS_EPS) * g).astype(out_ref.dtype)
        return carry

    lax.fori_loop(0, seq // (bb * MERGE_UNROLL), merge, 0)


def _head_mean_matrix(width):
    r = np.arange(width)
    return jnp.asarray(((r[:, None] // ATTN_HEAD_DIM) == (r[None, :] // ATTN_HEAD_DIM)) / ATTN_HEAD_DIM, BF16)


def _prompt_attention(qpm, kpm, vpm, band_bias, g_attn, n_seq, seq):
    n_pair = qpm.shape[0]
    n_pat = len(DILATED_PATTERNS)
    for _, d in DILATED_PATTERNS:
        assert seq % (d * BAND_BLOCK) == 0 and (seq // BAND_BLOCK) % (2 * ATTN_UNROLL) == 0
        nb = seq // (d * BAND_BLOCK)
        assert nb % min(ATTN_UNROLL, nb) == 0 and ATTN_UNROLL % min(ATTN_UNROLL, nb) == 0
    assert (seq // BAND_BLOCK) % MERGE_UNROLL == 0
    tok = pl.BlockSpec((1, seq, LANES), lambda p, b: (p, b, 0))
    bd = _head_mean_matrix(LANES)
    return pl.pallas_call(
        functools.partial(_prompt_attn_kernel, seq=seq),
        grid=(n_pair, n_seq),
        in_specs=[tok, tok, tok,
                  pl.BlockSpec((n_pat, 2, 2, BAND_BLOCK, 2 * BAND_BLOCK), lambda p, b: (0, 0, p, 0, 0)),
                  pl.BlockSpec((1, LANES), lambda p, b: (0, p)),
                  pl.BlockSpec(bd.shape, lambda p, b: (0, 0))],
        out_specs=tok,
        out_shape=jax.ShapeDtypeStruct(qpm.shape, BF16),
        scratch_shapes=[pltpu.VMEM((n_pat, seq, LANES), F32)] * 3
        + [pltpu.VMEM((2, 2 * ATTN_UNROLL, BAND_BLOCK, 2 * BAND_BLOCK), BF16)],
        compiler_params=pltpu.CompilerParams(
            dimension_semantics=("parallel", "parallel"), vmem_limit_bytes=VMEM_LIMIT),
        name="prompt_attn",
    )(qpm, kpm, vpm, band_bias, g_attn, bd)


def _sample_attn_pieces(q_ref, kn_ref, vn_ref, ckt_ref, cvt_ref, bias_ref, g_ref, bd_ref, o_ref, *, n_heads):
    n_seq, t_new, width = q_ref.shape
    n_past = ckt_ref.shape[2]
    rows = n_heads * t_new
    own = (lax.broadcasted_iota(jnp.int32, (rows, width), 0) // t_new
           == lax.broadcasted_iota(jnp.int32, (rows, width), 1) // ATTN_HEAD_DIM)
    pad = jnp.zeros((LANES - t_new, width), F32)
    for s in range(n_seq):
        qbd = jnp.where(own, jnp.concatenate([q_ref[s]] * n_heads, axis=0), 0.0).astype(BF16)
        kn = jnp.concatenate([kn_ref[s], pad], axis=0).astype(BF16)
        vn = jnp.concatenate([vn_ref[s], pad], axis=0).astype(BF16)
        s_c = _dot(qbd, ckt_ref[s].astype(BF16)) + bias_ref[:, :n_past]
        s_n = _dot_nt(qbd, kn) + bias_ref[:, n_past:]
        m = jnp.maximum(jnp.max(s_c, axis=-1, keepdims=True), jnp.max(s_n, axis=-1, keepdims=True))
        p_c = jnp.exp(s_c - m)
        p_n = jnp.exp(s_n - m)
        l = jnp.sum(p_c, axis=-1, keepdims=True) + jnp.sum(p_n, axis=-1, keepdims=True)
        p_cb = p_c.astype(BF16)
        yield
        parts = []
        for f0 in range(0, width, LANES):
            parts.append(_dot_nt(p_cb, cvt_ref[s, f0:f0 + LANES, :].astype(BF16)))
            if f0 + LANES < width:
                yield
        acc = jnp.concatenate(parts, axis=1) + _dot(p_n.astype(BF16), vn)
        o = jnp.where(own, acc * (1.0 / l), 0.0)
        ms = _dot((o * o).astype(BF16), bd_ref[...])
        o = o * lax.rsqrt(ms + RMS_EPS) * g_ref[...]
        out = o[0:t_new]
        for h in range(1, n_heads):
            out = out + o[h * t_new:(h + 1) * t_new]
        o_ref[s] = out


def _mlp_kernel(x_ref, rec_ref, attn_ref, wout_ref, wup_ref, wdn_ref, gpost_ref, gpre_ref, gmpost_ref,
                y_ref, *, ff_chunk):
    d_rec = rec_ref.shape[1]
    mixed = _dot(rec_ref[...].astype(BF16), wout_ref[0, 0:d_rec, :])
    for p in range(attn_ref.shape[0]):
        r0 = d_rec + p * LANES
        mixed = mixed + _dot(attn_ref[p].astype(BF16), wout_ref[0, r0:r0 + LANES, :])
    x1 = x_ref[...] + _rms(mixed, gpost_ref[...])
    h = _rms(x1, gpre_ref[...]).astype(BF16)
    acc = jnp.zeros(x1.shape, F32)
    for c0 in range(0, wup_ref.shape[2], ff_chunk):
        u = jnp.square(jnp.maximum(_dot(h, wup_ref[0, :, c0:c0 + ff_chunk]), 0.0))
        acc = acc + _dot(u.astype(BF16), wdn_ref[0, c0:c0 + ff_chunk, :])
    y_ref[...] = x1 + _rms(acc, gmpost_ref[...])


def _mlp(x2d, rec, attn_pm, wout, wup, wdn, layer, g_post, g_pre, g_mpost, tm):
    t, dm = x2d.shape
    n_pair = attn_pm.shape[0]
    row = lambda wd: pl.BlockSpec((tm, wd), lambda i: (i, 0))
    once = lambda a: pl.BlockSpec(a.shape, lambda i: (0,) * a.ndim, pipeline_mode=pl.Buffered(1))
    return pl.pallas_call(
        functools.partial(_mlp_kernel, ff_chunk=1024),
        grid=(t // tm,),
        in_specs=[row(dm), row(rec.shape[1]), pl.BlockSpec((n_pair, tm, LANES), lambda i: (0, i, 0)),
                  _layer_slab(wout, layer), _layer_slab(wup, layer), _layer_slab(wdn, layer),
                  once(g_post), once(g_pre), once(g_mpost)],
        out_specs=row(dm),
        out_shape=jax.ShapeDtypeStruct((t, dm), F32),
        compiler_params=pltpu.CompilerParams(dimension_semantics=("parallel",), vmem_limit_bytes=VMEM_LIMIT),
        name="mlp",
    )(x2d, rec, attn_pm, wout, wup, wdn, g_post, g_pre, g_mpost)


def kernel(x_prompt, x_sample, state_hgrn, cache_k, cache_v, rel_bias, lb_raw, w_in, w_out, w_up, w_down,
           g_mix_pre, g_mix_post, g_mlp_pre, g_mlp_post, g_rec_out, g_attn_out):
    n_p, seq, dm = x_prompt.shape
    n_s, t_new, _ = x_sample.shape
    depth = w_in.shape[0]
    n_past = cache_k.shape[2]
    n_heads, dh = cache_k.shape[3], cache_k.shape[4]
    d_attn = n_heads * dh
    d_rec = lb_raw.shape[1]
    n_rec_heads = d_rec // REC_EXPAND
    keep = min(MAX_WINDOW, seq)
    tm = 512
    tm_s = min(tm, n_s * t_new)
    assert dh == ATTN_HEAD_DIM and CHUNK % t_new == 0 and seq % tm == 0
    assert (n_s * t_new) % CHUNK == 0 and (n_s * t_new) % tm_s == 0

    band_idx, band_add = _band_tables()
    band_bias = _expand_bias(rel_bias, band_idx, band_add)
    band_bias = band_bias.reshape(len(DILATED_PATTERNS), 2, n_heads, BAND_BLOCK, 2 * BAND_BLOCK)
    s_idx, s_add = _sample_tables(n_past, t_new)
    sample_bias = _expand_bias(rel_bias, s_idx, s_add).reshape(n_heads * t_new, n_past + LANES)

    w_in_b, w_out_b = w_in.astype(BF16), w_out.astype(BF16)
    w_up_b, w_dn_b = w_up.astype(BF16), w_down.astype(BF16)
    row = lambda a, l: a[l][None, :]

    cache_kt = cache_k.transpose(0, 1, 3, 4, 2).reshape(depth, n_s, d_attn, n_past).astype(F32)
    cache_vt = cache_v.transpose(0, 1, 3, 4, 2).reshape(depth, n_s, d_attn, n_past).astype(F32)

    yp = x_prompt.reshape(n_p * seq, dm)
    ys = x_sample.reshape(n_s * t_new, dm)
    zeros_state = jnp.zeros((1, n_p, n_rec_heads, REC_EXPAND, REC_DV), F32)
    state_in = state_hgrn.astype(F32)
    cache_p = (jnp.zeros((depth, n_p, d_attn, keep), F32), jnp.zeros((depth, n_p, d_attn, keep), F32))
    rec_p = jnp.zeros((depth, n_p, n_rec_heads, REC_EXPAND, REC_DV), F32)
    rec_s = jnp.zeros((depth, n_s, n_rec_heads, REC_EXPAND, REC_DV), F32)
    k_s, v_s = [], []
    for l in range(depth):
        g_pre, g_rec, g_att = row(g_mix_pre, l), row(g_rec_out, l), row(g_attn_out, l)
        post = (row(g_mix_post, l), row(g_mlp_pre, l), row(g_mlp_post, l))

        qr, kr, vr, lf, gate, qpm, kpm, vpm, kt, vt = _inproj(
            yp, g_pre, w_in_b, lb_raw, l, tm, seq_keep=(seq, keep), cache_bufs=cache_p)
        cache_p = (kt, vt)
        qr_s, kr_s, vr_s, lf_s, gate_s, qpm_s, _, _, knat, vnat = _inproj(ys, g_pre, w_in_b, lb_raw, l, tm_s)
        q3 = qpm_s.transpose(1, 0, 2).reshape(n_s, t_new, d_attn)
        rider = (q3, knat.reshape(n_s, t_new, d_attn), vnat.reshape(n_s, t_new, d_attn),
                 cache_kt, cache_vt, l, sample_bias, g_att)

        rec, rec_p, attn_s = _hgrn(qr, kr, vr, lf, gate, zeros_state, 0, g_rec, CHUNK, HGRN_ROWS,
                                   seq // HGRN_ROWS, rec_p, l, sample_attn=rider)
        attn = _prompt_attention(qpm, kpm, vpm, band_bias, g_att, n_p, seq)
        yp = _mlp(yp, rec, attn, w_out_b, w_up_b, w_dn_b, l, *post, tm)

        rec, rec_s = _hgrn(qr_s, kr_s, vr_s, lf_s, gate_s, state_in, l, g_rec, t_new, CHUNK, 1, rec_s, l)
        attn_pm = attn_s.reshape(n_s * t_new, d_attn // LANES, LANES).transpose(1, 0, 2)
        ys = _mlp(ys, rec, attn_pm, w_out_b, w_up_b, w_dn_b, l, *post, tm_s)
        k_s.append(knat.reshape(n_s, t_new, n_heads, dh))
        v_s.append(vnat.reshape(n_s, t_new, n_heads, dh))

    to_cache = lambda a: a.reshape(depth, n_p, n_heads, dh, keep).transpose(0, 1, 4, 2, 3)
    return (yp.reshape(n_p, seq, dm), ys.reshape(n_s, t_new, dm), rec_p, to_cache(cache_p[0]),
            to_cache(cache_p[1]), rec_s, jnp.stack(k_s), jnp.stack(v_s))
```

```python
import functools
import math

import numpy as np
import jax
import jax.numpy as jnp
from jax import lax
from jax.experimental import pallas as pl
from jax.experimental.pallas import tpu as pltpu

F32 = jnp.float32
BF16 = jnp.bfloat16

ATTN_HEAD_DIM = 64
REC_EXPAND = 128
REC_DV = 128
DILATED_PATTERNS = ((128, 1), (512, 4), (2048, 16))
MAX_WINDOW = max(w for w, _ in DILATED_PATTERNS)
BAND_BLOCK = 128
NUM_BUCKETS = 32
MAX_DISTANCE = MAX_WINDOW
RMS_EPS = 1e-6
NEG_INF = -1e30
MIN_FORGET = 1e-30

LANES = 128
CHUNK = 128
HGRN_ROWS = 512
VMEM_LIMIT = 56 * 1024 * 1024


def _dot(a, b):
    return jnp.dot(a, b, preferred_element_type=F32)


def _dot_nt(a, b):
    return lax.dot_general(a, b, (((1,), (1,)), ((), ())), preferred_element_type=F32)


def _split2(x):
    hi = x.astype(BF16)
    lo = (x - hi.astype(F32)).astype(BF16)
    return jnp.concatenate([hi, lo], axis=1)


def _sum2(y):
    n = y.shape[1] // 2
    return y[:, :n] + y[:, n:]


def _rms(x, g):
    ms = jnp.mean(x * x, axis=-1, keepdims=True)
    return x * lax.rsqrt(ms + RMS_EPS) * g


def _rel_bucket_np(dist):
    exact = NUM_BUCKETS // 2
    d = np.maximum(dist, exact).astype(np.float64)
    large = exact + (np.log(d / exact) / math.log(MAX_DISTANCE / exact) * (NUM_BUCKETS - exact)).astype(np.int64)
    large = np.minimum(large, NUM_BUCKETS - 1)
    return np.where(dist < exact, dist, large).astype(np.int32)


def _band_tables():
    qi = np.arange(BAND_BLOCK)[:, None]
    ki = np.arange(2 * BAND_BLOCK)[None, :]
    idx, add = [], []
    for w, d in DILATED_PATTERNS:
        taps = w // d
        tap = qi + BAND_BLOCK - ki
        valid = (tap >= 0) & (tap <= taps)
        bucket = _rel_bucket_np(np.maximum(tap, 0) * d)
        for first in (False, True):
            v = valid & (ki >= BAND_BLOCK) if first else valid
            idx.append(bucket)
            add.append(np.where(v, 0.0, NEG_INF))
    return np.stack(idx).astype(np.int32), np.stack(add).astype(np.float32)


def _sample_tables(n_past, t_new):
    t = np.arange(t_new)[:, None]
    col = np.arange(n_past + LANES)[None, :]
    dist = np.where(col < n_past, n_past + t - col, t - (col - n_past))
    ok = (dist >= 0) & ((col < n_past) | (col - n_past < t_new))
    cnt = np.zeros(dist.shape, np.int64)
    for w, d in DILATED_PATTERNS:
        cnt += (ok & (dist % d == 0) & (dist <= w)).astype(np.int64)
    idx = _rel_bucket_np(np.maximum(dist, 0))
    add = np.where(cnt > 0, np.log(np.maximum(cnt, 1)), NEG_INF)
    return idx[None].astype(np.int32), add[None].astype(np.float32)


def _levels(seg):
    out, h = [], seg // 2
    while h >= 1:
        out.append(h)
        h //= 2
    return out


def _hgrn_tables(seg):
    r = np.arange(CHUNK)
    same_seg = (r[:, None] // seg) == (r[None, :] // seg)
    tri = (same_seg & (r[None, :] <= r[:, None])).astype(np.float32)
    eye = np.eye(CHUNK, dtype=np.float32)
    blocks = []
    lv = np.full((CHUNK, CHUNK), -1, np.int32)
    levels = _levels(seg)
    for i, h in enumerate(levels):
        ref = r - (r % (2 * h)) + h - 1
        g = np.zeros((CHUNK, CHUNK), np.float32)
        g[r, ref] = 1.0
        second = (r % (2 * h)) >= h
        sign = np.where(second, 1.0, -1.0)[:, None].astype(np.float32)
        blocks.append((sign * (eye - g)) @ tri)
        pair = ((r[:, None] // (2 * h)) == (r[None, :] // (2 * h))) & second[:, None] & (~second)[None, :]
        lv[pair] = i
    lv[r, r] = len(levels)
    seg_end = r - (r % seg) + seg - 1
    gl = np.zeros((CHUNK, CHUNK), np.float32)
    gl[r, seg_end] = 1.0
    blocks.append(tri)
    blocks.append((gl - eye) @ tri)
    mat = np.concatenate(blocks, axis=0)
    assert np.all(np.isin(mat, (-1.0, 0.0, 1.0)))
    return mat, lv


def _bias_kernel(rb_ref, idx_ref, add_ref, out_ref):
    idx = idx_ref[0]
    n_heads = out_ref.shape[1]
    vals = [jnp.zeros(idx.shape, F32)] * n_heads
    for b in range(NUM_BUCKETS):
        hit = idx == b
        vals = [jnp.where(hit, rb_ref[b, h], v) for h, v in enumerate(vals)]
    for h in range(n_heads):
        out_ref[0, h] = add_ref[0] + vals[h]


def _expand_bias(rel_bias, idx, add):
    n, r, c = idx.shape
    n_heads = rel_bias.shape[1]
    return pl.pallas_call(
        _bias_kernel,
        grid=(n,),
        in_specs=[
            pl.BlockSpec(memory_space=pltpu.SMEM),
            pl.BlockSpec((1, r, c), lambda i: (i, 0, 0)),
            pl.BlockSpec((1, r, c), lambda i: (i, 0, 0)),
        ],
        out_specs=pl.BlockSpec((1, n_heads, r, c), lambda i: (i, 0, 0, 0)),
        out_shape=jax.ShapeDtypeStruct((n, n_heads, r, c), F32),
        name="bias_expand",
    )(rel_bias.astype(F32), jnp.asarray(idx), jnp.asarray(add))


N_INPROJ_IN = 4


def _inproj_kernel(*refs, layer, d_rec, d_attn, cache_t):
    x_ref, g_ref, w_ref, lbraw_ref = refs[:N_INPROJ_IN]
    (qr_ref, kr_ref, vr_ref, lf_ref, gate_ref, qpm_ref, kpm_ref, vpm_ref, kout_ref, vout_ref) = refs[-10:]
    h = _rms(x_ref[...], g_ref[...]).astype(BF16)

    raw = lbraw_ref[...]
    e = jnp.exp(raw - jnp.max(raw, axis=0, keepdims=True))
    soft = e / jnp.sum(e, axis=0, keepdims=True)
    cum = soft[0:1]
    for i in range(1, layer + 1):
        cum = cum + soft[i:i + 1]
    lb = cum - soft[0:1]

    def proj(c0, width):
        return _dot(h, w_ref[0, :, c0:c0 + width])

    p = proj(0, d_rec)
    qr_ref[...] = (p * jax.nn.sigmoid(p)).astype(qr_ref.dtype)
    z = proj(d_rec, d_rec)
    t = jnp.exp(-jnp.abs(z))
    r = 1.0 / (1.0 + t)
    sig_pos = jnp.where(z >= 0, r, t * r)
    sig_neg = jnp.where(z >= 0, t * r, r)
    f = lb + (1.0 - lb) * sig_pos
    lf_ref[...] = jnp.log(jnp.maximum(f, MIN_FORGET))
    kr_ref[...] = ((1.0 - lb) * sig_neg).astype(kr_ref.dtype)
    vr_ref[...] = proj(2 * d_rec, d_rec).astype(vr_ref.dtype)
    p = proj(3 * d_rec, d_rec)
    gate_ref[...] = (p * jax.nn.sigmoid(p)).astype(gate_ref.dtype)

    c0 = 4 * d_rec
    qa = proj(c0, d_attn) * (ATTN_HEAD_DIM ** -0.5)
    ka = proj(c0 + d_attn, d_attn)
    va = proj(c0 + 2 * d_attn, d_attn)
    for pr in range(d_attn // LANES):
        sl = slice(pr * LANES, (pr + 1) * LANES)
        qpm_ref[pr] = qa[:, sl]
        kpm_ref[pr] = ka[:, sl]
        vpm_ref[pr] = va[:, sl]
    if cache_t is None:
        kout_ref[...] = ka
        vout_ref[...] = va
    else:
        kout_ref[0, 0] = ka.T
        vout_ref[0, 0] = va.T


def _layer_slab(a, layer):
    return pl.BlockSpec((1,) + a.shape[1:], lambda i: (layer, 0, 0), pipeline_mode=pl.Buffered(1))


def _inproj(x2d, g, w_bf, lb_raw, layer, tm, seq_keep=None, cache_bufs=None):
    t, dm = x2d.shape
    depth = w_bf.shape[0]
    d_rec = lb_raw.shape[1]
    d_attn = (w_bf.shape[2] - 4 * d_rec) // 3
    n_pair = d_attn // LANES
    nat = lambda wd: pl.BlockSpec((tm, wd), lambda i: (i, 0))
    pm = pl.BlockSpec((n_pair, tm, LANES), lambda i: (0, i, 0))
    full = lambda a: pl.BlockSpec(a.shape, lambda i: (0,) * a.ndim)
    f32 = lambda *s: jax.ShapeDtypeStruct(s, F32)
    operands = [x2d, g, w_bf, lb_raw]
    in_specs = [nat(dm), full(g), _layer_slab(w_bf, layer), full(lb_raw)]
    aliases = {}
    if seq_keep is None:
        cache_t, kv_spec, kv_shape = None, nat(d_attn), f32(t, d_attn)
    else:
        seq, keep = seq_keep
        assert seq % tm == 0 and keep % tm == 0
        bps, first = seq // tm, (seq - keep) // tm
        cache_t = (bps, first)
        kv_spec = pl.BlockSpec((1, 1, d_attn, tm),
                               lambda i: (layer, i // bps, 0, jnp.maximum(i % bps - first, 0)))
        kv_shape = f32(depth, t // seq, d_attn, keep)
        assert all(b.shape == kv_shape.shape for b in cache_bufs)
        operands += list(cache_bufs)
        in_specs += [pl.BlockSpec(memory_space=pl.ANY)] * 2
        aliases = {N_INPROJ_IN: 8, N_INPROJ_IN + 1: 9}
    return pl.pallas_call(
        functools.partial(_inproj_kernel, layer=layer, d_rec=d_rec, d_attn=d_attn, cache_t=cache_t),
        grid=(t // tm,),
        in_specs=in_specs,
        out_specs=[nat(d_rec)] * 5 + [pm] * 3 + [kv_spec] * 2,
        out_shape=[f32(t, d_rec), f32(t, d_rec), jax.ShapeDtypeStruct((t, d_rec), BF16), f32(t, d_rec),
                   jax.ShapeDtypeStruct((t, d_rec), BF16)] + [f32(n_pair, t, LANES)] * 3 + [kv_shape] * 2,
        input_output_aliases=aliases,
        compiler_params=pltpu.CompilerParams(dimension_semantics=("arbitrary",), vmem_limit_bytes=VMEM_LIMIT),
        name="inproj",
    )(*operands)


N_HGRN_IN = 9
N_ATTN_IN = 8


CACHE_RING = 3


def _cache_ring_step(ckt_hbm, cvt_hbm, ring_ref, sem_ref, cache_layer, per_step, lin, n_steps):
    def copies(step, slot):
        rows = pl.ds(step * per_step, per_step)
        return [pltpu.make_async_copy(src.at[cache_layer, rows], ring_ref.at[slot, i], sem_ref.at[slot, i])
                for i, src in enumerate((ckt_hbm, cvt_hbm))]

    @pl.when(lin == 0)
    def _():
        for step in range(min(CACHE_RING - 1, n_steps)):
            for cp in copies(step, step):
                cp.start()

    ahead = lin + (CACHE_RING - 1)

    @pl.when(ahead < n_steps)
    def _():
        for cp in copies(ahead, ahead % CACHE_RING):
            cp.start()

    slot = lin % CACHE_RING

    def wait():
        for cp in copies(lin, slot):
            cp.wait()

    return slot, wait


def _hgrn_kernel(*refs, seg, n_chunks, n_attn_heads, cache_layer, n_steps):
    q_ref, k_ref, v_ref, lf_ref, gate_ref, s0_ref, gn_ref, mat_ref, lv_ref = refs[:N_HGRN_IN]
    st_ref = refs[-1]
    nseg = CHUNK // seg
    seg_shift = seg.bit_length() - 1
    levels = _levels(seg)
    nl = len(levels)
    ti = pl.program_id(2)
    if n_attn_heads:
        o_ref, sout_ref, attn_out_ref, ring_ref, sem_ref = refs[-6:-1]
        aq_ref, akn_ref, avn_ref, ckt_hbm, cvt_hbm, abias_ref, ag_ref, abd_ref = refs[N_HGRN_IN:N_HGRN_IN + N_ATTN_IN]
        lin = (pl.program_id(0) * pl.num_programs(1) + pl.program_id(1)) * pl.num_programs(2) + ti
        slot, wait_cache = _cache_ring_step(ckt_hbm, cvt_hbm, ring_ref, sem_ref, cache_layer,
                                            aq_ref.shape[0], lin, n_steps)
        rider = _sample_attn_pieces(aq_ref, akn_ref, avn_ref, ring_ref.at[slot, 0], ring_ref.at[slot, 1],
                                    abias_ref, ag_ref, abd_ref, attn_out_ref, n_heads=n_attn_heads)
    else:
        o_ref, sout_ref = refs[-3:-1]
        rider, wait_cache = iter(()), None

    @pl.when(ti == 0)
    def _():
        st_ref[...] = s0_ref[0, :, 0]

    rowi = lax.broadcasted_iota(jnp.int32, (CHUNK, CHUNK), 0)
    coli = lax.broadcasted_iota(jnp.int32, (CHUNK, CHUNK), 1)
    mat = mat_ref[...]
    lv = lv_ref[...]
    gn = gn_ref[...]

    chunk_rows = [slice(c * CHUNK, (c + 1) * CHUNK) for c in range(n_chunks)]
    lf_split = [_split2(lf_ref[rows, :]) for rows in chunk_rows]
    e_lv = _dot(mat[:nl * CHUNK], jnp.concatenate([s[:, :LANES] for s in lf_split], axis=1))
    e_bl = _dot(mat[nl * CHUNK:], jnp.concatenate(lf_split, axis=1))
    if wait_cache is not None:
        wait_cache()
    next(rider, None)

    intra = []
    for c, rows in enumerate(chunk_rows):
        q = q_ref[rows, :]
        k = k_ref[rows, :]
        a = jnp.zeros((CHUNK, CHUNK), F32)
        for i, h in enumerate(levels):
            second = (rowi & (2 * h - 1)) >= h
            e = e_lv[i * CHUNK:(i + 1) * CHUNK, c * LANES:(c + 1) * LANES]
            x = (jnp.where(second, q, k) * jnp.exp(e)).astype(BF16)
            a = jnp.where(lv == i, _dot_nt(x, x), a)
        a = jnp.where(lv == nl, _dot_nt(q.astype(BF16), k.astype(BF16)), a)
        intra.append(a.astype(BF16))

    states = [st_ref[j] for j in range(nseg)]
    for c, rows in enumerate(chunk_rows):
        q = q_ref[rows, :]
        k = k_ref[rows, :]
        vb = v_ref[rows, :].astype(BF16)
        e = _sum2(e_bl[:, 2 * c * LANES:2 * (c + 1) * LANES])
        eb = jnp.exp(e[:CHUNK])
        qe = (q * eb).astype(BF16)
        kd_t = (k * jnp.exp(e[CHUNK:])).T
        dec_t = eb.T
        o = _dot(intra[c], vb)
        for j in range(nseg):
            s_old = states[j]
            oj = _dot(qe, s_old.astype(BF16))
            kd_j = kd_t
            if nseg > 1:
                oj = jnp.where((rowi >> seg_shift) == j, oj, 0.0)
                kd_j = jnp.where((coli >> seg_shift) == j, kd_t, 0.0)
            o = o + oj
            col = j * seg + seg - 1
            states[j] = s_old * dec_t[:, col:col + 1] + _dot(kd_j.astype(BF16), vb)
        o_ref[rows, :] = (_rms(o, gn) * gate_ref[rows, :]).astype(o_ref.dtype)
        next(rider, None)

    for j in range(nseg):
        st_ref[j] = states[j]
    for _ in rider:
        pass

    @pl.when(ti == pl.num_programs(2) - 1)
    def _():
        sout_ref[0, :, 0] = st_ref[...]


def _hgrn(qr, kr, vr, lf, gate, s0_all, layer, g_norm, seg, rows_per_step, steps_per_seq,
          state_buf, out_layer, sample_attn=None):
    t, d_rec = qr.shape
    n_heads = d_rec // LANES
    nseg = CHUNK // seg
    n_seq = s0_all.shape[1]
    mat, lv = _hgrn_tables(seg)
    grid = (n_seq // nseg, n_heads, steps_per_seq)
    tok = pl.BlockSpec((rows_per_step, LANES), lambda bi, h, ti: (bi * steps_per_seq + ti, h))
    st_in = pl.BlockSpec((1, nseg, 1, REC_EXPAND, REC_DV), lambda bi, h, ti: (layer, bi, h, 0, 0))
    st_out = pl.BlockSpec((1, nseg, 1, REC_EXPAND, REC_DV), lambda bi, h, ti: (out_layer, bi, h, 0, 0))
    full = lambda a: pl.BlockSpec(a.shape, lambda bi, h, ti: (0,) * a.ndim)
    mat_j, lv_j = jnp.asarray(mat, BF16), jnp.asarray(lv)
    operands = [qr, kr, vr, lf, gate, s0_all, g_norm, mat_j, lv_j]
    in_specs = [tok] * 5 + [st_in, pl.BlockSpec((1, LANES), lambda bi, h, ti: (0, h)), full(mat_j), full(lv_j)]
    out_specs = [tok, st_out]
    assert state_buf.shape[1:] == (n_seq, n_heads, REC_EXPAND, REC_DV)
    out_shape = [jax.ShapeDtypeStruct((t, d_rec), BF16), jax.ShapeDtypeStruct(state_buf.shape, F32)]
    n_attn_heads, cache_layer = 0, None
    n_steps = grid[0] * grid[1] * grid[2]
    scratch = []
    semantics = ("parallel", "parallel", "arbitrary")
    if sample_attn is not None:
        q3, kn3, vn3, cache_kt, cache_vt, cache_layer, bias, g_attn = sample_attn
        n_seq_s, t_new, width = q3.shape
        n_past = cache_kt.shape[3]
        assert n_seq_s % n_steps == 0
        per_step = n_seq_s // n_steps
        step = lambda bi, h, ti: (bi * n_heads + h) * steps_per_seq + ti
        tok_s = pl.BlockSpec((per_step, t_new, width), lambda bi, h, ti: (step(bi, h, ti), 0, 0))
        in_hbm = pl.BlockSpec(memory_space=pl.ANY)
        bd = _head_mean_matrix(width)
        operands += [q3, kn3, vn3, cache_kt, cache_vt, bias, g_attn, bd]
        in_specs += [tok_s, tok_s, tok_s, in_hbm, in_hbm, full(bias), full(g_attn), full(bd)]
        out_specs.append(tok_s)
        out_shape.append(jax.ShapeDtypeStruct(q3.shape, F32))
        n_attn_heads = width // ATTN_HEAD_DIM
        scratch = [pltpu.VMEM((CACHE_RING, 2, per_step, width, n_past), F32),
                   pltpu.SemaphoreType.DMA((CACHE_RING, 2))]
        semantics = ("arbitrary",) * 3
    aliases = {len(operands): 1}
    operands.append(state_buf)
    in_specs.append(pl.BlockSpec(memory_space=pl.ANY))
    return pl.pallas_call(
        functools.partial(_hgrn_kernel, seg=seg, n_chunks=rows_per_step // CHUNK, n_attn_heads=n_attn_heads,
                          cache_layer=cache_layer, n_steps=n_steps),
        grid=grid,
        in_specs=in_specs,
        out_specs=out_specs,
        out_shape=out_shape,
        input_output_aliases=aliases,
        scratch_shapes=scratch + [pltpu.VMEM((nseg, REC_EXPAND, REC_DV), F32)],
        compiler_params=pltpu.CompilerParams(dimension_semantics=semantics, vmem_limit_bytes=VMEM_LIMIT),
        name="hgrn",
    )(*operands)


ATTN_UNROLL = 4
MERGE_UNROLL = 4


def _prompt_attn_kernel(q_ref, k_ref, v_ref, bias_ref, g_ref, bd_ref, out_ref,
                        oacc_ref, macc_ref, lacc_ref, p_ref, *, seq):
    lane = lax.broadcasted_iota(jnp.int32, (1, LANES), 1)
    lo_half = lane < ATTN_HEAD_DIM
    keeps = (lo_half, jnp.logical_not(lo_half))
    one = jnp.ones((), BF16)
    bb = BAND_BLOCK

    def runs(gi, d, nb):
        run = min(ATTN_UNROLL, nb)
        runs_per_res = nb // run
        for w in range(ATTN_UNROLL // run):
            n = gi * (ATTN_UNROLL // run) + w
            r = n // runs_per_res
            i0 = (n % runs_per_res) * run
            base = r + i0 * (d * bb)
            psl = pl.ds(jnp.maximum(base - d * bb, r), bb, stride=d)
            yield base, i0, psl, run, w * run

    def stage_scores(pi, d, nb, gi, slot):
        for base, i0, psl, run, u0 in runs(gi, d, nb):
            k_prev = k_ref[0, psl, :].astype(BF16)
            for j in range(run):
                qsl = pl.ds(base + j * (d * bb), bb, stride=d)
                k_cur = k_ref[0, qsl, :].astype(BF16)
                k2 = jnp.concatenate([k_prev, k_cur], axis=0)
                q2 = q_ref[0, qsl, :]
                variant = jnp.where(i0 == 0, 1, 0) if j == 0 else 0
                ms = []
                for hh in range(2):
                    qm = jnp.where(keeps[hh], q2, 0.0).astype(BF16)
                    s = _dot_nt(qm, k2) + bias_ref[pi, variant, hh]
                    m = jnp.max(s, axis=-1, keepdims=True)
                    p_ref[slot, 2 * (u0 + j) + hh] = jnp.exp(s - m).astype(BF16)
                    ms.append(m)
                macc_ref[pi, qsl, :] = jnp.where(lo_half, ms[0], ms[1])
                k_prev = k_cur

    def stage_values(pi, d, nb, gi, slot):
        for base, i0, psl, run, u0 in runs(gi, d, nb):
            v_prev = v_ref[0, psl, :].astype(BF16)
            for j in range(run):
                qsl = pl.ds(base + j * (d * bb), bb, stride=d)
                v_cur = v_ref[0, qsl, :].astype(BF16)
                v2 = jnp.concatenate([v_prev, v_cur], axis=0)
                pvs = [_dot(p_ref[slot, 2 * (u0 + j) + hh], jnp.where(keeps[hh], v2, one)) for hh in range(2)]
                oacc_ref[pi, qsl, :] = jnp.where(lo_half, pvs[0], pvs[1])
                lacc_ref[pi, qsl, :] = pltpu.roll(jnp.where(lo_half, pvs[1], pvs[0]), ATTN_HEAD_DIM, axis=1)
                v_prev = v_cur

    for pi, (_, d) in enumerate(DILATED_PATTERNS):
        nb = seq // (d * bb)
        n_groups = (d * nb) // ATTN_UNROLL
        stage_scores(pi, d, nb, 0, 0)

        def overlapped(k, carry, pi=pi, d=d, nb=nb):
            stage_scores(pi, d, nb, 2 * k + 1, 1)
            stage_values(pi, d, nb, 2 * k, 0)
            stage_scores(pi, d, nb, 2 * k + 2, 0)
            stage_values(pi, d, nb, 2 * k + 1, 1)
            return carry

        lax.fori_loop(0, n_groups // 2 - 1, overlapped, 0)
        stage_scores(pi, d, nb, n_groups - 1, 1)
        stage_values(pi, d, nb, n_groups - 2, 0)
        stage_values(pi, d, nb, n_groups - 1, 1)

    g = g_ref[...]
    bd = bd_ref[...]

    def merge(gi, carry):
        for u in range(MERGE_UNROLL):
            rows = pl.ds(pl.multiple_of((gi * MERGE_UNROLL + u) * bb, bb), bb)
            ms = [macc_ref[pi, rows, :] for pi in range(len(DILATED_PATTERNS))]
            mx = functools.reduce(jnp.maximum, ms)
            ws = [jnp.exp(m - mx) for m in ms]
            add = lambda a, b: a + b
            den = functools.reduce(add, [w * lacc_ref[pi, rows, :] for pi, w in enumerate(ws)])
            num = functools.reduce(add, [w * oacc_ref[pi, rows, :] for pi, w in enumerate(ws)])
            o = num / den
            ms = _dot((o * o).astype(BF16), bd)
            out_ref[0, rows, :] = (o * lax.rsqrt(ms + RMS_EPS) * g).astype(out_ref.dtype)
        return carry

    lax.fori_loop(0, seq // (bb * MERGE_UNROLL), merge, 0)


def _head_mean_matrix(width):
    r = np.arange(width)
    return jnp.asarray(((r[:, None] // ATTN_HEAD_DIM) == (r[None, :] // ATTN_HEAD_DIM)) / ATTN_HEAD_DIM, BF16)


def _prompt_attention(qpm, kpm, vpm, band_bias, g_attn, n_seq, seq):
    n_pair = qpm.shape[0]
    n_pat = len(DILATED_PATTERNS)
    for _, d in DILATED_PATTERNS:
        assert seq % (d * BAND_BLOCK) == 0 and (seq // BAND_BLOCK) % (2 * ATTN_UNROLL) == 0
        nb = seq // (d * BAND_BLOCK)
        assert nb % min(ATTN_UNROLL, nb) == 0 and ATTN_UNROLL % min(ATTN_UNROLL, nb) == 0
    assert (seq // BAND_BLOCK) % MERGE_UNROLL == 0
    tok = pl.BlockSpec((1, seq, LANES), lambda p, b: (p, b, 0))
    bd = _head_mean_matrix(LANES)
    return pl.pallas_call(
        functools.partial(_prompt_attn_kernel, seq=seq),
        grid=(n_pair, n_seq),
        in_specs=[tok, tok, tok,
                  pl.BlockSpec((n_pat, 2, 2, BAND_BLOCK, 2 * BAND_BLOCK), lambda p, b: (0, 0, p, 0, 0)),
                  pl.BlockSpec((1, LANES), lambda p, b: (0, p)),
                  pl.BlockSpec(bd.shape, lambda p, b: (0, 0))],
        out_specs=tok,
        out_shape=jax.ShapeDtypeStruct(qpm.shape, BF16),
        scratch_shapes=[pltpu.VMEM((n_pat, seq, LANES), F32)] * 3
        + [pltpu.VMEM((2, 2 * ATTN_UNROLL, BAND_BLOCK, 2 * BAND_BLOCK), BF16)],
        compiler_params=pltpu.CompilerParams(
            dimension_semantics=("parallel", "parallel"), vmem_limit_bytes=VMEM_LIMIT),
        name="prompt_attn",
    )(qpm, kpm, vpm, band_bias, g_attn, bd)


def _sample_attn_pieces(q_ref, kn_ref, vn_ref, ckt_ref, cvt_ref, bias_ref, g_ref, bd_ref, o_ref, *, n_heads):
    n_seq, t_new, width = q_ref.shape
    n_past = ckt_ref.shape[2]
    rows = n_heads * t_new
    own = (lax.broadcasted_iota(jnp.int32, (rows, width), 0) // t_new
           == lax.broadcasted_iota(jnp.int32, (rows, width), 1) // ATTN_HEAD_DIM)
    pad = jnp.zeros((LANES - t_new, width), F32)
    for s in range(n_seq):
        qbd = jnp.where(own, jnp.concatenate([q_ref[s]] * n_heads, axis=0), 0.0).astype(BF16)
        kn = jnp.concatenate([kn_ref[s], pad], axis=0).astype(BF16)
        vn = jnp.concatenate([vn_ref[s], pad], axis=0).astype(BF16)
        s_c = _dot(qbd, ckt_ref[s].astype(BF16)) + bias_ref[:, :n_past]
        s_n = _dot_nt(qbd, kn) + bias_ref[:, n_past:]
        m = jnp.maximum(jnp.max(s_c, axis=-1, keepdims=True), jnp.max(s_n, axis=-1, keepdims=True))
        p_c = jnp.exp(s_c - m)
        p_n = jnp.exp(s_n - m)
        l = jnp.sum(p_c, axis=-1, keepdims=True) + jnp.sum(p_n, axis=-1, keepdims=True)
        p_cb = p_c.astype(BF16)
        yield
        parts = []
        for f0 in range(0, width, LANES):
            parts.append(_dot_nt(p_cb, cvt_ref[s, f0:f0 + LANES, :].astype(BF16)))
            if f0 + LANES < width:
                yield
        acc = jnp.concatenate(parts, axis=1) + _dot(p_n.astype(BF16), vn)
        o = jnp.where(own, acc * (1.0 / l), 0.0)
        ms = _dot((o * o).astype(BF16), bd_ref[...])
        o = o * lax.rsqrt(ms + RMS_EPS) * g_ref[...]
        out = o[0:t_new]
        for h in range(1, n_heads):
            out = out + o[h * t_new:(h + 1) * t_new]
        o_ref[s] = out


def _mlp_kernel(x_ref, rec_ref, attn_ref, wout_ref, wup_ref, wdn_ref, gpost_ref, gpre_ref, gmpost_ref,
                y_ref, *, ff_chunk):
    d_rec = rec_ref.shape[1]
    mixed = _dot(rec_ref[...].astype(BF16), wout_ref[0, 0:d_rec, :])
    for p in range(attn_ref.shape[0]):
        r0 = d_rec + p * LANES
        mixed = mixed + _dot(attn_ref[p].astype(BF16), wout_ref[0, r0:r0 + LANES, :])
    x1 = x_ref[...] + _rms(mixed, gpost_ref[...])
    h = _rms(x1, gpre_ref[...]).astype(BF16)
    acc = jnp.zeros(x1.shape, F32)
    for c0 in range(0, wup_ref.shape[2], ff_chunk):
        u = jnp.square(jnp.maximum(_dot(h, wup_ref[0, :, c0:c0 + ff_chunk]), 0.0))
        acc = acc + _dot(u.astype(BF16), wdn_ref[0, c0:c0 + ff_chunk, :])
    y_ref[...] = x1 + _rms(acc, gmpost_ref[...])


def _mlp(x2d, rec, attn_pm, wout, wup, wdn, layer, g_post, g_pre, g_mpost, tm):
    t, dm = x2d.shape
    n_pair = attn_pm.shape[0]
    row = lambda wd: pl.BlockSpec((tm, wd), lambda i: (i, 0))
    once = lambda a: pl.BlockSpec(a.shape, lambda i: (0,) * a.ndim, pipeline_mode=pl.Buffered(1))
    return pl.pallas_call(
        functools.partial(_mlp_kernel, ff_chunk=1024),
        grid=(t // tm,),
        in_specs=[row(dm), row(rec.shape[1]), pl.BlockSpec((n_pair, tm, LANES), lambda i: (0, i, 0)),
                  _layer_slab(wout, layer), _layer_slab(wup, layer), _layer_slab(wdn, layer),
                  once(g_post), once(g_pre), once(g_mpost)],
        out_specs=row(dm),
        out_shape=jax.ShapeDtypeStruct((t, dm), F32),
        compiler_params=pltpu.CompilerParams(dimension_semantics=("parallel",), vmem_limit_bytes=VMEM_LIMIT),
        name="mlp",
    )(x2d, rec, attn_pm, wout, wup, wdn, g_post, g_pre, g_mpost)


def kernel(x_prompt, x_sample, state_hgrn, cache_k, cache_v, rel_bias, lb_raw, w_in, w_out, w_up, w_down,
           g_mix_pre, g_mix_post, g_mlp_pre, g_mlp_post, g_rec_out, g_attn_out):
    n_p, seq, dm = x_prompt.shape
    n_s, t_new, _ = x_sample.shape
    depth = w_in.shape[0]
    n_past = cache_k.shape[2]
    n_heads, dh = cache_k.shape[3], cache_k.shape[4]
    d_attn = n_heads * dh
    d_rec = lb_raw.shape[1]
    n_rec_heads = d_rec // REC_EXPAND
    keep = min(MAX_WINDOW, seq)
    tm = 512
    tm_s = min(tm, n_s * t_new)
    assert dh == ATTN_HEAD_DIM and CHUNK % t_new == 0 and seq % tm == 0
    assert (n_s * t_new) % CHUNK == 0 and (n_s * t_new) % tm_s == 0

    band_idx, band_add = _band_tables()
    band_bias = _expand_bias(rel_bias, band_idx, band_add)
    band_bias = band_bias.reshape(len(DILATED_PATTERNS), 2, n_heads, BAND_BLOCK, 2 * BAND_BLOCK)
    s_idx, s_add = _sample_tables(n_past, t_new)
    sample_bias = _expand_bias(rel_bias, s_idx, s_add).reshape(n_heads * t_new, n_past + LANES)

    w_in_b, w_out_b = w_in.astype(BF16), w_out.astype(BF16)
    w_up_b, w_dn_b = w_up.astype(BF16), w_down.astype(BF16)
    row = lambda a, l: a[l][None, :]

    cache_kt = cache_k.transpose(0, 1, 3, 4, 2).reshape(depth, n_s, d_attn, n_past).astype(F32)
    cache_vt = cache_v.transpose(0, 1, 3, 4, 2).reshape(depth, n_s, d_attn, n_past).astype(F32)

    yp = x_prompt.reshape(n_p * seq, dm)
    ys = x_sample.reshape(n_s * t_new, dm)
    zeros_state = jnp.zeros((1, n_p, n_rec_heads, REC_EXPAND, REC_DV), F32)
    state_in = state_hgrn.astype(F32)
    cache_p = (jnp.zeros((depth, n_p, d_attn, keep), F32), jnp.zeros((depth, n_p, d_attn, keep), F32))
    rec_p = jnp.zeros((depth, n_p, n_rec_heads, REC_EXPAND, REC_DV), F32)
    rec_s = jnp.zeros((depth, n_s, n_rec_heads, REC_EXPAND, REC_DV), F32)
    k_s, v_s = [], []
    for l in range(depth):
        g_pre, g_rec, g_att = row(g_mix_pre, l), row(g_rec_out, l), row(g_attn_out, l)
        post = (row(g_mix_post, l), row(g_mlp_pre, l), row(g_mlp_post, l))

        qr, kr, vr, lf, gate, qpm, kpm, vpm, kt, vt = _inproj(
            yp, g_pre, w_in_b, lb_raw, l, tm, seq_keep=(seq, keep), cache_bufs=cache_p)
        cache_p = (kt, vt)
        qr_s, kr_s, vr_s, lf_s, gate_s, qpm_s, _, _, knat, vnat = _inproj(ys, g_pre, w_in_b, lb_raw, l, tm_s)
        q3 = qpm_s.transpose(1, 0, 2).reshape(n_s, t_new, d_attn)
        rider = (q3, knat.reshape(n_s, t_new, d_attn), vnat.reshape(n_s, t_new, d_attn),
                 cache_kt, cache_vt, l, sample_bias, g_att)

        rec, rec_p, attn_s = _hgrn(qr, kr, vr, lf, gate, zeros_state, 0, g_rec, CHUNK, HGRN_ROWS,
                                   seq // HGRN_ROWS, rec_p, l, sample_attn=rider)
        attn = _prompt_attention(qpm, kpm, vpm, band_bias, g_att, n_p, seq)
        yp = _mlp(yp, rec, attn, w_out_b, w_up_b, w_dn_b, l, *post, tm)

        rec, rec_s = _hgrn(qr_s, kr_s, vr_s, lf_s, gate_s, state_in, l, g_rec, t_new, CHUNK, 1, rec_s, l)
        attn_pm = attn_s.reshape(n_s * t_new, d_attn // LANES, LANES).transpose(1, 0, 2)
        ys = _mlp(ys, rec, attn_pm, w_out_b, w_up_b, w_dn_b, l, *post, tm_s)
        k_s.append(knat.reshape(n_s, t_new, n_heads, dh))
        v_s.append(vnat.reshape(n_s, t_new, n_heads, dh))

    to_cache = lambda a: a.reshape(depth, n_p, n_heads, dh, keep).transpose(0, 1, 4, 2, 3)
    return (yp.reshape(n_p, seq, dm), ys.reshape(n_s, t_new, dm), rec_p, to_cache(cache_p[0]),
            to_cache(cache_p[1]), rec_s, jnp.stack(k_s), jnp.stack(v_s))
```

```python
import functools
import math

import numpy as np
import jax
import jax.numpy as jnp
from jax import lax
from jax.experimental import pallas as pl
from jax.experimental.pallas import tpu as pltpu

F32 = jnp.float32
BF16 = jnp.bfloat16

ATTN_HEAD_DIM = 64
REC_EXPAND = 128
REC_DV = 128
DILATED_PATTERNS = ((128, 1), (512, 4), (2048, 16))
MAX_WINDOW = max(w for w, _ in DILATED_PATTERNS)
BAND_BLOCK = 128
NUM_BUCKETS = 32
MAX_DISTANCE = MAX_WINDOW
RMS_EPS = 1e-6
NEG_INF = -1e30
MIN_FORGET = 1e-30

LANES = 128
CHUNK = 128
HGRN_ROWS = 512
VMEM_LIMIT = 56 * 1024 * 1024


def _dot(a, b):
    return jnp.dot(a, b, preferred_element_type=F32)


def _dot_nt(a, b):
    return lax.dot_general(a, b, (((1,), (1,)), ((), ())), preferred_element_type=F32)


def _split2(x):
    hi = x.astype(BF16)
    lo = (x - hi.astype(F32)).astype(BF16)
    return jnp.concatenate([hi, lo], axis=1)


def _sum2(y):
    n = y.shape[1] // 2
    return y[:, :n] + y[:, n:]


def _rms(x, g):
    ms = jnp.mean(x * x, axis=-1, keepdims=True)
    return x * lax.rsqrt(ms + RMS_EPS) * g


def _rel_bucket_np(dist):
    exact = NUM_BUCKETS // 2
    d = np.maximum(dist, exact).astype(np.float64)
    large = exact + (np.log(d / exact) / math.log(MAX_DISTANCE / exact) * (NUM_BUCKETS - exact)).astype(np.int64)
    large = np.minimum(large, NUM_BUCKETS - 1)
    return np.where(dist < exact, dist, large).astype(np.int32)


def _band_tables():
    qi = np.arange(BAND_BLOCK)[:, None]
    ki = np.arange(2 * BAND_BLOCK)[None, :]
    idx, add = [], []
    for w, d in DILATED_PATTERNS:
        taps = w // d
        tap = qi + BAND_BLOCK - ki
        valid = (tap >= 0) & (tap <= taps)
        bucket = _rel_bucket_np(np.maximum(tap, 0) * d)
        for first in (False, True):
            v = valid & (ki >= BAND_BLOCK) if first else valid
            idx.append(bucket)
            add.append(np.where(v, 0.0, NEG_INF))
    return np.stack(idx).astype(np.int32), np.stack(add).astype(np.float32)


def _sample_tables(n_past, t_new):
    t = np.arange(t_new)[:, None]
    col = np.arange(n_past + LANES)[None, :]
    dist = np.where(col < n_past, n_past + t - col, t - (col - n_past))
    ok = (dist >= 0) & ((col < n_past) | (col - n_past < t_new))
    cnt = np.zeros(dist.shape, np.int64)
    for w, d in DILATED_PATTERNS:
        cnt += (ok & (dist % d == 0) & (dist <= w)).astype(np.int64)
    idx = _rel_bucket_np(np.maximum(dist, 0))
    add = np.where(cnt > 0, np.log(np.maximum(cnt, 1)), NEG_INF)
    return idx[None].astype(np.int32), add[None].astype(np.float32)


def _levels(seg):
    out, h = [], seg // 2
    while h >= 1:
        out.append(h)
        h //= 2
    return out


def _hgrn_tables(seg):
    r = np.arange(CHUNK)
    same_seg = (r[:, None] // seg) == (r[None, :] // seg)
    tri = (same_seg & (r[None, :] <= r[:, None])).astype(np.float32)
    eye = np.eye(CHUNK, dtype=np.float32)
    blocks = []
    lv = np.full((CHUNK, CHUNK), -1, np.int32)
    levels = _levels(seg)
    for i, h in enumerate(levels):
        ref = r - (r % (2 * h)) + h - 1
        g = np.zeros((CHUNK, CHUNK), np.float32)
        g[r, ref] = 1.0
        second = (r % (2 * h)) >= h
        sign = np.where(second, 1.0, -1.0)[:, None].astype(np.float32)
        blocks.append((sign * (eye - g)) @ tri)
        pair = ((r[:, None] // (2 * h)) == (r[None, :] // (2 * h))) & second[:, None] & (~second)[None, :]
        lv[pair] = i
    lv[r, r] = len(levels)
    seg_end = r - (r % seg) + seg - 1
    gl = np.zeros((CHUNK, CHUNK), np.float32)
    gl[r, seg_end] = 1.0
    blocks.append(tri)
    blocks.append((gl - eye) @ tri)
    mat = np.concatenate(blocks, axis=0)
    assert np.all(np.isin(mat, (-1.0, 0.0, 1.0)))
    return mat, lv


def _bias_kernel(rb_ref, idx_ref, add_ref, out_ref):
    idx = idx_ref[0]
    n_heads = out_ref.shape[1]
    vals = [jnp.zeros(idx.shape, F32)] * n_heads
    for b in range(NUM_BUCKETS):
        hit = idx == b
        vals = [jnp.where(hit, rb_ref[b, h], v) for h, v in enumerate(vals)]
    for h in range(n_heads):
        out_ref[0, h] = add_ref[0] + vals[h]


def _expand_bias(rel_bias, idx, add):
    n, r, c = idx.shape
    n_heads = rel_bias.shape[1]
    return pl.pallas_call(
        _bias_kernel,
        grid=(n,),
        in_specs=[
            pl.BlockSpec(memory_space=pltpu.SMEM),
            pl.BlockSpec((1, r, c), lambda i: (i, 0, 0)),
            pl.BlockSpec((1, r, c), lambda i: (i, 0, 0)),
        ],
        out_specs=pl.BlockSpec((1, n_heads, r, c), lambda i: (i, 0, 0, 0)),
        out_shape=jax.ShapeDtypeStruct((n, n_heads, r, c), F32),
        name="bias_expand",
    )(rel_bias.astype(F32), jnp.asarray(idx), jnp.asarray(add))


N_INPROJ_IN = 4


def _inproj_kernel(*refs, layer, d_rec, d_attn, cache_t):
    x_ref, g_ref, w_ref, lbraw_ref = refs[:N_INPROJ_IN]
    (qr_ref, kr_ref, vr_ref, lf_ref, gate_ref, qpm_ref, kpm_ref, vpm_ref, kout_ref, vout_ref) = refs[-10:]
    h = _rms(x_ref[...], g_ref[...]).astype(BF16)

    raw = lbraw_ref[...]
    e = jnp.exp(raw - jnp.max(raw, axis=0, keepdims=True))
    soft = e / jnp.sum(e, axis=0, keepdims=True)
    cum = soft[0:1]
    for i in range(1, layer + 1):
        cum = cum + soft[i:i + 1]
    lb = cum - soft[0:1]

    def proj(c0, width):
        return _dot(h, w_ref[0, :, c0:c0 + width])

    p = proj(0, d_rec)
    qr_ref[...] = (p * (0.5 + 0.5 * jnp.tanh(0.5 * p))).astype(qr_ref.dtype)
    half_th = 0.5 * jnp.tanh(0.5 * proj(d_rec, d_rec))
    f = lb + (1.0 - lb) * (0.5 + half_th)
    lf_ref[...] = jnp.log(jnp.maximum(f, MIN_FORGET))
    kr_ref[...] = ((1.0 - lb) * (0.5 - half_th)).astype(kr_ref.dtype)
    vr_ref[...] = proj(2 * d_rec, d_rec).astype(vr_ref.dtype)
    p = proj(3 * d_rec, d_rec)
    gate_ref[...] = (p * (0.5 + 0.5 * jnp.tanh(0.5 * p))).astype(gate_ref.dtype)

    c0 = 4 * d_rec
    qa = proj(c0, d_attn) * (ATTN_HEAD_DIM ** -0.5)
    ka = proj(c0 + d_attn, d_attn)
    va = proj(c0 + 2 * d_attn, d_attn)
    for pr in range(d_attn // LANES):
        sl = slice(pr * LANES, (pr + 1) * LANES)
        qpm_ref[pr] = qa[:, sl]
        kpm_ref[pr] = ka[:, sl]
        vpm_ref[pr] = va[:, sl]
    if cache_t is None:
        kout_ref[...] = ka
        vout_ref[...] = va
    else:
        kout_ref[0, 0] = ka.T
        vout_ref[0, 0] = va.T


def _layer_slab(a, layer):
    return pl.BlockSpec((1,) + a.shape[1:], lambda i: (layer, 0, 0), pipeline_mode=pl.Buffered(1))


def _inproj(x2d, g, w_bf, lb_raw, layer, tm, seq_keep=None, cache_bufs=None):
    t, dm = x2d.shape
    depth = w_bf.shape[0]
    d_rec = lb_raw.shape[1]
    d_attn = (w_bf.shape[2] - 4 * d_rec) // 3
    n_pair = d_attn // LANES
    nat = lambda wd: pl.BlockSpec((tm, wd), lambda i: (i, 0))
    pm = pl.BlockSpec((n_pair, tm, LANES), lambda i: (0, i, 0))
    full = lambda a: pl.BlockSpec(a.shape, lambda i: (0,) * a.ndim)
    f32 = lambda *s: jax.ShapeDtypeStruct(s, F32)
    operands = [x2d, g, w_bf, lb_raw]
    in_specs = [nat(dm), full(g), _layer_slab(w_bf, layer), full(lb_raw)]
    aliases = {}
    if seq_keep is None:
        cache_t, kv_spec, kv_shape = None, nat(d_attn), f32(t, d_attn)
    else:
        seq, keep = seq_keep
        assert seq % tm == 0 and keep % tm == 0
        bps, first = seq // tm, (seq - keep) // tm
        cache_t = (bps, first)
        kv_spec = pl.BlockSpec((1, 1, d_attn, tm),
                               lambda i: (layer, i // bps, 0, jnp.maximum(i % bps - first, 0)))
        kv_shape = f32(depth, t // seq, d_attn, keep)
        assert all(b.shape == kv_shape.shape for b in cache_bufs)
        operands += list(cache_bufs)
        in_specs += [pl.BlockSpec(memory_space=pl.ANY)] * 2
        aliases = {N_INPROJ_IN: 8, N_INPROJ_IN + 1: 9}
    return pl.pallas_call(
        functools.partial(_inproj_kernel, layer=layer, d_rec=d_rec, d_attn=d_attn, cache_t=cache_t),
        grid=(t // tm,),
        in_specs=in_specs,
        out_specs=[nat(d_rec)] * 5 + [pm] * 3 + [kv_spec] * 2,
        out_shape=[f32(t, d_rec), f32(t, d_rec), jax.ShapeDtypeStruct((t, d_rec), BF16), f32(t, d_rec),
                   jax.ShapeDtypeStruct((t, d_rec), BF16)] + [f32(n_pair, t, LANES)] * 3 + [kv_shape] * 2,
        input_output_aliases=aliases,
        compiler_params=pltpu.CompilerParams(dimension_semantics=("arbitrary",), vmem_limit_bytes=VMEM_LIMIT),
        name="inproj",
    )(*operands)


N_HGRN_IN = 9
N_ATTN_IN = 8


CACHE_RING = 3


def _cache_ring_step(ckt_hbm, cvt_hbm, ring_ref, sem_ref, cache_layer, per_step, lin, n_steps):
    def copies(step, slot):
        rows = pl.ds(step * per_step, per_step)
        return [pltpu.make_async_copy(src.at[cache_layer, rows], ring_ref.at[slot, i], sem_ref.at[slot, i])
                for i, src in enumerate((ckt_hbm, cvt_hbm))]

    @pl.when(lin == 0)
    def _():
        for step in range(min(CACHE_RING - 1, n_steps)):
            for cp in copies(step, step):
                cp.start()

    ahead = lin + (CACHE_RING - 1)

    @pl.when(ahead < n_steps)
    def _():
        for cp in copies(ahead, ahead % CACHE_RING):
            cp.start()

    slot = lin % CACHE_RING

    def wait():
        for cp in copies(lin, slot):
            cp.wait()

    return slot, wait


def _hgrn_kernel(*refs, seg, n_chunks, n_attn_heads, cache_layer, n_steps):
    q_ref, k_ref, v_ref, lf_ref, gate_ref, s0_ref, gn_ref, mat_ref, lv_ref = refs[:N_HGRN_IN]
    st_ref = refs[-1]
    nseg = CHUNK // seg
    seg_shift = seg.bit_length() - 1
    levels = _levels(seg)
    nl = len(levels)
    ti = pl.program_id(2)
    if n_attn_heads:
        o_ref, sout_ref, attn_out_ref, ring_ref, sem_ref = refs[-6:-1]
        aq_ref, akn_ref, avn_ref, ckt_hbm, cvt_hbm, abias_ref, ag_ref, abd_ref = refs[N_HGRN_IN:N_HGRN_IN + N_ATTN_IN]
        lin = (pl.program_id(0) * pl.num_programs(1) + pl.program_id(1)) * pl.num_programs(2) + ti
        slot, wait_cache = _cache_ring_step(ckt_hbm, cvt_hbm, ring_ref, sem_ref, cache_layer,
                                            aq_ref.shape[0], lin, n_steps)
        rider = _sample_attn_pieces(aq_ref, akn_ref, avn_ref, ring_ref.at[slot, 0], ring_ref.at[slot, 1],
                                    abias_ref, ag_ref, abd_ref, attn_out_ref, n_heads=n_attn_heads)
    else:
        o_ref, sout_ref = refs[-3:-1]
        rider, wait_cache = iter(()), None

    @pl.when(ti == 0)
    def _():
        st_ref[...] = s0_ref[0, :, 0]

    rowi = lax.broadcasted_iota(jnp.int32, (CHUNK, CHUNK), 0)
    coli = lax.broadcasted_iota(jnp.int32, (CHUNK, CHUNK), 1)
    mat = mat_ref[...]
    lv = lv_ref[...]
    gn = gn_ref[...]

    chunk_rows = [slice(c * CHUNK, (c + 1) * CHUNK) for c in range(n_chunks)]
    lf_split = [_split2(lf_ref[rows, :]) for rows in chunk_rows]
    e_lv = _dot(mat[:nl * CHUNK], jnp.concatenate([s[:, :LANES] for s in lf_split], axis=1))
    e_bl = _dot(mat[nl * CHUNK:], jnp.concatenate(lf_split, axis=1))
    if wait_cache is not None:
        wait_cache()
    next(rider, None)

    intra = []
    for c, rows in enumerate(chunk_rows):
        q = q_ref[rows, :]
        k = k_ref[rows, :]
        a = jnp.zeros((CHUNK, CHUNK), F32)
        for i, h in enumerate(levels):
            second = (rowi & (2 * h - 1)) >= h
            e = e_lv[i * CHUNK:(i + 1) * CHUNK, c * LANES:(c + 1) * LANES]
            x = (jnp.where(second, q, k) * jnp.exp(e)).astype(BF16)
            a = jnp.where(lv == i, _dot_nt(x, x), a)
        a = jnp.where(lv == nl, _dot_nt(q.astype(BF16), k.astype(BF16)), a)
        intra.append(a.astype(BF16))

    states = [st_ref[j] for j in range(nseg)]
    for c, rows in enumerate(chunk_rows):
        q = q_ref[rows, :]
        k = k_ref[rows, :]
        vb = v_ref[rows, :].astype(BF16)
        e = _sum2(e_bl[:, 2 * c * LANES:2 * (c + 1) * LANES])
        eb = jnp.exp(e[:CHUNK])
        qe = (q * eb).astype(BF16)
        kd_t = (k * jnp.exp(e[CHUNK:])).T
        dec_t = eb.T
        o = _dot(intra[c], vb)
        for j in range(nseg):
            s_old = states[j]
            oj = _dot(qe, s_old.astype(BF16))
            kd_j = kd_t
            if nseg > 1:
                oj = jnp.where((rowi >> seg_shift) == j, oj, 0.0)
                kd_j = jnp.where((coli >> seg_shift) == j, kd_t, 0.0)
            o = o + oj
            col = j * seg + seg - 1
            states[j] = s_old * dec_t[:, col:col + 1] + _dot(kd_j.astype(BF16), vb)
        o_ref[rows, :] = (_rms(o, gn) * gate_ref[rows, :]).astype(o_ref.dtype)
        next(rider, None)

    for j in range(nseg):
        st_ref[j] = states[j]
    for _ in rider:
        pass

    @pl.when(ti == pl.num_programs(2) - 1)
    def _():
        sout_ref[0, :, 0] = st_ref[...]


def _hgrn(qr, kr, vr, lf, gate, s0_all, layer, g_norm, seg, rows_per_step, steps_per_seq,
          state_buf, out_layer, sample_attn=None):
    t, d_rec = qr.shape
    n_heads = d_rec // LANES
    nseg = CHUNK // seg
    n_seq = s0_all.shape[1]
    mat, lv = _hgrn_tables(seg)
    grid = (n_seq // nseg, n_heads, steps_per_seq)
    tok = pl.BlockSpec((rows_per_step, LANES), lambda bi, h, ti: (bi * steps_per_seq + ti, h))
    st_in = pl.BlockSpec((1, nseg, 1, REC_EXPAND, REC_DV), lambda bi, h, ti: (layer, bi, h, 0, 0))
    st_out = pl.BlockSpec((1, nseg, 1, REC_EXPAND, REC_DV), lambda bi, h, ti: (out_layer, bi, h, 0, 0))
    full = lambda a: pl.BlockSpec(a.shape, lambda bi, h, ti: (0,) * a.ndim)
    mat_j, lv_j = jnp.asarray(mat, BF16), jnp.asarray(lv)
    operands = [qr, kr, vr, lf, gate, s0_all, g_norm, mat_j, lv_j]
    in_specs = [tok] * 5 + [st_in, pl.BlockSpec((1, LANES), lambda bi, h, ti: (0, h)), full(mat_j), full(lv_j)]
    out_specs = [tok, st_out]
    assert state_buf.shape[1:] == (n_seq, n_heads, REC_EXPAND, REC_DV)
    out_shape = [jax.ShapeDtypeStruct((t, d_rec), BF16), jax.ShapeDtypeStruct(state_buf.shape, F32)]
    n_attn_heads, cache_layer = 0, None
    n_steps = grid[0] * grid[1] * grid[2]
    scratch = []
    semantics = ("parallel", "parallel", "arbitrary")
    if sample_attn is not None:
        q3, kn3, vn3, cache_kt, cache_vt, cache_layer, bias, g_attn = sample_attn
        n_seq_s, t_new, width = q3.shape
        n_past = cache_kt.shape[3]
        assert n_seq_s % n_steps == 0
        per_step = n_seq_s // n_steps
        step = lambda bi, h, ti: (bi * n_heads + h) * steps_per_seq + ti
        tok_s = pl.BlockSpec((per_step, t_new, width), lambda bi, h, ti: (step(bi, h, ti), 0, 0))
        in_hbm = pl.BlockSpec(memory_space=pl.ANY)
        bd = _head_mean_matrix(width)
        operands += [q3, kn3, vn3, cache_kt, cache_vt, bias, g_attn, bd]
        in_specs += [tok_s, tok_s, tok_s, in_hbm, in_hbm, full(bias), full(g_attn), full(bd)]
        out_specs.append(tok_s)
        out_shape.append(jax.ShapeDtypeStruct(q3.shape, F32))
        n_attn_heads = width // ATTN_HEAD_DIM
        scratch = [pltpu.VMEM((CACHE_RING, 2, per_step, width, n_past), F32),
                   pltpu.SemaphoreType.DMA((CACHE_RING, 2))]
        semantics = ("arbitrary",) * 3
    aliases = {len(operands): 1}
    operands.append(state_buf)
    in_specs.append(pl.BlockSpec(memory_space=pl.ANY))
    return pl.pallas_call(
        functools.partial(_hgrn_kernel, seg=seg, n_chunks=rows_per_step // CHUNK, n_attn_heads=n_attn_heads,
                          cache_layer=cache_layer, n_steps=n_steps),
        grid=grid,
        in_specs=in_specs,
        out_specs=out_specs,
        out_shape=out_shape,
        input_output_aliases=aliases,
        scratch_shapes=scratch + [pltpu.VMEM((nseg, REC_EXPAND, REC_DV), F32)],
        compiler_params=pltpu.CompilerParams(dimension_semantics=semantics, vmem_limit_bytes=VMEM_LIMIT),
        name="hgrn",
    )(*operands)


ATTN_UNROLL = 4
MERGE_UNROLL = 4


def _prompt_attn_kernel(q_ref, k_ref, v_ref, bias_ref, g_ref, bd_ref, out_ref,
                        oacc_ref, macc_ref, lacc_ref, p_ref, *, seq):
    lane = lax.broadcasted_iota(jnp.int32, (1, LANES), 1)
    lo_half = lane < ATTN_HEAD_DIM
    keeps = (lo_half, jnp.logical_not(lo_half))
    one = jnp.ones((), BF16)
    bb = BAND_BLOCK

    def runs(gi, d, nb):
        run = min(ATTN_UNROLL, nb)
        runs_per_res = nb // run
        for w in range(ATTN_UNROLL // run):
            n = gi * (ATTN_UNROLL // run) + w
            r = n // runs_per_res
            i0 = (n % runs_per_res) * run
            base = r + i0 * (d * bb)
            psl = pl.ds(jnp.maximum(base - d * bb, r), bb, stride=d)
            yield base, i0, psl, run, w * run

    def stage_scores(pi, d, nb, gi, slot):
        for base, i0, psl, run, u0 in runs(gi, d, nb):
            k_prev = k_ref[0, psl, :].astype(BF16)
            for j in range(run):
                qsl = pl.ds(base + j * (d * bb), bb, stride=d)
                k_cur = k_ref[0, qsl, :].astype(BF16)
                k2 = jnp.concatenate([k_prev, k_cur], axis=0)
                q2 = q_ref[0, qsl, :]
                variant = jnp.where(i0 == 0, 1, 0) if j == 0 else 0
                ms = []
                for hh in range(2):
                    qm = jnp.where(keeps[hh], q2, 0.0).astype(BF16)
                    s = _dot_nt(qm, k2) + bias_ref[pi, variant, hh]
                    m = jnp.max(s, axis=-1, keepdims=True)
                    p_ref[slot, 2 * (u0 + j) + hh] = jnp.exp(s - m).astype(BF16)
                    ms.append(m)
                macc_ref[pi, qsl, :] = jnp.where(lo_half, ms[0], ms[1])
                k_prev = k_cur

    def stage_values(pi, d, nb, gi, slot):
        for base, i0, psl, run, u0 in runs(gi, d, nb):
            v_prev = v_ref[0, psl, :].astype(BF16)
            for j in range(run):
                qsl = pl.ds(base + j * (d * bb), bb, stride=d)
                v_cur = v_ref[0, qsl, :].astype(BF16)
                v2 = jnp.concatenate([v_prev, v_cur], axis=0)
                pvs = [_dot(p_ref[slot, 2 * (u0 + j) + hh], jnp.where(keeps[hh], v2, one)) for hh in range(2)]
                oacc_ref[pi, qsl, :] = jnp.where(lo_half, pvs[0], pvs[1])
                lacc_ref[pi, qsl, :] = pltpu.roll(jnp.where(lo_half, pvs[1], pvs[0]), ATTN_HEAD_DIM, axis=1)
                v_prev = v_cur

    for pi, (_, d) in enumerate(DILATED_PATTERNS):
        nb = seq // (d * bb)
        n_groups = (d * nb) // ATTN_UNROLL
        stage_scores(pi, d, nb, 0, 0)

        def overlapped(k, carry, pi=pi, d=d, nb=nb):
            stage_scores(pi, d, nb, 2 * k + 1, 1)
            stage_values(pi, d, nb, 2 * k, 0)
            stage_scores(pi, d, nb, 2 * k + 2, 0)
            stage_values(pi, d, nb, 2 * k + 1, 1)
            return carry

        lax.fori_loop(0, n_groups // 2 - 1, overlapped, 0)
        stage_scores(pi, d, nb, n_groups - 1, 1)
        stage_values(pi, d, nb, n_groups - 2, 0)
        stage_values(pi, d, nb, n_groups - 1, 1)

    g = g_ref[...]
    bd = bd_ref[...]

    def merge(gi, carry):
        for u in range(MERGE_UNROLL):
            rows = pl.ds(pl.multiple_of((gi * MERGE_UNROLL + u) * bb, bb), bb)
            ms = [macc_ref[pi, rows, :] for pi in range(len(DILATED_PATTERNS))]
            mx = functools.reduce(jnp.maximum, ms)
            ws = [jnp.exp(m - mx) for m in ms]
            add = lambda a, b: a + b
            den = functools.reduce(add, [w * lacc_ref[pi, rows, :] for pi, w in enumerate(ws)])
            num = functools.reduce(add, [w * oacc_ref[pi, rows, :] for pi, w in enumerate(ws)])
            o = num / den
            ms = _dot((o * o).astype(BF16), bd)
            out_ref[0, rows, :] = (o * lax.rsqrt(ms + RMS_EPS) * g).astype(out_ref.dtype)
        return carry

    lax.fori_loop(0, seq // (bb * MERGE_UNROLL), merge, 0)


def _head_mean_matrix(width):
    r = np.arange(width)
    return jnp.asarray(((r[:, None] // ATTN_HEAD_DIM) == (r[None, :] // ATTN_HEAD_DIM)) / ATTN_HEAD_DIM, BF16)


def _prompt_attention(qpm, kpm, vpm, band_bias, g_attn, n_seq, seq):
    n_pair = qpm.shape[0]
    n_pat = len(DILATED_PATTERNS)
    for _, d in DILATED_PATTERNS:
        assert seq % (d * BAND_BLOCK) == 0 and (seq // BAND_BLOCK) % (2 * ATTN_UNROLL) == 0
        nb = seq // (d * BAND_BLOCK)
        assert nb % min(ATTN_UNROLL, nb) == 0 and ATTN_UNROLL % min(ATTN_UNROLL, nb) == 0
    assert (seq // BAND_BLOCK) % MERGE_UNROLL == 0
    tok = pl.BlockSpec((1, seq, LANES), lambda p, b: (p, b, 0))
    bd = _head_mean_matrix(LANES)
    return pl.pallas_call(
        functools.partial(_prompt_attn_kernel, seq=seq),
        grid=(n_pair, n_seq),
        in_specs=[tok, tok, tok,
                  pl.BlockSpec((n_pat, 2, 2, BAND_BLOCK, 2 * BAND_BLOCK), lambda p, b: (0, 0, p, 0, 0)),
                  pl.BlockSpec((1, LANES), lambda p, b: (0, p)),
                  pl.BlockSpec(bd.shape, lambda p, b: (0, 0))],
        out_specs=tok,
        out_shape=jax.ShapeDtypeStruct(qpm.shape, BF16),
        scratch_shapes=[pltpu.VMEM((n_pat, seq, LANES), F32)] * 3
        + [pltpu.VMEM((2, 2 * ATTN_UNROLL, BAND_BLOCK, 2 * BAND_BLOCK), BF16)],
        compiler_params=pltpu.CompilerParams(
            dimension_semantics=("parallel", "parallel"), vmem_limit_bytes=VMEM_LIMIT),
        name="prompt_attn",
    )(qpm, kpm, vpm, band_bias, g_attn, bd)


def _sample_attn_pieces(q_ref, kn_ref, vn_ref, ckt_ref, cvt_ref, bias_ref, g_ref, bd_ref, o_ref, *, n_heads):
    n_seq, t_new, width = q_ref.shape
    n_past = ckt_ref.shape[2]
    rows = n_heads * t_new
    own = (lax.broadcasted_iota(jnp.int32, (rows, width), 0) // t_new
           == lax.broadcasted_iota(jnp.int32, (rows, width), 1) // ATTN_HEAD_DIM)
    pad = jnp.zeros((LANES - t_new, width), F32)
    for s in range(n_seq):
        qbd = jnp.where(own, jnp.concatenate([q_ref[s]] * n_heads, axis=0), 0.0).astype(BF16)
        kn = jnp.concatenate([kn_ref[s], pad], axis=0).astype(BF16)
        vn = jnp.concatenate([vn_ref[s], pad], axis=0).astype(BF16)
        s_c = _dot(qbd, ckt_ref[s].astype(BF16)) + bias_ref[:, :n_past]
        s_n = _dot_nt(qbd, kn) + bias_ref[:, n_past:]
        m = jnp.maximum(jnp.max(s_c, axis=-1, keepdims=True), jnp.max(s_n, axis=-1, keepdims=True))
        p_c = jnp.exp(s_c - m)
        p_n = jnp.exp(s_n - m)
        l = jnp.sum(p_c, axis=-1, keepdims=True) + jnp.sum(p_n, axis=-1, keepdims=True)
        p_cb = p_c.astype(BF16)
        yield
        parts = []
        for f0 in range(0, width, LANES):
            parts.append(_dot_nt(p_cb, cvt_ref[s, f0:f0 + LANES, :].astype(BF16)))
            if f0 + LANES < width:
                yield
        acc = jnp.concatenate(parts, axis=1) + _dot(p_n.astype(BF16), vn)
        o = jnp.where(own, acc * (1.0 / l), 0.0)
        ms = _dot((o * o).astype(BF16), bd_ref[...])
        o = o * lax.rsqrt(ms + RMS_EPS) * g_ref[...]
        out = o[0:t_new]
        for h in range(1, n_heads):
            out = out + o[h * t_new:(h + 1) * t_new]
        o_ref[s] = out


def _mlp_kernel(x_ref, rec_ref, attn_ref, wout_ref, wup_ref, wdn_ref, gpost_ref, gpre_ref, gmpost_ref,
                y_ref, *, ff_chunk):
    d_rec = rec_ref.shape[1]
    mixed = _dot(rec_ref[...].astype(BF16), wout_ref[0, 0:d_rec, :])
    for p in range(attn_ref.shape[0]):
        r0 = d_rec + p * LANES
        mixed = mixed + _dot(attn_ref[p].astype(BF16), wout_ref[0, r0:r0 + LANES, :])
    x1 = x_ref[...] + _rms(mixed, gpost_ref[...])
    h = _rms(x1, gpre_ref[...]).astype(BF16)
    acc = jnp.zeros(x1.shape, F32)
    for c0 in range(0, wup_ref.shape[2], ff_chunk):
        u = jnp.square(jnp.maximum(_dot(h, wup_ref[0, :, c0:c0 + ff_chunk]), 0.0))
        acc = acc + _dot(u.astype(BF16), wdn_ref[0, c0:c0 + ff_chunk, :])
    y_ref[...] = x1 + _rms(acc, gmpost_ref[...])


def _mlp(x2d, rec, attn_pm, wout, wup, wdn, layer, g_post, g_pre, g_mpost, tm):
    t, dm = x2d.shape
    n_pair = attn_pm.shape[0]
    row = lambda wd: pl.BlockSpec((tm, wd), lambda i: (i, 0))
    once = lambda a: pl.BlockSpec(a.shape, lambda i: (0,) * a.ndim, pipeline_mode=pl.Buffered(1))
    return pl.pallas_call(
        functools.partial(_mlp_kernel, ff_chunk=1024),
        grid=(t // tm,),
        in_specs=[row(dm), row(rec.shape[1]), pl.BlockSpec((n_pair, tm, LANES), lambda i: (0, i, 0)),
                  _layer_slab(wout, layer), _layer_slab(wup, layer), _layer_slab(wdn, layer),
                  once(g_post), once(g_pre), once(g_mpost)],
        out_specs=row(dm),
        out_shape=jax.ShapeDtypeStruct((t, dm), F32),
        compiler_params=pltpu.CompilerParams(dimension_semantics=("parallel",), vmem_limit_bytes=VMEM_LIMIT),
        name="mlp",
    )(x2d, rec, attn_pm, wout, wup, wdn, g_post, g_pre, g_mpost)


def kernel(x_prompt, x_sample, state_hgrn, cache_k, cache_v, rel_bias, lb_raw, w_in, w_out, w_up, w_down,
           g_mix_pre, g_mix_post, g_mlp_pre, g_mlp_post, g_rec_out, g_attn_out):
    n_p, seq, dm = x_prompt.shape
    n_s, t_new, _ = x_sample.shape
    depth = w_in.shape[0]
    n_past = cache_k.shape[2]
    n_heads, dh = cache_k.shape[3], cache_k.shape[4]
    d_attn = n_heads * dh
    d_rec = lb_raw.shape[1]
    n_rec_heads = d_rec // REC_EXPAND
    keep = min(MAX_WINDOW, seq)
    tm = 512
    tm_s = min(tm, n_s * t_new)
    assert dh == ATTN_HEAD_DIM and CHUNK % t_new == 0 and seq % tm == 0
    assert (n_s * t_new) % CHUNK == 0 and (n_s * t_new) % tm_s == 0

    band_idx, band_add = _band_tables()
    band_bias = _expand_bias(rel_bias, band_idx, band_add)
    band_bias = band_bias.reshape(len(DILATED_PATTERNS), 2, n_heads, BAND_BLOCK, 2 * BAND_BLOCK)
    s_idx, s_add = _sample_tables(n_past, t_new)
    sample_bias = _expand_bias(rel_bias, s_idx, s_add).reshape(n_heads * t_new, n_past + LANES)

    w_in_b, w_out_b = w_in.astype(BF16), w_out.astype(BF16)
    w_up_b, w_dn_b = w_up.astype(BF16), w_down.astype(BF16)
    row = lambda a, l: a[l][None, :]

    cache_kt = cache_k.transpose(0, 1, 3, 4, 2).reshape(depth, n_s, d_attn, n_past).astype(F32)
    cache_vt = cache_v.transpose(0, 1, 3, 4, 2).reshape(depth, n_s, d_attn, n_past).astype(F32)

    yp = x_prompt.reshape(n_p * seq, dm)
    ys = x_sample.reshape(n_s * t_new, dm)
    zeros_state = jnp.zeros((1, n_p, n_rec_heads, REC_EXPAND, REC_DV), F32)
    state_in = state_hgrn.astype(F32)
    cache_p = (jnp.zeros((depth, n_p, d_attn, keep), F32), jnp.zeros((depth, n_p, d_attn, keep), F32))
    rec_p = jnp.zeros((depth, n_p, n_rec_heads, REC_EXPAND, REC_DV), F32)
    rec_s = jnp.zeros((depth, n_s, n_rec_heads, REC_EXPAND, REC_DV), F32)
    k_s, v_s = [], []
    for l in range(depth):
        g_pre, g_rec, g_att = row(g_mix_pre, l), row(g_rec_out, l), row(g_attn_out, l)
        post = (row(g_mix_post, l), row(g_mlp_pre, l), row(g_mlp_post, l))

        qr, kr, vr, lf, gate, qpm, kpm, vpm, kt, vt = _inproj(
            yp, g_pre, w_in_b, lb_raw, l, tm, seq_keep=(seq, keep), cache_bufs=cache_p)
        cache_p = (kt, vt)
        qr_s, kr_s, vr_s, lf_s, gate_s, qpm_s, _, _, knat, vnat = _inproj(ys, g_pre, w_in_b, lb_raw, l, tm_s)
        q3 = qpm_s.transpose(1, 0, 2).reshape(n_s, t_new, d_attn)
        rider = (q3, knat.reshape(n_s, t_new, d_attn), vnat.reshape(n_s, t_new, d_attn),
                 cache_kt, cache_vt, l, sample_bias, g_att)

        rec, rec_p, attn_s = _hgrn(qr, kr, vr, lf, gate, zeros_state, 0, g_rec, CHUNK, HGRN_ROWS,
                                   seq // HGRN_ROWS, rec_p, l, sample_attn=rider)
        attn = _prompt_attention(qpm, kpm, vpm, band_bias, g_att, n_p, seq)
        yp = _mlp(yp, rec, attn, w_out_b, w_up_b, w_dn_b, l, *post, tm)

        rec, rec_s = _hgrn(qr_s, kr_s, vr_s, lf_s, gate_s, state_in, l, g_rec, t_new, CHUNK, 1, rec_s, l)
        attn_pm = attn_s.reshape(n_s * t_new, d_attn // LANES, LANES).transpose(1, 0, 2)
        ys = _mlp(ys, rec, attn_pm, w_out_b, w_up_b, w_dn_b, l, *post, tm_s)
        k_s.append(knat.reshape(n_s, t_new, n_heads, dh))
        v_s.append(vnat.reshape(n_s, t_new, n_heads, dh))

    to_cache = lambda a: a.reshape(depth, n_p, n_heads, dh, keep).transpose(0, 1, 4, 2, 3)
    return (yp.reshape(n_p, seq, dm), ys.reshape(n_s, t_new, dm), rec_p, to_cache(cache_p[0]),
            to_cache(cache_p[1]), rec_s, jnp.stack(k_s), jnp.stack(v_s))
```

```python
import functools
import math

import numpy as np
import jax
import jax.numpy as jnp
from jax import lax
from jax.experimental import pallas as pl
from jax.experimental.pallas import tpu as pltpu

F32 = jnp.float32
BF16 = jnp.bfloat16

ATTN_HEAD_DIM = 64
REC_EXPAND = 128
REC_DV = 128
DILATED_PATTERNS = ((128, 1), (512, 4), (2048, 16))
MAX_WINDOW = max(w for w, _ in DILATED_PATTERNS)
BAND_BLOCK = 128
NUM_BUCKETS = 32
MAX_DISTANCE = MAX_WINDOW
RMS_EPS = 1e-6
NEG_INF = -1e30
MIN_FORGET = 1e-30
LOG2E = math.log2(math.e)

LANES = 128
CHUNK = 128
HGRN_ROWS = 512
VMEM_LIMIT = 56 * 1024 * 1024


def _dot(a, b):
    return jnp.dot(a, b, preferred_element_type=F32)


def _dot_nt(a, b):
    return lax.dot_general(a, b, (((1,), (1,)), ((), ())), preferred_element_type=F32)


def _split2(x):
    hi = x.astype(BF16)
    lo = (x - hi.astype(F32)).astype(BF16)
    return jnp.concatenate([hi, lo], axis=1)


def _sum2(y):
    n = y.shape[1] // 2
    return y[:, :n] + y[:, n:]


def _rms(x, g):
    ms = jnp.mean(x * x, axis=-1, keepdims=True)
    return x * lax.rsqrt(ms + RMS_EPS) * g


def _rel_bucket_np(dist):
    exact = NUM_BUCKETS // 2
    d = np.maximum(dist, exact).astype(np.float64)
    large = exact + (np.log(d / exact) / math.log(MAX_DISTANCE / exact) * (NUM_BUCKETS - exact)).astype(np.int64)
    large = np.minimum(large, NUM_BUCKETS - 1)
    return np.where(dist < exact, dist, large).astype(np.int32)


def _band_tables():
    qi = np.arange(BAND_BLOCK)[:, None]
    ki = np.arange(2 * BAND_BLOCK)[None, :]
    idx, add = [], []
    for w, d in DILATED_PATTERNS:
        taps = w // d
        tap = qi + BAND_BLOCK - ki
        valid = (tap >= 0) & (tap <= taps)
        bucket = _rel_bucket_np(np.maximum(tap, 0) * d)
        for first in (False, True):
            v = valid & (ki >= BAND_BLOCK) if first else valid
            idx.append(bucket)
            add.append(np.where(v, 0.0, NEG_INF))
    return np.stack(idx).astype(np.int32), np.stack(add).astype(np.float32)


def _sample_tables(n_past, t_new):
    t = np.arange(t_new)[:, None]
    col = np.arange(n_past + LANES)[None, :]
    dist = np.where(col < n_past, n_past + t - col, t - (col - n_past))
    ok = (dist >= 0) & ((col < n_past) | (col - n_past < t_new))
    cnt = np.zeros(dist.shape, np.int64)
    for w, d in DILATED_PATTERNS:
        cnt += (ok & (dist % d == 0) & (dist <= w)).astype(np.int64)
    idx = _rel_bucket_np(np.maximum(dist, 0))
    add = np.where(cnt > 0, np.log(np.maximum(cnt, 1)), NEG_INF)
    return idx[None].astype(np.int32), add[None].astype(np.float32)


def _levels(seg):
    out, h = [], seg // 2
    while h >= 1:
        out.append(h)
        h //= 2
    return out


def _hgrn_tables(seg):
    r = np.arange(CHUNK)
    same_seg = (r[:, None] // seg) == (r[None, :] // seg)
    tri = (same_seg & (r[None, :] <= r[:, None])).astype(np.float32)
    eye = np.eye(CHUNK, dtype=np.float32)
    blocks = []
    lv = np.full((CHUNK, CHUNK), -1, np.int32)
    levels = _levels(seg)
    for i, h in enumerate(levels):
        ref = r - (r % (2 * h)) + h - 1
        g = np.zeros((CHUNK, CHUNK), np.float32)
        g[r, ref] = 1.0
        second = (r % (2 * h)) >= h
        sign = np.where(second, 1.0, -1.0)[:, None].astype(np.float32)
        blocks.append((sign * (eye - g)) @ tri)
        pair = ((r[:, None] // (2 * h)) == (r[None, :] // (2 * h))) & second[:, None] & (~second)[None, :]
        lv[pair] = i
    lv[r, r] = len(levels)
    seg_end = r - (r % seg) + seg - 1
    gl = np.zeros((CHUNK, CHUNK), np.float32)
    gl[r, seg_end] = 1.0
    blocks.append(tri)
    blocks.append((gl - eye) @ tri)
    mat = np.concatenate(blocks, axis=0)
    assert np.all(np.isin(mat, (-1.0, 0.0, 1.0)))
    return mat, lv


def _bias_kernel(rb_ref, idx_ref, add_ref, out_ref):
    idx = idx_ref[0]
    n_heads = out_ref.shape[1]
    vals = [jnp.zeros(idx.shape, F32)] * n_heads
    for b in range(NUM_BUCKETS):
        hit = idx == b
        vals = [jnp.where(hit, rb_ref[b, h], v) for h, v in enumerate(vals)]
    for h in range(n_heads):
        out_ref[0, h] = (add_ref[0] + vals[h]) * LOG2E


def _expand_bias(rel_bias, idx, add):
    n, r, c = idx.shape
    n_heads = rel_bias.shape[1]
    return pl.pallas_call(
        _bias_kernel,
        grid=(n,),
        in_specs=[
            pl.BlockSpec(memory_space=pltpu.SMEM),
            pl.BlockSpec((1, r, c), lambda i: (i, 0, 0)),
            pl.BlockSpec((1, r, c), lambda i: (i, 0, 0)),
        ],
        out_specs=pl.BlockSpec((1, n_heads, r, c), lambda i: (i, 0, 0, 0)),
        out_shape=jax.ShapeDtypeStruct((n, n_heads, r, c), F32),
        name="bias_expand",
    )(rel_bias.astype(F32), jnp.asarray(idx), jnp.asarray(add))


N_INPROJ_IN = 4


def _inproj_kernel(*refs, layer, d_rec, d_attn, cache_t):
    x_ref, g_ref, w_ref, lbraw_ref = refs[:N_INPROJ_IN]
    (qr_ref, kr_ref, vr_ref, lf_ref, gate_ref, qpm_ref, kpm_ref, vpm_ref, kout_ref, vout_ref) = refs[-10:]
    h = _rms(x_ref[...], g_ref[...]).astype(BF16)

    raw = lbraw_ref[...]
    e = jnp.exp(raw - jnp.max(raw, axis=0, keepdims=True))
    soft = e / jnp.sum(e, axis=0, keepdims=True)
    cum = soft[0:1]
    for i in range(1, layer + 1):
        cum = cum + soft[i:i + 1]
    lb = cum - soft[0:1]

    def proj(c0, width):
        return _dot(h, w_ref[0, :, c0:c0 + width])

    p = proj(0, d_rec)
    qr_ref[...] = (p * (0.5 + 0.5 * jnp.tanh(0.5 * p))).astype(qr_ref.dtype)
    half_th = 0.5 * jnp.tanh(0.5 * proj(d_rec, d_rec))
    f = lb + (1.0 - lb) * (0.5 + half_th)
    lf_ref[...] = jnp.log(jnp.maximum(f, MIN_FORGET))
    kr_ref[...] = ((1.0 - lb) * (0.5 - half_th)).astype(kr_ref.dtype)
    vr_ref[...] = proj(2 * d_rec, d_rec).astype(vr_ref.dtype)
    p = proj(3 * d_rec, d_rec)
    gate_ref[...] = (p * (0.5 + 0.5 * jnp.tanh(0.5 * p))).astype(gate_ref.dtype)

    c0 = 4 * d_rec
    qa = proj(c0, d_attn) * (ATTN_HEAD_DIM ** -0.5 * LOG2E)
    ka = proj(c0 + d_attn, d_attn)
    va = proj(c0 + 2 * d_attn, d_attn)
    for pr in range(d_attn // LANES):
        sl = slice(pr * LANES, (pr + 1) * LANES)
        qpm_ref[pr] = qa[:, sl]
        kpm_ref[pr] = ka[:, sl]
        vpm_ref[pr] = va[:, sl]
    if cache_t is None:
        kout_ref[...] = ka
        vout_ref[...] = va
    else:
        kout_ref[0, 0] = ka.T
        vout_ref[0, 0] = va.T


def _layer_slab(a, layer):
    return pl.BlockSpec((1,) + a.shape[1:], lambda i: (layer, 0, 0), pipeline_mode=pl.Buffered(1))


def _inproj(x2d, g, w_bf, lb_raw, layer, tm, seq_keep=None, cache_bufs=None):
    t, dm = x2d.shape
    depth = w_bf.shape[0]
    d_rec = lb_raw.shape[1]
    d_attn = (w_bf.shape[2] - 4 * d_rec) // 3
    n_pair = d_attn // LANES
    nat = lambda wd: pl.BlockSpec((tm, wd), lambda i: (i, 0))
    pm = pl.BlockSpec((n_pair, tm, LANES), lambda i: (0, i, 0))
    full = lambda a: pl.BlockSpec(a.shape, lambda i: (0,) * a.ndim)
    f32 = lambda *s: jax.ShapeDtypeStruct(s, F32)
    operands = [x2d, g, w_bf, lb_raw]
    in_specs = [nat(dm), full(g), _layer_slab(w_bf, layer), full(lb_raw)]
    aliases = {}
    if seq_keep is None:
        cache_t, kv_spec, kv_shape = None, nat(d_attn), f32(t, d_attn)
    else:
        seq, keep = seq_keep
        assert seq % tm == 0 and keep % tm == 0
        bps, first = seq // tm, (seq - keep) // tm
        cache_t = (bps, first)
        kv_spec = pl.BlockSpec((1, 1, d_attn, tm),
                               lambda i: (layer, i // bps, 0, jnp.maximum(i % bps - first, 0)))
        kv_shape = f32(depth, t // seq, d_attn, keep)
        assert all(b.shape == kv_shape.shape for b in cache_bufs)
        operands += list(cache_bufs)
        in_specs += [pl.BlockSpec(memory_space=pl.ANY)] * 2
        aliases = {N_INPROJ_IN: 8, N_INPROJ_IN + 1: 9}
    return pl.pallas_call(
        functools.partial(_inproj_kernel, layer=layer, d_rec=d_rec, d_attn=d_attn, cache_t=cache_t),
        grid=(t // tm,),
        in_specs=in_specs,
        out_specs=[nat(d_rec)] * 5 + [pm] * 3 + [kv_spec] * 2,
        out_shape=[f32(t, d_rec), f32(t, d_rec), jax.ShapeDtypeStruct((t, d_rec), BF16), f32(t, d_rec),
                   jax.ShapeDtypeStruct((t, d_rec), BF16)] + [f32(n_pair, t, LANES)] * 3 + [kv_shape] * 2,
        input_output_aliases=aliases,
        compiler_params=pltpu.CompilerParams(dimension_semantics=("arbitrary",), vmem_limit_bytes=VMEM_LIMIT),
        name="inproj",
    )(*operands)


N_HGRN_IN = 9
N_ATTN_IN = 8


CACHE_RING = 3


def _cache_ring_step(ckt_hbm, cvt_hbm, ring_ref, sem_ref, cache_layer, per_step, lin, n_steps):
    def copies(step, slot):
        rows = pl.ds(step * per_step, per_step)
        return [pltpu.make_async_copy(src.at[cache_layer, rows], ring_ref.at[slot, i], sem_ref.at[slot, i])
                for i, src in enumerate((ckt_hbm, cvt_hbm))]

    @pl.when(lin == 0)
    def _():
        for step in range(min(CACHE_RING - 1, n_steps)):
            for cp in copies(step, step):
                cp.start()

    ahead = lin + (CACHE_RING - 1)

    @pl.when(ahead < n_steps)
    def _():
        for cp in copies(ahead, ahead % CACHE_RING):
            cp.start()

    slot = lin % CACHE_RING

    def wait():
        for cp in copies(lin, slot):
            cp.wait()

    return slot, wait


def _hgrn_kernel(*refs, seg, n_chunks, n_attn_heads, cache_layer, n_steps):
    q_ref, k_ref, v_ref, lf_ref, gate_ref, s0_ref, gn_ref, mat_ref, lv_ref = refs[:N_HGRN_IN]
    st_ref = refs[-1]
    nseg = CHUNK // seg
    seg_shift = seg.bit_length() - 1
    levels = _levels(seg)
    nl = len(levels)
    ti = pl.program_id(2)
    if n_attn_heads:
        o_ref, sout_ref, attn_out_ref, ring_ref, sem_ref = refs[-6:-1]
        aq_ref, akn_ref, avn_ref, ckt_hbm, cvt_hbm, abias_ref, ag_ref, abd_ref = refs[N_HGRN_IN:N_HGRN_IN + N_ATTN_IN]
        lin = (pl.program_id(0) * pl.num_programs(1) + pl.program_id(1)) * pl.num_programs(2) + ti
        slot, wait_cache = _cache_ring_step(ckt_hbm, cvt_hbm, ring_ref, sem_ref, cache_layer,
                                            aq_ref.shape[0], lin, n_steps)
        rider = _sample_attn_pieces(aq_ref, akn_ref, avn_ref, ring_ref.at[slot, 0], ring_ref.at[slot, 1],
                                    abias_ref, ag_ref, abd_ref, attn_out_ref, n_heads=n_attn_heads)
    else:
        o_ref, sout_ref = refs[-3:-1]
        rider, wait_cache = iter(()), None

    @pl.when(ti == 0)
    def _():
        st_ref[...] = s0_ref[0, :, 0]

    rowi = lax.broadcasted_iota(jnp.int32, (CHUNK, CHUNK), 0)
    coli = lax.broadcasted_iota(jnp.int32, (CHUNK, CHUNK), 1)
    mat = mat_ref[...]
    lv = lv_ref[...]
    gn = gn_ref[...]

    chunk_rows = [slice(c * CHUNK, (c + 1) * CHUNK) for c in range(n_chunks)]
    lf_split = [_split2(lf_ref[rows, :]) for rows in chunk_rows]
    e_lv = _dot(mat[:nl * CHUNK], jnp.concatenate([s[:, :LANES] for s in lf_split], axis=1))
    e_bl = _dot(mat[nl * CHUNK:], jnp.concatenate(lf_split, axis=1))
    if wait_cache is not None:
        wait_cache()
    next(rider, None)

    intra = []
    for c, rows in enumerate(chunk_rows):
        q = q_ref[rows, :]
        k = k_ref[rows, :]
        a = jnp.zeros((CHUNK, CHUNK), F32)
        for i, h in enumerate(levels):
            second = (rowi & (2 * h - 1)) >= h
            e = e_lv[i * CHUNK:(i + 1) * CHUNK, c * LANES:(c + 1) * LANES]
            x = (jnp.where(second, q, k) * jnp.exp(e)).astype(BF16)
            a = jnp.where(lv == i, _dot_nt(x, x), a)
        a = jnp.where(lv == nl, _dot_nt(q.astype(BF16), k.astype(BF16)), a)
        intra.append(a.astype(BF16))

    states = [st_ref[j] for j in range(nseg)]
    for c, rows in enumerate(chunk_rows):
        q = q_ref[rows, :]
        k = k_ref[rows, :]
        vb = v_ref[rows, :].astype(BF16)
        e = _sum2(e_bl[:, 2 * c * LANES:2 * (c + 1) * LANES])
        eb = jnp.exp(e[:CHUNK])
        qe = (q * eb).astype(BF16)
        kd_t = (k * jnp.exp(e[CHUNK:])).T
        dec_t = eb.T
        o = _dot(intra[c], vb)
        for j in range(nseg):
            s_old = states[j]
            oj = _dot(qe, s_old.astype(BF16))
            kd_j = kd_t
            if nseg > 1:
                oj = jnp.where((rowi >> seg_shift) == j, oj, 0.0)
                kd_j = jnp.where((coli >> seg_shift) == j, kd_t, 0.0)
            o = o + oj
            col = j * seg + seg - 1
            states[j] = s_old * dec_t[:, col:col + 1] + _dot(kd_j.astype(BF16), vb)
        o_ref[rows, :] = (_rms(o, gn) * gate_ref[rows, :]).astype(o_ref.dtype)
        next(rider, None)

    for j in range(nseg):
        st_ref[j] = states[j]
    for _ in rider:
        pass

    @pl.when(ti == pl.num_programs(2) - 1)
    def _():
        sout_ref[0, :, 0] = st_ref[...]


def _hgrn(qr, kr, vr, lf, gate, s0_all, layer, g_norm, seg, rows_per_step, steps_per_seq,
          state_buf, out_layer, sample_attn=None):
    t, d_rec = qr.shape
    n_heads = d_rec // LANES
    nseg = CHUNK // seg
    n_seq = s0_all.shape[1]
    mat, lv = _hgrn_tables(seg)
    grid = (n_seq // nseg, n_heads, steps_per_seq)
    tok = pl.BlockSpec((rows_per_step, LANES), lambda bi, h, ti: (bi * steps_per_seq + ti, h))
    st_in = pl.BlockSpec((1, nseg, 1, REC_EXPAND, REC_DV), lambda bi, h, ti: (layer, bi, h, 0, 0))
    st_out = pl.BlockSpec((1, nseg, 1, REC_EXPAND, REC_DV), lambda bi, h, ti: (out_layer, bi, h, 0, 0))
    full = lambda a: pl.BlockSpec(a.shape, lambda bi, h, ti: (0,) * a.ndim)
    mat_j, lv_j = jnp.asarray(mat, BF16), jnp.asarray(lv)
    operands = [qr, kr, vr, lf, gate, s0_all, g_norm, mat_j, lv_j]
    in_specs = [tok] * 5 + [st_in, pl.BlockSpec((1, LANES), lambda bi, h, ti: (0, h)), full(mat_j), full(lv_j)]
    out_specs = [tok, st_out]
    assert state_buf.shape[1:] == (n_seq, n_heads, REC_EXPAND, REC_DV)
    out_shape = [jax.ShapeDtypeStruct((t, d_rec), BF16), jax.ShapeDtypeStruct(state_buf.shape, F32)]
    n_attn_heads, cache_layer = 0, None
    n_steps = grid[0] * grid[1] * grid[2]
    scratch = []
    semantics = ("parallel", "parallel", "arbitrary")
    if sample_attn is not None:
        q3, kn3, vn3, cache_kt, cache_vt, cache_layer, bias, g_attn = sample_attn
        n_seq_s, t_new, width = q3.shape
        n_past = cache_kt.shape[3]
        assert n_seq_s % n_steps == 0
        per_step = n_seq_s // n_steps
        step = lambda bi, h, ti: (bi * n_heads + h) * steps_per_seq + ti
        tok_s = pl.BlockSpec((per_step, t_new, width), lambda bi, h, ti: (step(bi, h, ti), 0, 0))
        in_hbm = pl.BlockSpec(memory_space=pl.ANY)
        bd = _head_mean_matrix(width)
        operands += [q3, kn3, vn3, cache_kt, cache_vt, bias, g_attn, bd]
        in_specs += [tok_s, tok_s, tok_s, in_hbm, in_hbm, full(bias), full(g_attn), full(bd)]
        out_specs.append(tok_s)
        out_shape.append(jax.ShapeDtypeStruct(q3.shape, F32))
        n_attn_heads = width // ATTN_HEAD_DIM
        scratch = [pltpu.VMEM((CACHE_RING, 2, per_step, width, n_past), F32),
                   pltpu.SemaphoreType.DMA((CACHE_RING, 2))]
        semantics = ("arbitrary",) * 3
    aliases = {len(operands): 1}
    operands.append(state_buf)
    in_specs.append(pl.BlockSpec(memory_space=pl.ANY))
    return pl.pallas_call(
        functools.partial(_hgrn_kernel, seg=seg, n_chunks=rows_per_step // CHUNK, n_attn_heads=n_attn_heads,
                          cache_layer=cache_layer, n_steps=n_steps),
        grid=grid,
        in_specs=in_specs,
        out_specs=out_specs,
        out_shape=out_shape,
        input_output_aliases=aliases,
        scratch_shapes=scratch + [pltpu.VMEM((nseg, REC_EXPAND, REC_DV), F32)],
        compiler_params=pltpu.CompilerParams(dimension_semantics=semantics, vmem_limit_bytes=VMEM_LIMIT),
        name="hgrn",
    )(*operands)


ATTN_UNROLL = 4
MERGE_UNROLL = 4


def _prompt_attn_kernel(q_ref, k_ref, v_ref, bias_ref, g_ref, bd_ref, out_ref,
                        oacc_ref, macc_ref, lacc_ref, p_ref, *, seq):
    lane = lax.broadcasted_iota(jnp.int32, (1, LANES), 1)
    lo_half = lane < ATTN_HEAD_DIM
    keeps = (lo_half, jnp.logical_not(lo_half))
    one = jnp.ones((), BF16)
    bb = BAND_BLOCK

    def runs(gi, d, nb):
        run = min(ATTN_UNROLL, nb)
        runs_per_res = nb // run
        for w in range(ATTN_UNROLL // run):
            n = gi * (ATTN_UNROLL // run) + w
            r = n // runs_per_res
            i0 = (n % runs_per_res) * run
            base = r + i0 * (d * bb)
            psl = pl.ds(jnp.maximum(base - d * bb, r), bb, stride=d)
            yield base, i0, psl, run, w * run

    def stage_scores(pi, d, nb, gi, slot):
        for base, i0, psl, run, u0 in runs(gi, d, nb):
            k_prev = k_ref[0, psl, :].astype(BF16)
            for j in range(run):
                qsl = pl.ds(base + j * (d * bb), bb, stride=d)
                k_cur = k_ref[0, qsl, :].astype(BF16)
                k2 = jnp.concatenate([k_prev, k_cur], axis=0)
                q2 = q_ref[0, qsl, :]
                variant = jnp.where(i0 == 0, 1, 0) if j == 0 else 0
                ms = []
                for hh in range(2):
                    qm = jnp.where(keeps[hh], q2, 0.0).astype(BF16)
                    s = _dot_nt(qm, k2) + bias_ref[pi, variant, hh]
                    m = jnp.max(s, axis=-1, keepdims=True)
                    p_ref[slot, 2 * (u0 + j) + hh] = jnp.exp2(s - m).astype(BF16)
                    ms.append(m)
                macc_ref[pi, qsl, :] = jnp.where(lo_half, ms[0], ms[1])
                k_prev = k_cur

    def stage_values(pi, d, nb, gi, slot):
        for base, i0, psl, run, u0 in runs(gi, d, nb):
            v_prev = v_ref[0, psl, :].astype(BF16)
            for j in range(run):
                qsl = pl.ds(base + j * (d * bb), bb, stride=d)
                v_cur = v_ref[0, qsl, :].astype(BF16)
                v2 = jnp.concatenate([v_prev, v_cur], axis=0)
                pvs = [_dot(p_ref[slot, 2 * (u0 + j) + hh], jnp.where(keeps[hh], v2, one)) for hh in range(2)]
                oacc_ref[pi, qsl, :] = jnp.where(lo_half, pvs[0], pvs[1])
                lacc_ref[pi, qsl, :] = pltpu.roll(jnp.where(lo_half, pvs[1], pvs[0]), ATTN_HEAD_DIM, axis=1)
                v_prev = v_cur

    for pi, (_, d) in enumerate(DILATED_PATTERNS):
        nb = seq // (d * bb)
        n_groups = (d * nb) // ATTN_UNROLL
        stage_scores(pi, d, nb, 0, 0)

        def overlapped(k, carry, pi=pi, d=d, nb=nb):
            stage_scores(pi, d, nb, 2 * k + 1, 1)
            stage_values(pi, d, nb, 2 * k, 0)
            stage_scores(pi, d, nb, 2 * k + 2, 0)
            stage_values(pi, d, nb, 2 * k + 1, 1)
            return carry

        lax.fori_loop(0, n_groups // 2 - 1, overlapped, 0)
        stage_scores(pi, d, nb, n_groups - 1, 1)
        stage_values(pi, d, nb, n_groups - 2, 0)
        stage_values(pi, d, nb, n_groups - 1, 1)

    g = g_ref[...]
    bd = bd_ref[...]

    def merge(gi, carry):
        for u in range(MERGE_UNROLL):
            rows = pl.ds(pl.multiple_of((gi * MERGE_UNROLL + u) * bb, bb), bb)
            ms = [macc_ref[pi, rows, :] for pi in range(len(DILATED_PATTERNS))]
            mx = functools.reduce(jnp.maximum, ms)
            ws = [jnp.exp2(m - mx) for m in ms]
            add = lambda a, b: a + b
            den = functools.reduce(add, [w * lacc_ref[pi, rows, :] for pi, w in enumerate(ws)])
            num = functools.reduce(add, [w * oacc_ref[pi, rows, :] for pi, w in enumerate(ws)])
            o = num / den
            ms = _dot((o * o).astype(BF16), bd)
            out_ref[0, rows, :] = (o * lax.rsqrt(ms + RMS_EPS) * g).astype(out_ref.dtype)
        return carry

    lax.fori_loop(0, seq // (bb * MERGE_UNROLL), merge, 0)


def _head_mean_matrix(width):
    r = np.arange(width)
    return jnp.asarray(((r[:, None] // ATTN_HEAD_DIM) == (r[None, :] // ATTN_HEAD_DIM)) / ATTN_HEAD_DIM, BF16)


def _prompt_attention(qpm, kpm, vpm, band_bias, g_attn, n_seq, seq):
    n_pair = qpm.shape[0]
    n_pat = len(DILATED_PATTERNS)
    for _, d in DILATED_PATTERNS:
        assert seq % (d * BAND_BLOCK) == 0 and (seq // BAND_BLOCK) % (2 * ATTN_UNROLL) == 0
        nb = seq // (d * BAND_BLOCK)
        assert nb % min(ATTN_UNROLL, nb) == 0 and ATTN_UNROLL % min(ATTN_UNROLL, nb) == 0
    assert (seq // BAND_BLOCK) % MERGE_UNROLL == 0
    tok = pl.BlockSpec((1, seq, LANES), lambda p, b: (p, b, 0))
    bd = _head_mean_matrix(LANES)
    return pl.pallas_call(
        functools.partial(_prompt_attn_kernel, seq=seq),
        grid=(n_pair, n_seq),
        in_specs=[tok, tok, tok,
                  pl.BlockSpec((n_pat, 2, 2, BAND_BLOCK, 2 * BAND_BLOCK), lambda p, b: (0, 0, p, 0, 0)),
                  pl.BlockSpec((1, LANES), lambda p, b: (0, p)),
                  pl.BlockSpec(bd.shape, lambda p, b: (0, 0))],
        out_specs=tok,
        out_shape=jax.ShapeDtypeStruct(qpm.shape, BF16),
        scratch_shapes=[pltpu.VMEM((n_pat, seq, LANES), F32)] * 3
        + [pltpu.VMEM((2, 2 * ATTN_UNROLL, BAND_BLOCK, 2 * BAND_BLOCK), BF16)],
        compiler_params=pltpu.CompilerParams(
            dimension_semantics=("parallel", "parallel"), vmem_limit_bytes=VMEM_LIMIT),
        name="prompt_attn",
    )(qpm, kpm, vpm, band_bias, g_attn, bd)


def _sample_attn_pieces(q_ref, kn_ref, vn_ref, ckt_ref, cvt_ref, bias_ref, g_ref, bd_ref, o_ref, *, n_heads):
    n_seq, t_new, width = q_ref.shape
    n_past = ckt_ref.shape[2]
    rows = n_heads * t_new
    own = (lax.broadcasted_iota(jnp.int32, (rows, width), 0) // t_new
           == lax.broadcasted_iota(jnp.int32, (rows, width), 1) // ATTN_HEAD_DIM)
    pad = jnp.zeros((LANES - t_new, width), F32)
    for s in range(n_seq):
        qbd = jnp.where(own, jnp.concatenate([q_ref[s]] * n_heads, axis=0), 0.0).astype(BF16)
        kn = jnp.concatenate([kn_ref[s], pad], axis=0).astype(BF16)
        vn = jnp.concatenate([vn_ref[s], pad], axis=0).astype(BF16)
        s_c = _dot(qbd, ckt_ref[s].astype(BF16)) + bias_ref[:, :n_past]
        s_n = _dot_nt(qbd, kn) + bias_ref[:, n_past:]
        m = jnp.maximum(jnp.max(s_c, axis=-1, keepdims=True), jnp.max(s_n, axis=-1, keepdims=True))
        p_c = jnp.exp2(s_c - m)
        p_n = jnp.exp2(s_n - m)
        l = jnp.sum(p_c, axis=-1, keepdims=True) + jnp.sum(p_n, axis=-1, keepdims=True)
        p_cb = p_c.astype(BF16)
        yield
        parts = []
        for f0 in range(0, width, LANES):
            parts.append(_dot_nt(p_cb, cvt_ref[s, f0:f0 + LANES, :].astype(BF16)))
            if f0 + LANES < width:
                yield
        acc = jnp.concatenate(parts, axis=1) + _dot(p_n.astype(BF16), vn)
        o = jnp.where(own, acc * (1.0 / l), 0.0)
        ms = _dot((o * o).astype(BF16), bd_ref[...])
        o = o * lax.rsqrt(ms + RMS_EPS) * g_ref[...]
        out = o[0:t_new]
        for h in range(1, n_heads):
            out = out + o[h * t_new:(h + 1) * t_new]
        o_ref[s] = out


def _mlp_kernel(x_ref, rec_ref, attn_ref, wout_ref, wup_ref, wdn_ref, gpost_ref, gpre_ref, gmpost_ref,
                y_ref, *, ff_chunk):
    d_rec = rec_ref.shape[1]
    mixed = _dot(rec_ref[...].astype(BF16), wout_ref[0, 0:d_rec, :])
    for p in range(attn_ref.shape[0]):
        r0 = d_rec + p * LANES
        mixed = mixed + _dot(attn_ref[p].astype(BF16), wout_ref[0, r0:r0 + LANES, :])
    x1 = x_ref[...] + _rms(mixed, gpost_ref[...])
    h = _rms(x1, gpre_ref[...]).astype(BF16)
    acc = jnp.zeros(x1.shape, F32)
    for c0 in range(0, wup_ref.shape[2], ff_chunk):
        u = jnp.square(jnp.maximum(_dot(h, wup_ref[0, :, c0:c0 + ff_chunk]), 0.0))
        acc = acc + _dot(u.astype(BF16), wdn_ref[0, c0:c0 + ff_chunk, :])
    y_ref[...] = x1 + _rms(acc, gmpost_ref[...])


def _mlp(x2d, rec, attn_pm, wout, wup, wdn, layer, g_post, g_pre, g_mpost, tm):
    t, dm = x2d.shape
    n_pair = attn_pm.shape[0]
    row = lambda wd: pl.BlockSpec((tm, wd), lambda i: (i, 0))
    once = lambda a: pl.BlockSpec(a.shape, lambda i: (0,) * a.ndim, pipeline_mode=pl.Buffered(1))
    return pl.pallas_call(
        functools.partial(_mlp_kernel, ff_chunk=1024),
        grid=(t // tm,),
        in_specs=[row(dm), row(rec.shape[1]), pl.BlockSpec((n_pair, tm, LANES), lambda i: (0, i, 0)),
                  _layer_slab(wout, layer), _layer_slab(wup, layer), _layer_slab(wdn, layer),
                  once(g_post), once(g_pre), once(g_mpost)],
        out_specs=row(dm),
        out_shape=jax.ShapeDtypeStruct((t, dm), F32),
        compiler_params=pltpu.CompilerParams(dimension_semantics=("parallel",), vmem_limit_bytes=VMEM_LIMIT),
        name="mlp",
    )(x2d, rec, attn_pm, wout, wup, wdn, g_post, g_pre, g_mpost)


def kernel(x_prompt, x_sample, state_hgrn, cache_k, cache_v, rel_bias, lb_raw, w_in, w_out, w_up, w_down,
           g_mix_pre, g_mix_post, g_mlp_pre, g_mlp_post, g_rec_out, g_attn_out):
    n_p, seq, dm = x_prompt.shape
    n_s, t_new, _ = x_sample.shape
    depth = w_in.shape[0]
    n_past = cache_k.shape[2]
    n_heads, dh = cache_k.shape[3], cache_k.shape[4]
    d_attn = n_heads * dh
    d_rec = lb_raw.shape[1]
    n_rec_heads = d_rec // REC_EXPAND
    keep = min(MAX_WINDOW, seq)
    tm = 512
    tm_s = min(tm, n_s * t_new)
    assert dh == ATTN_HEAD_DIM and CHUNK % t_new == 0 and seq % tm == 0
    assert (n_s * t_new) % CHUNK == 0 and (n_s * t_new) % tm_s == 0

    band_idx, band_add = _band_tables()
    band_bias = _expand_bias(rel_bias, band_idx, band_add)
    band_bias = band_bias.reshape(len(DILATED_PATTERNS), 2, n_heads, BAND_BLOCK, 2 * BAND_BLOCK)
    s_idx, s_add = _sample_tables(n_past, t_new)
    sample_bias = _expand_bias(rel_bias, s_idx, s_add).reshape(n_heads * t_new, n_past + LANES)

    w_in_b, w_out_b = w_in.astype(BF16), w_out.astype(BF16)
    w_up_b, w_dn_b = w_up.astype(BF16), w_down.astype(BF16)
    row = lambda a, l: a[l][None, :]

    cache_kt = cache_k.transpose(0, 1, 3, 4, 2).reshape(depth, n_s, d_attn, n_past).astype(F32)
    cache_vt = cache_v.transpose(0, 1, 3, 4, 2).reshape(depth, n_s, d_attn, n_past).astype(F32)

    yp = x_prompt.reshape(n_p * seq, dm)
    ys = x_sample.reshape(n_s * t_new, dm)
    zeros_state = jnp.zeros((1, n_p, n_rec_heads, REC_EXPAND, REC_DV), F32)
    state_in = state_hgrn.astype(F32)
    cache_p = (jnp.zeros((depth, n_p, d_attn, keep), F32), jnp.zeros((depth, n_p, d_attn, keep), F32))
    rec_p = jnp.zeros((depth, n_p, n_rec_heads, REC_EXPAND, REC_DV), F32)
    rec_s = jnp.zeros((depth, n_s, n_rec_heads, REC_EXPAND, REC_DV), F32)
    k_s, v_s = [], []
    for l in range(depth):
        g_pre, g_rec, g_att = row(g_mix_pre, l), row(g_rec_out, l), row(g_attn_out, l)
        post = (row(g_mix_post, l), row(g_mlp_pre, l), row(g_mlp_post, l))

        qr, kr, vr, lf, gate, qpm, kpm, vpm, kt, vt = _inproj(
            yp, g_pre, w_in_b, lb_raw, l, tm, seq_keep=(seq, keep), cache_bufs=cache_p)
        cache_p = (kt, vt)
        qr_s, kr_s, vr_s, lf_s, gate_s, qpm_s, _, _, knat, vnat = _inproj(ys, g_pre, w_in_b, lb_raw, l, tm_s)
        q3 = qpm_s.transpose(1, 0, 2).reshape(n_s, t_new, d_attn)
        rider = (q3, knat.reshape(n_s, t_new, d_attn), vnat.reshape(n_s, t_new, d_attn),
                 cache_kt, cache_vt, l, sample_bias, g_att)

        rec, rec_p, attn_s = _hgrn(qr, kr, vr, lf, gate, zeros_state, 0, g_rec, CHUNK, HGRN_ROWS,
                                   seq // HGRN_ROWS, rec_p, l, sample_attn=rider)
        attn = _prompt_attention(qpm, kpm, vpm, band_bias, g_att, n_p, seq)
        yp = _mlp(yp, rec, attn, w_out_b, w_up_b, w_dn_b, l, *post, tm)

        rec, rec_s = _hgrn(qr_s, kr_s, vr_s, lf_s, gate_s, state_in, l, g_rec, t_new, CHUNK, 1, rec_s, l)
        attn_pm = attn_s.reshape(n_s * t_new, d_attn // LANES, LANES).transpose(1, 0, 2)
        ys = _mlp(ys, rec, attn_pm, w_out_b, w_up_b, w_dn_b, l, *post, tm_s)
        k_s.append(knat.reshape(n_s, t_new, n_heads, dh))
        v_s.append(vnat.reshape(n_s, t_new, n_heads, dh))

    to_cache = lambda a: a.reshape(depth, n_p, n_heads, dh, keep).transpose(0, 1, 4, 2, 3)
    return (yp.reshape(n_p, seq, dm), ys.reshape(n_s, t_new, dm), rec_p, to_cache(cache_p[0]),
            to_cache(cache_p[1]), rec_s, jnp.stack(k_s), jnp.stack(v_s))
```
